```python
import math
import jax
import jax.numpy as jnp
from jax import lax
import numpy as np

D_MODEL = 2048
BATCH = 2
SEQ = 4096
DEPTH = 2
DEC_BATCH = 128
DEC_SEQ = 4
PAST_LEN = 2048
PAGE_SIZE = 128

N_EVEN = (DEPTH + 1) // 2
N_ODD = DEPTH // 2
MIX_HALF = D_MODEL // 2
POOL_WINDOWS = (2, 4, 8, 16)
POOL_GROUPS = len(POOL_WINDOWS)
POOL_GW = MIX_HALF // POOL_GROUPS
POOL_STATE = max(POOL_WINDOWS) - 1
GMLP_CHUNK = 128
GMLP_HEAD_DIM = 128
GMLP_HEADS = MIX_HALF // GMLP_HEAD_DIM
CONV_WIDTH = 3
CONV_DIM = MIX_HALF
FOX_HEAD_DIM = 128
FOX_HEADS = MIX_HALF // FOX_HEAD_DIM
Q_BLOCK = 128
EVEN_IN = 3 * MIX_HALF
ODD_IN = 6 * MIX_HALF + FOX_HEADS
D_FF = 5632
N_EXPERTS = 8
TOP_K = 2
RMS_EPS = 1e-6
FORGET_BIAS_INIT = 4.0

kernel_name = 'hybrid_pool_gmlp_conv_fox_decoder_step'


def rms_norm(x, g):
    xf = x.astype(jnp.float32)
    y = xf * lax.rsqrt(jnp.mean(xf * xf, axis=-1, keepdims=True) + RMS_EPS)
    return (y * g.astype(jnp.float32)).astype(x.dtype)


def swiglu(h, w_gate, w_up, w_down):
    return (jax.nn.silu(h @ w_gate) * (h @ w_up)) @ w_down


def pool_branch(p_ctx, n_ctx, pos0, w_pool, pool_scale):
    n, t, c = p_ctx.shape
    length = t - n_ctx
    pf = p_ctx.astype(jnp.float32)
    cs = jnp.concatenate([jnp.zeros((n, 1, c), jnp.float32), jnp.cumsum(pf, axis=1)], axis=1)
    hi = jnp.arange(length) + n_ctx + 1
    pos = pos0 + jnp.arange(length)
    means = []
    for g, w in enumerate(POOL_WINDOWS):
        lo = jnp.maximum(hi - w, 0)
        csg = cs[:, :, g * POOL_GW:(g + 1) * POOL_GW]
        cnt = jnp.minimum(pos + 1, w).astype(jnp.float32)
        means.append((csg[:, hi] - csg[:, lo]) / cnt[None, :, None])
    d = (jnp.concatenate(means, axis=-1) - pf[:, n_ctx:]).reshape(n, length, POOL_GROUPS, POOL_GW)
    y = jnp.einsum('nlgc,gcd->nlgd', d, w_pool.astype(jnp.float32)).reshape(n, length, c)
    return (y * pool_scale.astype(jnp.float32)).astype(p_ctx.dtype)


def gmlp_branch(u, v, w_s, b_s):
    n, length, c = v.shape
    cl = min(length, GMLP_CHUNK)
    nk = length // cl
    vh = v.reshape(n, nk, cl, GMLP_HEADS, GMLP_HEAD_DIM)
    mask = jnp.tril(jnp.ones((cl, cl), dtype=bool))
    w = jnp.where(mask, w_s[:, :cl, :cl], 0)
    mixed = jnp.einsum('hts,nkshd->nkthd', w, vh) + b_s[:, :cl].T[None, None, :, :, None]
    return u * mixed.reshape(n, length, c)


def even_mixer(xn, pool_ctx, pos0, w_in, w_pool, pool_scale, w_s, b_s, w_out):
    proj = xn @ w_in
    p = proj[..., :MIX_HALF]
    u = proj[..., MIX_HALF:2 * MIX_HALF]
    v = proj[..., 2 * MIX_HALF:]
    if pool_ctx is None:
        p_ctx, n_ctx = p, 0
    else:
        p_ctx, n_ctx = jnp.concatenate([pool_ctx.astype(p.dtype), p], axis=1), pool_ctx.shape[1]
    a = pool_branch(p_ctx, n_ctx, pos0, w_pool, pool_scale)
    b = gmlp_branch(u, v, w_s, b_s)
    out = jnp.concatenate([a, b], axis=-1) @ w_out
    return out, p_ctx[:, -POOL_STATE:], v


def fox_attend(q, c_q, qpos, k, v, c_k, kpos):
    s = jnp.einsum('nqhd,nkhd->nhqk', q.astype(jnp.float32), k) * (1.0 / math.sqrt(FOX_HEAD_DIM))
    s = s + jnp.transpose(c_q, (0, 2, 1))[:, :, :, None] - jnp.transpose(c_k, (0, 2, 1))[:, :, None, :]
    s = jnp.where(kpos[None, :] <= qpos[:, None], s, -jnp.inf)
    p = jax.nn.softmax(s, axis=-1)
    return jnp.einsum('nhqk,nkhd->nqhd', p, v)


def fox_prompt(q, k, v, logf):
    n, length, h, d = q.shape
    c = jnp.cumsum(logf, axis=1)
    kf, vf = k.astype(jnp.float32), v.astype(jnp.float32)
    nb = length // Q_BLOCK
    qb = jnp.transpose(q.reshape(n, nb, Q_BLOCK, h, d), (1, 0, 2, 3, 4))
    cb = jnp.transpose(c.reshape(n, nb, Q_BLOCK, h), (1, 0, 2, 3))
    qpos = jnp.arange(length).reshape(nb, Q_BLOCK)
    kpos = jnp.arange(length)

    def block(args):
        q_blk, c_blk, qp = args
        return fox_attend(q_blk, c_blk, qp, kf, vf, c, kpos)

    out = lax.map(block, (qb, cb, qpos))
    return jnp.transpose(out, (1, 0, 2, 3, 4)).reshape(n, length, h, d)


def fox_sample(q, k, v, logf, k_past, v_past, logf_past):
    n_past = k_past.shape[1]
    length = q.shape[1]
    k_all = jnp.concatenate([k_past.astype(jnp.float32), k.astype(jnp.float32)], axis=1)
    v_all = jnp.concatenate([v_past.astype(jnp.float32), v.astype(jnp.float32)], axis=1)
    c_all = jnp.cumsum(jnp.concatenate([logf_past.astype(jnp.float32), logf], axis=1), axis=1)
    qpos = n_past + jnp.arange(length)
    kpos = jnp.arange(n_past + length)
    return fox_attend(q, c_all[:, n_past:], qpos, k_all, v_all, c_all, kpos)


def odd_mixer(xn, conv_ctx, past, w_in, conv_w, b_forget, w_out):
    n, length, _ = xn.shape
    proj = xn @ w_in
    xc = proj[..., 0:MIX_HALF]
    bg = proj[..., MIX_HALF:2 * MIX_HALF]
    cg = proj[..., 2 * MIX_HALF:3 * MIX_HALF]
    q = proj[..., 3 * MIX_HALF:4 * MIX_HALF].reshape(n, length, FOX_HEADS, FOX_HEAD_DIM)
    k = proj[..., 4 * MIX_HALF:5 * MIX_HALF].reshape(n, length, FOX_HEADS, FOX_HEAD_DIM)
    v = proj[..., 5 * MIX_HALF:6 * MIX_HALF].reshape(n, length, FOX_HEADS, FOX_HEAD_DIM)
    f_logit = proj[..., 6 * MIX_HALF:]
    z = cg * xc
    if conv_ctx is None:
        zc = jnp.concatenate([jnp.zeros((n, CONV_WIDTH - 1, CONV_DIM), z.dtype), z], axis=1)
    else:
        zc = jnp.concatenate([conv_ctx.astype(z.dtype), z], axis=1)
    y = conv_w[0] * zc[:, 0:length]
    for j in range(1, CONV_WIDTH):
        y = y + conv_w[j] * zc[:, j:j + length]
    c_out = bg * y
    logf = jax.nn.log_sigmoid((f_logit + b_forget).astype(jnp.float32))
    if past is None:
        att = fox_prompt(q, k, v, logf)
    else:
        att = fox_sample(q, k, v, logf, *past)
    att = att.astype(c_out.dtype).reshape(n, length, MIX_HALF)
    out = jnp.concatenate([c_out, att], axis=-1) @ w_out
    return out, zc[:, -(CONV_WIDTH - 1):], k, v, logf.astype(xn.dtype)


def moe_ffn(h, w_router, w_gate, w_up, w_down):
    logits = jnp.einsum('nld,de->nle', h.astype(jnp.float32), w_router.astype(jnp.float32))
    top_val, top_idx = lax.top_k(logits, TOP_K)
    gates = jax.nn.softmax(top_val, axis=-1)
    combine = jnp.einsum('nlk,nlke->nle', gates, jax.nn.one_hot(top_idx, N_EXPERTS, dtype=jnp.float32))
    out = jnp.zeros(h.shape, jnp.float32)
    for e in range(N_EXPERTS):
        out = out + combine[..., e:e + 1] * swiglu(h, w_gate[e], w_up[e], w_down[e]).astype(jnp.float32)
    return out.astype(h.dtype)


def setup_inputs(seed: int = 0) -> dict:
    key = jax.random.key(seed)
    ks = list(jax.random.split(key, 40))

    def nrm(shape, scale=1.0):
        return jax.random.normal(ks.pop(), shape, jnp.float32) * scale

    n_pages = PAST_LEN // PAGE_SIZE
    n_used = DEC_BATCH * n_pages
    n_pool = n_used + max(1, n_used // 4)
    d_in = D_MODEL ** -0.5
    inp = {}
    inp['x_prompt'] = nrm((BATCH, SEQ, D_MODEL))
    inp['x_sample'] = nrm((DEC_BATCH, DEC_SEQ, D_MODEL))
    inp['state_pool'] = nrm((N_EVEN, DEC_BATCH, POOL_STATE, MIX_HALF))
    inp['state_conv'] = nrm((N_ODD, DEC_BATCH, CONV_WIDTH - 1, CONV_DIM))
    inp['cache_k'] = nrm((N_ODD, n_pool, PAGE_SIZE, FOX_HEADS, FOX_HEAD_DIM))
    inp['cache_v'] = nrm((N_ODD, n_pool, PAGE_SIZE, FOX_HEADS, FOX_HEAD_DIM))
    inp['cache_logf'] = jax.nn.log_sigmoid(FORGET_BIAS_INIT + nrm((N_ODD, n_pool, PAGE_SIZE, FOX_HEADS)))
    inp['page_table'] = jax.random.permutation(ks.pop(), n_pool)[:n_used].reshape(DEC_BATCH, n_pages).astype(jnp.int32)
    inp['e_norm_mix_pre'] = 1.0 + nrm((N_EVEN, D_MODEL), 0.05)
    inp['e_norm_mix_post'] = 1.0 + nrm((N_EVEN, D_MODEL), 0.05)
    inp['e_norm_ffn_pre'] = 1.0 + nrm((N_EVEN, D_MODEL), 0.05)
    inp['e_norm_ffn_post'] = 1.0 + nrm((N_EVEN, D_MODEL), 0.05)
    inp['e_w_in'] = nrm((N_EVEN, D_MODEL, EVEN_IN), d_in)
    inp['e_w_pool'] = nrm((N_EVEN, POOL_GROUPS, POOL_GW, POOL_GW), POOL_GW ** -0.5)
    inp['e_pool_scale'] = 1.0 + nrm((N_EVEN, MIX_HALF), 0.1)
    inp['e_w_spatial'] = nrm((N_EVEN, GMLP_HEADS, GMLP_CHUNK, GMLP_CHUNK), GMLP_CHUNK ** -0.5)
    inp['e_b_spatial'] = 1.0 + nrm((N_EVEN, GMLP_HEADS, GMLP_CHUNK), 0.1)
    inp['e_w_out'] = nrm((N_EVEN, D_MODEL, D_MODEL), d_in)
    inp['e_ffn_gate'] = nrm((N_EVEN, D_MODEL, D_FF), d_in)
    inp['e_ffn_up'] = nrm((N_EVEN, D_MODEL, D_FF), d_in)
    inp['e_ffn_down'] = nrm((N_EVEN, D_FF, D_MODEL), D_FF ** -0.5)
    inp['o_norm_mix_pre'] = 1.0 + nrm((N_ODD, D_MODEL), 0.05)
    inp['o_norm_mix_post'] = 1.0 + nrm((N_ODD, D_MODEL), 0.05)
    inp['o_norm_ffn_pre'] = 1.0 + nrm((N_ODD, D_MODEL), 0.05)
    inp['o_norm_ffn_post'] = 1.0 + nrm((N_ODD, D_MODEL), 0.05)
    inp['o_w_in'] = nrm((N_ODD, D_MODEL, ODD_IN), d_in)
    inp['o_conv_w'] = nrm((N_ODD, CONV_WIDTH, CONV_DIM), CONV_WIDTH ** -0.5)
    inp['o_b_forget'] = FORGET_BIAS_INIT + nrm((N_ODD, FOX_HEADS), 0.5)
    inp['o_w_out'] = nrm((N_ODD, D_MODEL, D_MODEL), d_in)
    inp['o_w_router'] = nrm((N_ODD, D_MODEL, N_EXPERTS), d_in)
    inp['o_exp_gate'] = nrm((N_ODD, N_EXPERTS, D_MODEL, D_FF), d_in)
    inp['o_exp_up'] = nrm((N_ODD, N_EXPERTS, D_MODEL, D_FF), d_in)
    inp['o_exp_down'] = nrm((N_ODD, N_EXPERTS, D_FF, D_MODEL), D_FF ** -0.5)
    return inp


def reference(x_prompt, x_sample, state_pool, state_conv, cache_k, cache_v, cache_logf, page_table,
              e_norm_mix_pre, e_norm_mix_post, e_norm_ffn_pre, e_norm_ffn_post,
              e_w_in, e_w_pool, e_pool_scale, e_w_spatial, e_b_spatial, e_w_out,
              e_ffn_gate, e_ffn_up, e_ffn_down,
              o_norm_mix_pre, o_norm_mix_post, o_norm_ffn_pre, o_norm_ffn_post,
              o_w_in, o_conv_w, o_b_forget, o_w_out,
              o_w_router, o_exp_gate, o_exp_up, o_exp_down):
    hp, hs = x_prompt, x_sample
    past_len = page_table.shape[1] * cache_k.shape[2]
    n_dec = page_table.shape[0]
    pool_p, pool_s, gv_s, conv_p, conv_s = [], [], [], [], []
    kp_l, vp_l, lp_l, ks_l, vs_l, ls_l = [], [], [], [], [], []
    for layer in range(DEPTH):
        i = layer // 2
        if layer % 2 == 0:
            mp, st_p, _ = even_mixer(rms_norm(hp, e_norm_mix_pre[i]), None, 0, e_w_in[i], e_w_pool[i],
                                     e_pool_scale[i], e_w_spatial[i], e_b_spatial[i], e_w_out[i])
            ms, st_s, v_s = even_mixer(rms_norm(hs, e_norm_mix_pre[i]), state_pool[i], past_len, e_w_in[i],
                                       e_w_pool[i], e_pool_scale[i], e_w_spatial[i], e_b_spatial[i], e_w_out[i])
            hp = hp + rms_norm(mp, e_norm_mix_post[i])
            hs = hs + rms_norm(ms, e_norm_mix_post[i])
            hp = hp + rms_norm(swiglu(rms_norm(hp, e_norm_ffn_pre[i]), e_ffn_gate[i], e_ffn_up[i], e_ffn_down[i]), e_norm_ffn_post[i])
            hs = hs + rms_norm(swiglu(rms_norm(hs, e_norm_ffn_pre[i]), e_ffn_gate[i], e_ffn_up[i], e_ffn_down[i]), e_norm_ffn_post[i])
            pool_p.append(st_p)
            pool_s.append(st_s)
            gv_s.append(v_s)
        else:
            k_past = cache_k[i][page_table].reshape(n_dec, past_len, FOX_HEADS, FOX_HEAD_DIM)
            v_past = cache_v[i][page_table].reshape(n_dec, past_len, FOX_HEADS, FOX_HEAD_DIM)
            l_past = cache_logf[i][page_table].reshape(n_dec, past_len, FOX_HEADS)
            mp, cst_p, k_p, v_p, l_p = odd_mixer(rms_norm(hp, o_norm_mix_pre[i]), None, None,
                                                 o_w_in[i], o_conv_w[i], o_b_forget[i], o_w_out[i])
            ms, cst_s, k_s, v_s, l_s = odd_mixer(rms_norm(hs, o_norm_mix_pre[i]), state_conv[i], (k_past, v_past, l_past),
                                                 o_w_in[i], o_conv_w[i], o_b_forget[i], o_w_out[i])
            hp = hp + rms_norm(mp, o_norm_mix_post[i])
            hs = hs + rms_norm(ms, o_norm_mix_post[i])
            hp = hp + rms_norm(moe_ffn(rms_norm(hp, o_norm_ffn_pre[i]), o_w_router[i], o_exp_gate[i], o_exp_up[i], o_exp_down[i]), o_norm_ffn_post[i])
            hs = hs + rms_norm(moe_ffn(rms_norm(hs, o_norm_ffn_pre[i]), o_w_router[i], o_exp_gate[i], o_exp_up[i], o_exp_down[i]), o_norm_ffn_post[i])
            conv_p.append(cst_p)
            conv_s.append(cst_s)
            kp_l.append(k_p)
            vp_l.append(v_p)
            lp_l.append(l_p)
            ks_l.append(k_s)
            vs_l.append(v_s)
            ls_l.append(l_s)
    return (hp, hs, jnp.stack(pool_p), jnp.stack(pool_s), jnp.stack(gv_s), jnp.stack(conv_p), jnp.stack(conv_s),
            jnp.stack(kp_l), jnp.stack(vp_l), jnp.stack(lp_l), jnp.stack(ks_l), jnp.stack(vs_l), jnp.stack(ls_l))
```

```python
import functools
import math

import jax
import jax.numpy as jnp
from jax import lax
from jax.experimental import pallas as pl
from jax.experimental.pallas import tpu as pltpu

F32, BF16, I32 = jnp.float32, jnp.bfloat16, jnp.int32
RMS_EPS = 1e-6
POOL_WINDOWS = (2, 4, 8, 16)
POOL_HALO = 16
CONV_WIDTH = 3
CONV_HALO = 8
HEAD_DIM = 128
GMLP_CHUNK = 128
TOP_K = 2
LANES = 128
MASKED = -1e30
VMEM_LIMIT_BYTES = 56 * 1024 * 1024
HIGHEST = lax.Precision.HIGHEST


def _params(*semantics):
    return pltpu.CompilerParams(dimension_semantics=semantics, vmem_limit_bytes=VMEM_LIMIT_BYTES)


def _tile(n, pref, mult=8):
    t = min(n, pref)
    t -= t % mult
    while t > mult and n % t:
        t -= mult
    assert t > 0 and n % t == 0, (n, pref, mult)
    return t


def _rms(x, g):
    return x * lax.rsqrt(jnp.mean(x * x, axis=-1, keepdims=True) + RMS_EPS) * g


def _norm_cast_kernel(x_ref, g_ref, o_ref):
    o_ref[...] = _rms(x_ref[...], g_ref[...]).astype(o_ref.dtype)


def _norm_cast(x, g, tm):
    m, d = x.shape
    return pl.pallas_call(
        _norm_cast_kernel,
        grid=(m // tm,),
        in_specs=[pl.BlockSpec((tm, d), lambda i: (i, 0)), pl.BlockSpec((1, d), lambda i: (0, 0))],
        out_specs=pl.BlockSpec((tm, d), lambda i: (i, 0)),
        out_shape=jax.ShapeDtypeStruct((m, d), BF16),
        compiler_params=_params("parallel"),
        name="norm_cast",
    )(x, g.reshape(1, d))


def _mm_kernel(x_ref, w_ref, o_ref):
    o_ref[...] = jnp.dot(x_ref[...], w_ref[...], preferred_element_type=F32)


def _mm(x, w, tm, tn):
    m, k = x.shape
    n = w.shape[1]
    return pl.pallas_call(
        _mm_kernel,
        grid=(n // tn, m // tm),
        in_specs=[pl.BlockSpec((tm, k), lambda j, i: (i, 0)), pl.BlockSpec((k, tn), lambda j, i: (0, j))],
        out_specs=pl.BlockSpec((tm, tn), lambda j, i: (i, j)),
        out_shape=jax.ShapeDtypeStruct((m, n), F32),
        compiler_params=_params("parallel", "parallel"),
        name="proj_in",
    )(x, w)


def _top2(logits, n_experts):
    lane = lax.broadcasted_iota(I32, logits.shape, 1)
    neg_inf = jnp.float32(-jnp.inf)
    l1 = jnp.where(lane < n_experts, logits, neg_inf)
    m1 = jnp.max(l1, axis=-1, keepdims=True)
    i1 = jnp.min(jnp.where(l1 == m1, lane, LANES), axis=-1, keepdims=True)
    l2 = jnp.where(lane == i1, neg_inf, l1)
    m2 = jnp.max(l2, axis=-1, keepdims=True)
    i2 = jnp.min(jnp.where(l2 == m2, lane, LANES), axis=-1, keepdims=True)
    e2 = jnp.exp(m2 - m1)
    g1 = 1.0 / (1.0 + e2)
    g2 = e2 / (1.0 + e2)
    return jnp.where(lane == 0, g1,
                     jnp.where(lane == 1, g2,
                               jnp.where(lane == 2, i1.astype(F32),
                                         jnp.where(lane == 3, i2.astype(F32), 0.0))))


def _proj_out_kernel(a_ref, b_ref, w_ref, h_ref, gpost_ref, gnext_ref, *rest, n_experts):
    half = a_ref.shape[1]
    m = (jnp.dot(a_ref[...], w_ref[:half, :], preferred_element_type=F32)
         + jnp.dot(b_ref[...], w_ref[half:, :], preferred_element_type=F32))
    hn = h_ref[...] + _rms(m, gpost_ref[...])
    xn = _rms(hn, gnext_ref[...])
    if n_experts:
        wr_ref, hn_ref, xn_ref, route_ref = rest
        logits = jnp.dot(xn, wr_ref[...], precision=HIGHEST, preferred_element_type=F32)
        route_ref[...] = _top2(logits, n_experts)
    else:
        hn_ref, xn_ref = rest
    hn_ref[...] = hn
    xn_ref[...] = xn.astype(xn_ref.dtype)


def _proj_out(a, b, w, h, g_post, g_next, tm, w_router=None):
    m, d = h.shape
    half = a.shape[1]
    n_experts = 0 if w_router is None else w_router.shape[1]
    row = lambda i: (i, 0)
    fixed = lambda i: (0, 0)
    in_specs = [pl.BlockSpec((tm, half), row), pl.BlockSpec((tm, half), row), pl.BlockSpec((2 * half, d), fixed),
                pl.BlockSpec((tm, d), row), pl.BlockSpec((1, d), fixed), pl.BlockSpec((1, d), fixed)]
    out_specs = [pl.BlockSpec((tm, d), row), pl.BlockSpec((tm, d), row)]
    out_shape = [jax.ShapeDtypeStruct((m, d), F32), jax.ShapeDtypeStruct((m, d), BF16)]
    args = [a, b, w, h, g_post.reshape(1, d), g_next.reshape(1, d)]
    if n_experts:
        wr = jnp.zeros((d, LANES), F32).at[:, :n_experts].set(w_router.astype(F32))
        in_specs.append(pl.BlockSpec((d, LANES), fixed))
        out_specs.append(pl.BlockSpec((tm, LANES), row))
        out_shape.append(jax.ShapeDtypeStruct((m, LANES), F32))
        args.append(wr)
    return pl.pallas_call(
        functools.partial(_proj_out_kernel, n_experts=n_experts),
        grid=(m // tm,),
        in_specs=in_specs, out_specs=out_specs, out_shape=out_shape,
        compiler_params=_params("parallel"),
        name="proj_out",
    )(*args)


def _norm_residual_kernel(m_ref, h_ref, gpost_ref, gnext_ref, hn_ref, xn_ref):
    hn = h_ref[...] + _rms(m_ref[...], gpost_ref[...])
    hn_ref[...] = hn
    xn_ref[...] = _rms(hn, gnext_ref[...]).astype(xn_ref.dtype)


def _norm_residual(mix, h, g_post, g_next, tm):
    m, d = h.shape
    row = lambda i: (i, 0)
    fixed = lambda i: (0, 0)
    return pl.pallas_call(
        _norm_residual_kernel,
        grid=(m // tm,),
        in_specs=[pl.BlockSpec((tm, d), row), pl.BlockSpec((tm, d), row),
                  pl.BlockSpec((1, d), fixed), pl.BlockSpec((1, d), fixed)],
        out_specs=[pl.BlockSpec((tm, d), row), pl.BlockSpec((tm, d), row)],
        out_shape=[jax.ShapeDtypeStruct((m, d), F32), jax.ShapeDtypeStruct((m, d), BF16)],
        compiler_params=_params("parallel"),
        name="norm_residual",
    )(mix, h, g_post.reshape(1, d), g_next.reshape(1, d))


def _weights_changed(te_ref, t):
    return jnp.logical_or(t == 0, te_ref[t] != te_ref[jnp.maximum(t - 1, 0)])


def _ffn_up_kernel(te_ref, src_ref, nv_ref, x_ref, wg_ref, wu_ref, o_ref, wg_bf, wu_bf):
    t = pl.program_id(1)

    @pl.when(_weights_changed(te_ref, t))
    def _():
        wg_bf[...] = wg_ref[...].astype(BF16)
        wu_bf[...] = wu_ref[...].astype(BF16)

    @pl.when(t < nv_ref[0])
    def _():
        x = x_ref[...]
        g = jnp.dot(x, wg_bf[...], preferred_element_type=F32)
        u = jnp.dot(x, wu_bf[...], preferred_element_type=F32)
        o_ref[...] = (g * jax.nn.sigmoid(g) * u).astype(o_ref.dtype)

    @pl.when(t >= nv_ref[0])
    def _():
        o_ref[...] = jnp.zeros_like(o_ref)


def _ffn_up(x, w_gate, w_up, tile_expert, tile_src, n_valid, tm, tf):
    s, k = x.shape
    f = w_gate.shape[2]
    n_tiles = s // tm
    grid_spec = pltpu.PrefetchScalarGridSpec(
        num_scalar_prefetch=3,
        grid=(f // tf, n_tiles),
        in_specs=[pl.BlockSpec((tm, k), lambda j, t, te, src, nv: (src[t], 0)),
                  pl.BlockSpec((None, k, tf), lambda j, t, te, src, nv: (te[t], 0, j)),
                  pl.BlockSpec((None, k, tf), lambda j, t, te, src, nv: (te[t], 0, j))],
        out_specs=pl.BlockSpec((tm, tf), lambda j, t, te, src, nv: (t, j)),
        scratch_shapes=[pltpu.VMEM((k, tf), BF16), pltpu.VMEM((k, tf), BF16)],
    )
    return pl.pallas_call(
        _ffn_up_kernel, grid_spec=grid_spec,
        out_shape=jax.ShapeDtypeStruct((s, f), BF16),
        compiler_params=_params("arbitrary", "arbitrary"),
        name="ffn_up",
    )(tile_expert, tile_src, n_valid, x, w_gate, w_up)


def _ffn_down_kernel(te_ref, src_ref, nv_ref, a_ref, wd_ref, o_ref, wd_bf):
    t = pl.program_id(1)

    @pl.when(_weights_changed(te_ref, t))
    def _():
        wd_bf[...] = wd_ref[...].astype(BF16)

    @pl.when(t < nv_ref[0])
    def _():
        o_ref[...] = jnp.dot(a_ref[...], wd_bf[...], preferred_element_type=F32)

    @pl.when(t >= nv_ref[0])
    def _():
        o_ref[...] = jnp.zeros_like(o_ref)


def _ffn_down(a, w_down, tile_expert, tile_src, n_valid, tm, tn):
    s, f = a.shape
    d = w_down.shape[2]
    n_tiles = s // tm
    grid_spec = pltpu.PrefetchScalarGridSpec(
        num_scalar_prefetch=3,
        grid=(d // tn, n_tiles),
        in_specs=[pl.BlockSpec((tm, f), lambda j, t, te, src, nv: (src[t], 0)),
                  pl.BlockSpec((None, f, tn), lambda j, t, te, src, nv: (te[t], 0, j))],
        out_specs=pl.BlockSpec((tm, tn), lambda j, t, te, src, nv: (t, j)),
        scratch_shapes=[pltpu.VMEM((f, tn), BF16)],
    )
    return pl.pallas_call(
        _ffn_down_kernel, grid_spec=grid_spec,
        out_shape=jax.ShapeDtypeStruct((s, d), F32),
        compiler_params=_params("arbitrary", "arbitrary"),
        name="ffn_down",
    )(tile_expert, tile_src, n_valid, a, w_down)


def _swiglu_grouped(x, w_gate, w_up, w_down, tile_expert, tile_src, n_valid, tm):
    tf = _tile(w_gate.shape[2], 512, LANES)
    tn = _tile(w_down.shape[2], 512, LANES)
    act = _ffn_up(x, w_gate, w_up, tile_expert, tile_src, n_valid, tm, tf)
    return _ffn_down(act, w_down, tile_expert, tile_src, n_valid, tm, tn)


def _row_copy(src_hbm, row, buf, slot, r, sem):
    return pltpu.make_async_copy(src_hbm.at[pl.ds(row, 1), :], buf.at[slot, pl.ds(r, 1), :], sem.at[slot])


def _dispatch_kernel(tok_ref, h_hbm, g_ref, o_ref, buf, sem, *, tm):
    t = pl.program_id(0)
    n_t = pl.num_programs(0)

    def issue(tile, slot):
        def body(r, c):
            _row_copy(h_hbm, tok_ref[tile * tm + r], buf, slot, r, sem).start()
            return c
        lax.fori_loop(0, tm, body, 0)

    @pl.when(t == 0)
    def _():
        issue(0, 0)

    @pl.when(t + 1 < n_t)
    def _():
        issue(t + 1, (t + 1) % 2)

    slot = t % 2

    def wait_row(r, c):
        _row_copy(h_hbm, 0, buf, slot, r, sem).wait()
        return c
    lax.fori_loop(0, tm, wait_row, 0)
    o_ref[...] = _rms(buf[slot], g_ref[...]).astype(o_ref.dtype)


def _dispatch(h, g, slot_token, tm):
    s = slot_token.shape[0]
    d = h.shape[1]
    grid_spec = pltpu.PrefetchScalarGridSpec(
        num_scalar_prefetch=1,
        grid=(s // tm,),
        in_specs=[pl.BlockSpec(memory_space=pl.ANY), pl.BlockSpec((1, d), lambda t, tok: (0, 0))],
        out_specs=pl.BlockSpec((tm, d), lambda t, tok: (t, 0)),
        scratch_shapes=[pltpu.VMEM((2, tm, d), F32), pltpu.SemaphoreType.DMA((2,))],
    )
    return pl.pallas_call(
        functools.partial(_dispatch_kernel, tm=tm), grid_spec=grid_spec,
        out_shape=jax.ShapeDtypeStruct((s, d), BF16),
        compiler_params=_params("arbitrary"),
        name="moe_dispatch",
    )(slot_token, h, g.reshape(1, d))


def _combine_kernel(slot_ref, y_hbm, route_ref, h_ref, gpost_ref, o_ref, buf, sem, *, tm):
    t = pl.program_id(0)
    n_t = pl.num_programs(0)

    def issue(tile, slot):
        def body(r, c):
            for k in range(TOP_K):
                _row_copy(y_hbm, slot_ref[k, tile * tm + r], buf, slot, k * tm + r, sem).start()
            return c
        lax.fori_loop(0, tm, body, 0)

    @pl.when(t == 0)
    def _():
        issue(0, 0)

    @pl.when(t + 1 < n_t)
    def _():
        issue(t + 1, (t + 1) % 2)

    slot = t % 2

    def wait_row(r, c):
        _row_copy(y_hbm, 0, buf, slot, r, sem).wait()
        return c
    lax.fori_loop(0, TOP_K * tm, wait_row, 0)
    route = route_ref[...]
    mix = route[:, 0:1] * buf[slot, 0:tm, :]
    for k in range(1, TOP_K):
        mix = mix + route[:, k:k + 1] * buf[slot, k * tm:(k + 1) * tm, :]
    o_ref[...] = h_ref[...] + _rms(mix, gpost_ref[...])


def _combine(y_sorted, token_slots, route, h, g_post, tm):
    m, d = h.shape
    grid_spec = pltpu.PrefetchScalarGridSpec(
        num_scalar_prefetch=1,
        grid=(m // tm,),
        in_specs=[pl.BlockSpec(memory_space=pl.ANY),
                  pl.BlockSpec((tm, LANES), lambda t, sl: (t, 0)),
                  pl.BlockSpec((tm, d), lambda t, sl: (t, 0)),
                  pl.BlockSpec((1, d), lambda t, sl: (0, 0))],
        out_specs=pl.BlockSpec((tm, d), lambda t, sl: (t, 0)),
        scratch_shapes=[pltpu.VMEM((2, TOP_K * tm, d), F32), pltpu.SemaphoreType.DMA((2,))],
    )
    return pl.pallas_call(
        functools.partial(_combine_kernel, tm=tm), grid_spec=grid_spec,
        out_shape=jax.ShapeDtypeStruct((m, d), F32),
        compiler_params=_params("arbitrary"),
        name="moe_combine",
    )(token_slots, y_sorted, route, h, g_post.reshape(1, d))


def _routing_tables(route, n_experts, tm):
    m = route.shape[0]
    eid = jnp.concatenate([route[:, 2 + k].astype(I32) for k in range(TOP_K)])
    onehot = (eid[:, None] == jnp.arange(n_experts, dtype=I32)[None, :]).astype(I32)
    rank = jnp.sum((jnp.cumsum(onehot, axis=0) - onehot) * onehot, axis=1)
    counts = jnp.sum(onehot, axis=0)
    tiles_e = (counts + tm - 1) // tm
    tile_end = jnp.cumsum(tiles_e)
    start = (tile_end - tiles_e) * tm
    slot = start[eid] + rank
    n_tiles = (TOP_K * m + n_experts * (tm - 1)) // tm
    token = jnp.tile(jnp.arange(m, dtype=I32), TOP_K)
    slot_token = jnp.zeros((n_tiles * tm,), I32).at[slot].set(token)
    n_valid = tile_end[-1].astype(I32)
    tile_id = jnp.arange(n_tiles, dtype=I32)
    tile_src = jnp.minimum(tile_id, n_valid - 1)
    tile_expert = jnp.minimum(jnp.searchsorted(tile_end, tile_src, side="right"), n_experts - 1).astype(I32)
    return slot_token, slot.reshape(TOP_K, m).astype(I32), tile_expert, tile_src, n_valid.reshape(1)


def _pool_prompt_kernel(halo_ref, p_ref, w_ref, s_ref, o_ref, buf_ref, *, tiles_per_seq):
    i = pl.program_id(0) % tiles_per_seq
    t = p_ref.shape[0]
    gw = w_ref.shape[1]
    buf_ref[0:POOL_HALO, :] = jnp.where(i == 0, 0.0, halo_ref[...])
    buf_ref[POOL_HALO:POOL_HALO + t, :] = p_ref[...]
    pos = i * t + lax.broadcasted_iota(I32, (t, 1), 0)
    for g, w in enumerate(POOL_WINDOWS):
        c0 = g * gw
        cur = buf_ref[POOL_HALO:POOL_HALO + t, c0:c0 + gw]
        acc = cur
        for j in range(1, w):
            acc = acc + buf_ref[POOL_HALO - j:POOL_HALO - j + t, c0:c0 + gw]
        cnt = jnp.minimum(pos + 1, w).astype(F32)
        d = acc / cnt - cur
        y = jnp.dot(d.astype(BF16), w_ref[g], preferred_element_type=F32)
        o_ref[:, c0:c0 + gw] = (y * s_ref[:, c0:c0 + gw]).astype(o_ref.dtype)


def _pool_prompt(proj, n_rows, seq, w_pool, scale, t):
    c = scale.shape[0]
    halo_per_tile = t // POOL_HALO
    return pl.pallas_call(
        functools.partial(_pool_prompt_kernel, tiles_per_seq=seq // t),
        grid=(n_rows // t,),
        in_specs=[pl.BlockSpec((POOL_HALO, c), lambda i: (jnp.maximum(i * halo_per_tile - 1, 0), 0)),
                  pl.BlockSpec((t, c), lambda i: (i, 0)),
                  pl.BlockSpec(w_pool.shape, lambda i: (0, 0, 0)),
                  pl.BlockSpec((1, c), lambda i: (0, 0))],
        out_specs=pl.BlockSpec((t, c), lambda i: (i, 0)),
        out_shape=jax.ShapeDtypeStruct((n_rows, c), BF16),
        scratch_shapes=[pltpu.VMEM((POOL_HALO + t, c), F32)],
        compiler_params=_params("parallel"),
        name="pool_prompt",
    )(proj, proj, w_pool, scale.reshape(1, c))


def _pool_sample_kernel(ctx_ref, w_ref, s_ref, o_ref, *, n_ctx, pos0):
    gw = w_ref.shape[1]
    for t in range(o_ref.shape[0]):
        hi = n_ctx + t + 1
        for g, w in enumerate(POOL_WINDOWS):
            c0 = g * gw
            lo = max(hi - w, 0)
            acc = ctx_ref[lo, :, c0:c0 + gw]
            for r in range(lo + 1, hi):
                acc = acc + ctx_ref[r, :, c0:c0 + gw]
            d = acc / float(min(pos0 + t + 1, w)) - ctx_ref[hi - 1, :, c0:c0 + gw]
            y = jnp.dot(d.astype(BF16), w_ref[g], preferred_element_type=F32)
            o_ref[t, :, c0:c0 + gw] = (y * s_ref[:, c0:c0 + gw]).astype(o_ref.dtype)


def _pool_sample(ctx_t, n_ctx, pos0, w_pool, scale):
    total, n, c = ctx_t.shape
    return pl.pallas_call(
        functools.partial(_pool_sample_kernel, n_ctx=n_ctx, pos0=pos0),
        out_shape=jax.ShapeDtypeStruct((total - n_ctx, n, c), BF16),
        compiler_params=pltpu.CompilerParams(vmem_limit_bytes=VMEM_LIMIT_BYTES),
        name="pool_sample",
    )(ctx_t, w_pool, scale.reshape(1, c))


def _gmlp_kernel(u_ref, v_ref, w_ref, b_ref, o_ref):
    t = u_ref.shape[0]
    n_heads = w_ref.shape[0]
    ck = w_ref.shape[1]
    row = lax.broadcasted_iota(I32, (ck, ck), 0)
    col = lax.broadcasted_iota(I32, (ck, ck), 1)
    for h in range(n_heads):
        c0 = h * HEAD_DIM
        wm = jnp.where(col <= row, w_ref[h], 0.0).astype(BF16)
        bias = b_ref[:, h:h + 1]
        for k in range(t // ck):
            r0 = k * ck
            v = v_ref[r0:r0 + ck, c0:c0 + HEAD_DIM].astype(BF16)
            mixed = jnp.dot(wm, v, preferred_element_type=F32) + bias
            o_ref[r0:r0 + ck, c0:c0 + HEAD_DIM] = (u_ref[r0:r0 + ck, c0:c0 + HEAD_DIM] * mixed).astype(o_ref.dtype)


def _gmlp(proj, n_prompt_rows, w_pair, b_pair, t):
    m = proj.shape[0]
    c = proj.shape[1] // 3
    n_heads = w_pair.shape[1]
    first_sample_tile = n_prompt_rows // t
    which = lambda i: jnp.where(i >= first_sample_tile, 1, 0)
    return pl.pallas_call(
        _gmlp_kernel,
        grid=(m // t,),
        in_specs=[pl.BlockSpec((t, c), lambda i: (i, 1)),
                  pl.BlockSpec((t, c), lambda i: (i, 2)),
                  pl.BlockSpec((None, n_heads, GMLP_CHUNK, GMLP_CHUNK), lambda i: (which(i), 0, 0, 0)),
                  pl.BlockSpec((None, GMLP_CHUNK, n_heads), lambda i: (which(i), 0, 0))],
        out_specs=pl.BlockSpec((t, c), lambda i: (i, 0)),
        out_shape=jax.ShapeDtypeStruct((m, c), BF16),
        compiler_params=_params("parallel"),
        name="gmlp",
    )(proj, proj, w_pair, b_pair)


def _conv_prompt_kernel(xc_ref, bg_ref, cg_ref, xch_ref, cgh_ref, w_ref, o_ref, tail_ref, buf_ref, *, tiles_per_seq):
    i = pl.program_id(0) % tiles_per_seq
    t = xc_ref.shape[0]
    z = cg_ref[...] * xc_ref[...]
    buf_ref[0:CONV_HALO, :] = jnp.where(i == 0, 0.0, cgh_ref[...] * xch_ref[...])
    buf_ref[CONV_HALO:CONV_HALO + t, :] = z
    y = w_ref[0:1, :] * buf_ref[CONV_HALO - 2:CONV_HALO - 2 + t, :]
    for j in range(1, CONV_WIDTH):
        y = y + w_ref[j:j + 1, :] * buf_ref[CONV_HALO - 2 + j:CONV_HALO - 2 + j + t, :]
    o_ref[...] = (bg_ref[...] * y).astype(o_ref.dtype)
    tail_ref[...] = z[t - CONV_HALO:, :]


def _conv_prompt(proj, n_rows, seq, conv_w, t):
    c = conv_w.shape[1]
    halo_per_tile = t // CONV_HALO
    halo_row = lambda i: jnp.maximum(i * halo_per_tile - 1, 0)
    return pl.pallas_call(
        functools.partial(_conv_prompt_kernel, tiles_per_seq=seq // t),
        grid=(n_rows // t,),
        in_specs=[pl.BlockSpec((t, c), lambda i: (i, 0)),
                  pl.BlockSpec((t, c), lambda i: (i, 1)),
                  pl.BlockSpec((t, c), lambda i: (i, 2)),
                  pl.BlockSpec((CONV_HALO, c), lambda i: (halo_row(i), 0)),
                  pl.BlockSpec((CONV_HALO, c), lambda i: (halo_row(i), 2)),
                  pl.BlockSpec((CONV_WIDTH, c), lambda i: (0, 0))],
        out_specs=[pl.BlockSpec((t, c), lambda i: (i, 0)),
                   pl.BlockSpec((CONV_HALO, c), lambda i: (i, 0))],
        out_shape=[jax.ShapeDtypeStruct((n_rows, c), BF16),
                   jax.ShapeDtypeStruct((n_rows // t * CONV_HALO, c), F32)],
        scratch_shapes=[pltpu.VMEM((CONV_HALO + t, c), F32)],
        compiler_params=_params("parallel"),
        name="conv_prompt",
    )(proj, proj, proj, proj, proj, conv_w)


def _conv_sample_kernel(ctx_ref, xc_ref, bg_ref, cg_ref, w_ref, o_ref, z_ref):
    n_ctx = ctx_ref.shape[0]
    length = xc_ref.shape[0]
    rows = [ctx_ref[r] for r in range(n_ctx)]
    for t in range(length):
        z = cg_ref[t] * xc_ref[t]
        z_ref[t] = z
        rows.append(z)
    for t in range(length):
        y = w_ref[0:1, :] * rows[t]
        for j in range(1, CONV_WIDTH):
            y = y + w_ref[j:j + 1, :] * rows[t + j]
        o_ref[t] = (bg_ref[t] * y).astype(o_ref.dtype)


def _conv_sample(ctx_t, xc_t, bg_t, cg_t, conv_w):
    length, n, c = xc_t.shape
    return pl.pallas_call(
        _conv_sample_kernel,
        out_shape=[jax.ShapeDtypeStruct((length, n, c), BF16), jax.ShapeDtypeStruct((length, n, c), F32)],
        compiler_params=pltpu.CompilerParams(vmem_limit_bytes=VMEM_LIMIT_BYTES),
        name="conv_sample",
    )(ctx_t, xc_t, bg_t, cg_t, conv_w)


def _log_sigmoid(x):
    return jnp.minimum(x, 0.0) - jnp.log1p(jnp.exp(-jnp.abs(x)))


def _logf_kernel(f_ref, b_ref, lf_ref, c_ref, carry_ref, *, blocks_per_seq):
    i = pl.program_id(0)
    blk = f_ref.shape[0]
    lf = _log_sigmoid(f_ref[...] + b_ref[...])
    lf_ref[...] = lf

    @pl.when(i % blocks_per_seq == 0)
    def _():
        carry_ref[...] = jnp.zeros_like(carry_ref)

    row = lax.broadcasted_iota(I32, (blk, blk), 0)
    col = lax.broadcasted_iota(I32, (blk, blk), 1)
    tri = jnp.where(col <= row, 1.0, 0.0).astype(F32)
    c = jnp.dot(tri, lf, precision=HIGHEST, preferred_element_type=F32) + carry_ref[0:1, :]
    c_ref[...] = c
    carry_ref[...] = jnp.broadcast_to(c[blk - 1:blk, :], carry_ref.shape)


def _logf_cumsum(f_logit, b_forget, seq, blk):
    m = f_logit.shape[0]
    return pl.pallas_call(
        functools.partial(_logf_kernel, blocks_per_seq=seq // blk),
        grid=(m // blk,),
        in_specs=[pl.BlockSpec((blk, LANES), lambda i: (i, 0)), pl.BlockSpec((1, LANES), lambda i: (0, 0))],
        out_specs=[pl.BlockSpec((blk, LANES), lambda i: (i, 0)), pl.BlockSpec((blk, LANES), lambda i: (i, 0))],
        out_shape=[jax.ShapeDtypeStruct((m, LANES), F32), jax.ShapeDtypeStruct((m, LANES), F32)],
        scratch_shapes=[pltpu.VMEM((8, LANES), F32)],
        compiler_params=_params("arbitrary"),
        name="logf_cumsum",
    )(f_logit, b_forget)


def _flash_kernel(q_ref, k_ref, v_ref, cc_ref, cr_ref, o_ref, *, tq):
    h = pl.program_id(1)
    qb = pl.program_id(2)
    q = (q_ref[...] * (1.0 / math.sqrt(HEAD_DIM))).astype(BF16)
    lane = lax.broadcasted_iota(I32, (tq, LANES), 1)
    cq = jnp.sum(jnp.where(lane == h, cc_ref[...], 0.0), axis=-1, keepdims=True)

    def scores(j):
        start = pl.multiple_of(j * tq, tq)
        k = k_ref[pl.ds(start, tq), :].astype(BF16)
        v = v_ref[pl.ds(start, tq), :].astype(BF16)
        s = lax.dot_general(q, k, (((1,), (1,)), ((), ())), preferred_element_type=F32)
        ck = cr_ref[:, pl.ds(start, tq)]
        return s + (cq - ck), v

    def update(carry, s, v):
        m, l, acc = carry
        m_new = jnp.maximum(m, jnp.max(s, axis=-1, keepdims=True))
        alpha = jnp.exp(m - m_new)
        p = jnp.exp(s - m_new)
        l = alpha * l + jnp.sum(p, axis=-1, keepdims=True)
        acc = alpha * acc + jnp.dot(p.astype(BF16), v, preferred_element_type=F32)
        return m_new, l, acc

    def body(j, carry):
        s, v = scores(j)
        return update(carry, s, v)

    init = (jnp.full((tq, 1), MASKED, F32), jnp.zeros((tq, 1), F32), jnp.zeros((tq, HEAD_DIM), F32))
    carry = lax.fori_loop(0, qb, body, init)
    s, v = scores(qb)
    row = lax.broadcasted_iota(I32, (tq, tq), 0)
    col = lax.broadcasted_iota(I32, (tq, tq), 1)
    s = jnp.where(col <= row, s, MASKED)
    _, l, acc = update(carry, s, v)
    o_ref[...] = (acc / l).astype(o_ref.dtype)


def _flash_prompt(proj, c_col, c_row, n_seq, seq, n_heads, tq):
    q_blk, k_blk, v_blk = 3 * n_heads, 4 * n_heads, 5 * n_heads
    nqb = seq // tq
    return pl.pallas_call(
        functools.partial(_flash_kernel, tq=tq),
        grid=(n_seq, n_heads, nqb),
        in_specs=[pl.BlockSpec((tq, HEAD_DIM), lambda n, h, i: (n * nqb + i, q_blk + h)),
                  pl.BlockSpec((seq, HEAD_DIM), lambda n, h, i: (n, k_blk + h)),
                  pl.BlockSpec((seq, HEAD_DIM), lambda n, h, i: (n, v_blk + h)),
                  pl.BlockSpec((tq, LANES), lambda n, h, i: (n * nqb + i, 0)),
                  pl.BlockSpec((None, 1, seq), lambda n, h, i: (h, 0, n))],
        out_specs=pl.BlockSpec((tq, HEAD_DIM), lambda n, h, i: (n * nqb + i, h)),
        out_shape=jax.ShapeDtypeStruct((n_seq * seq, n_heads * HEAD_DIM), BF16),
        compiler_params=_params("parallel", "parallel", "arbitrary"),
        name="fox_prompt",
    )(proj, proj, proj, c_col, c_row)


def _decay_past_kernel(pt_ref, *refs, n_pages, n_heads):
    lf_refs, o_ref = refs[:n_pages], refs[n_pages]
    width = o_ref.shape[2]
    lf = jnp.concatenate([r[...] for r in lf_refs], axis=0)
    lane = lax.broadcasted_iota(I32, lf.shape, 1)
    incl, total = lf, lf
    shift = n_heads
    while shift < width:
        incl = incl + jnp.where(lane + shift < width, pltpu.roll(incl, width - shift, 1), 0.0)
        total = total + pltpu.roll(total, shift, 1)
        shift *= 2
    prow = lax.broadcasted_iota(I32, (n_pages, n_pages), 0)
    pcol = lax.broadcasted_iota(I32, (n_pages, n_pages), 1)
    later = jnp.where(pcol > prow, 1.0, 0.0).astype(F32)
    carry = jnp.dot(later, total, precision=HIGHEST, preferred_element_type=F32)
    o_ref[0] = (incl - lf) + carry


def _decay_past(logf_pages, page_table, n_heads):
    n, n_pages = page_table.shape
    width = logf_pages.shape[2]
    grid_spec = pltpu.PrefetchScalarGridSpec(
        num_scalar_prefetch=1,
        grid=(n,),
        in_specs=[pl.BlockSpec((None, 1, width), functools.partial(lambda i, pt, p: (pt[i * n_pages + p], 0, 0), p=p))
                  for p in range(n_pages)],
        out_specs=pl.BlockSpec((1, n_pages, width), lambda i, pt: (i, 0, 0)),
    )
    return pl.pallas_call(
        functools.partial(_decay_past_kernel, n_pages=n_pages, n_heads=n_heads),
        grid_spec=grid_spec,
        out_shape=jax.ShapeDtypeStruct((n, n_pages, width), F32),
        compiler_params=_params("parallel"),
        name="fox_decay_past",
    )(page_table.reshape(-1), *([logf_pages] * n_pages))


def _decode_kernel(pt_ref, q_ref, k_ref, v_ref, dp_ref, kn_ref, vn_ref, lfn_ref, o_ref, m_ref, l_ref, acc_ref,
                   *, n_heads, n_q):
    j = pl.program_id(1)
    rows = n_heads * n_q
    page_keys = k_ref.shape[0]
    width = page_keys * n_heads

    @pl.when(j == 0)
    def _():
        m_ref[...] = jnp.full_like(m_ref, MASKED)
        l_ref[...] = jnp.zeros_like(l_ref)
        acc_ref[...] = jnp.zeros_like(acc_ref)

    q = (q_ref[0] * (1.0 / math.sqrt(HEAD_DIM))).astype(BF16)
    lfn = lfn_ref[0]
    r_i = lax.broadcasted_iota(I32, (rows, LANES), 0)
    c_i = lax.broadcasted_iota(I32, (rows, LANES), 1)
    sel = (c_i % n_heads == r_i // n_q) & (c_i // n_heads <= r_i % n_q) & (c_i < n_q * n_heads)
    cq = jnp.sum(jnp.where(sel, jnp.broadcast_to(lfn, (rows, LANES)), 0.0), axis=-1, keepdims=True)

    def update(s, valid, v):
        s = jnp.where(valid, s, MASKED)
        m_old = m_ref[:, 0:1]
        m_new = jnp.maximum(m_old, jnp.max(s, axis=-1, keepdims=True))
        alpha = jnp.exp(m_old - m_new)
        p = jnp.where(valid, jnp.exp(s - m_new), 0.0)
        l_ref[...] = jnp.broadcast_to(alpha * l_ref[:, 0:1] + jnp.sum(p, axis=-1, keepdims=True), l_ref.shape)
        acc_ref[...] = alpha * acc_ref[...] + jnp.dot(p.astype(BF16), v, preferred_element_type=F32)
        m_ref[...] = jnp.broadcast_to(m_new, m_ref.shape)

    k = k_ref[...].reshape(width, HEAD_DIM).astype(BF16)
    v = v_ref[...].reshape(width, HEAD_DIM).astype(BF16)
    s = lax.dot_general(q, k, (((1,), (1,)), ((), ())), preferred_element_type=F32)
    s = s + dp_ref[0] + cq
    r_w = lax.broadcasted_iota(I32, (rows, width), 0)
    c_w = lax.broadcasted_iota(I32, (rows, width), 1)
    update(s, c_w % n_heads == r_w // n_q, v)

    @pl.when(j == pl.num_programs(1) - 1)
    def _():
        kn = kn_ref[0].astype(BF16)
        vn = vn_ref[0].astype(BF16)
        sn = lax.dot_general(q, kn, (((1,), (1,)), ((), ())), preferred_element_type=F32)
        lf8 = jnp.broadcast_to(lfn, (8, LANES))
        lane8 = lax.broadcasted_iota(I32, (8, LANES), 1)
        csum = lf8
        shift = n_heads
        while shift < n_q * n_heads:
            csum = csum + jnp.where(lane8 >= shift, pltpu.roll(csum, shift, 1), 0.0)
            shift *= 2
        sn = sn + (cq - csum[0:1, :])
        update(sn, sel, vn)
        o_ref[0] = acc_ref[...] / l_ref[:, 0:1]


def _decode_attention(q_hq, cache_k, cache_v, layer, page_table, decay_past, k_new, v_new, lf_new, n_heads, n_q):
    n, n_pages = page_table.shape
    page = cache_k.shape[2]
    rows = n_heads * n_q
    width = page * n_heads
    page_spec = pl.BlockSpec((None, None, page, n_heads, HEAD_DIM),
                             lambda i, j, pt: (layer, pt[i * n_pages + j], 0, 0, 0))
    per_seq = lambda shape: pl.BlockSpec(shape, lambda i, j, pt: (i, 0, 0))
    grid_spec = pltpu.PrefetchScalarGridSpec(
        num_scalar_prefetch=1,
        grid=(n, n_pages),
        in_specs=[per_seq((1, rows, HEAD_DIM)), page_spec, page_spec,
                  pl.BlockSpec((1, 1, width), lambda i, j, pt: (i * n_pages + j, 0, 0)),
                  per_seq((1, LANES, HEAD_DIM)), per_seq((1, LANES, HEAD_DIM)), per_seq((1, 1, LANES))],
        out_specs=per_seq((1, rows, HEAD_DIM)),
        scratch_shapes=[pltpu.VMEM((rows, LANES), F32), pltpu.VMEM((rows, LANES), F32),
                        pltpu.VMEM((rows, HEAD_DIM), F32)],
    )
    return pl.pallas_call(
        functools.partial(_decode_kernel, n_heads=n_heads, n_q=n_q),
        grid_spec=grid_spec,
        out_shape=jax.ShapeDtypeStruct((n, rows, HEAD_DIM), F32),
        compiler_params=_params("parallel", "arbitrary"),
        name="fox_decode",
    )(page_table.reshape(-1), q_hq, cache_k, cache_v, decay_past, k_new, v_new, lf_new)


def kernel(x_prompt, x_sample, state_pool, state_conv, cache_k, cache_v, cache_logf, page_table, e_norm_mix_pre, e_norm_mix_post, e_norm_ffn_pre, e_norm_ffn_post, e_w_in, e_w_pool, e_pool_scale, e_w_spatial, e_b_spatial, e_w_out, e_ffn_gate, e_ffn_up, e_ffn_down, o_norm_mix_pre, o_norm_mix_post, o_norm_ffn_pre, o_norm_ffn_post, o_w_in, o_conv_w, o_b_forget, o_w_out, o_w_router, o_exp_gate, o_exp_up, o_exp_down):
    n_p, seq, d = x_prompt.shape
    n_s, dec, _ = x_sample.shape
    mp, ms = n_p * seq, n_s * dec
    m = mp + ms
    c = d // 2
    n_heads = c // HEAD_DIM
    n_pages = page_table.shape[1]
    page = cache_k.shape[2]
    past_len = n_pages * page
    n_layers = e_w_in.shape[0] + o_w_in.shape[0]
    n_experts = o_w_router.shape[2]
    assert ms % GMLP_CHUNK == 0 and seq % GMLP_CHUNK == 0 and past_len % GMLP_CHUNK == 0
    assert GMLP_CHUNK % min(dec, GMLP_CHUNK) == 0 and dec % min(dec, GMLP_CHUNK) == 0
    assert dec * n_heads <= LANES and dec >= CONV_WIDTH - 1

    tm = _tile(math.gcd(mp, ms), 512, LANES)
    t_seq = _tile(math.gcd(seq, tm), 512, LANES)
    tq = _tile(seq, 256, LANES)
    tm_moe = 256
    dense_tiles = m // tm
    dense_te = jnp.zeros((dense_tiles,), I32)
    dense_src = jnp.arange(dense_tiles, dtype=I32)
    dense_nv = jnp.full((1,), dense_tiles, I32)

    h = jnp.concatenate([x_prompt.reshape(mp, d), x_sample.reshape(ms, d)], axis=0)
    xn = _norm_cast(h, e_norm_mix_pre[0], tm)

    pool_p, pool_s, gv_s, conv_p, conv_s = [], [], [], [], []
    kp_l, vp_l, lp_l, ks_l, vs_l, ls_l = [], [], [], [], [], []

    def time_major(x2d):
        return jnp.transpose(x2d.reshape(n_s, dec, -1), (1, 0, 2))

    def row_major(x3d):
        return jnp.transpose(x3d, (1, 0, 2)).reshape(ms, -1)

    for layer in range(n_layers):
        i = layer // 2
        if layer % 2 == 0:
            g_next = e_norm_ffn_pre[i]
            proj = _mm(xn, e_w_in[i].astype(BF16), tm, _tile(3 * c, 1024, LANES))
            p_s = proj[mp:, :c].reshape(n_s, dec, c)
            ctx = jnp.concatenate([state_pool[i].astype(F32), p_s], axis=1)
            n_ctx = state_pool.shape[2]
            w_pool = e_w_pool[i].astype(BF16)
            a_p = _pool_prompt(proj, mp, seq, w_pool, e_pool_scale[i], t_seq)
            a_s = row_major(_pool_sample(jnp.transpose(ctx, (1, 0, 2)), n_ctx, past_len, w_pool, e_pool_scale[i]))
            a = jnp.concatenate([a_p, a_s], axis=0)
            cl = min(dec, GMLP_CHUNK)
            reps = GMLP_CHUNK // cl
            w_samp = jnp.einsum("ab,hts->hatbs", jnp.eye(reps, dtype=F32),
                                e_w_spatial[i][:, :cl, :cl]).reshape(n_heads, GMLP_CHUNK, GMLP_CHUNK)
            b_samp = jnp.tile(e_b_spatial[i][:, :cl], (1, reps))
            w_pair = jnp.stack([e_w_spatial[i], w_samp])
            b_pair = jnp.stack([e_b_spatial[i].T, b_samp.T])
            b = _gmlp(proj, mp, w_pair, b_pair, tm)
            h, xn = _proj_out(a, b, e_w_out[i].astype(BF16), h, e_norm_mix_post[i], g_next, tm)
            g_after = o_norm_mix_pre[i] if layer + 1 < n_layers else jnp.ones((d,), F32)
            mix = _swiglu_grouped(xn, e_ffn_gate[i][None], e_ffn_up[i][None], e_ffn_down[i][None],
                                  dense_te, dense_src, dense_nv, tm)
            h, xn = _norm_residual(mix, h, e_norm_ffn_post[i], g_after, tm)
            pool_p.append(proj[:mp, :c].reshape(n_p, seq, c)[:, seq - n_ctx:])
            pool_s.append(ctx[:, ctx.shape[1] - n_ctx:])
            gv_s.append(proj[mp:, 2 * c:].reshape(n_s, dec, c))
        else:
            w_in = o_w_in[i]
            proj = _mm(xn, w_in[:, :6 * c].astype(BF16), tm, _tile(6 * c, 1024, LANES))
            w_f = jnp.zeros((d, LANES), BF16).at[:, :n_heads].set(w_in[:, 6 * c:].astype(BF16))
            f_logit = _mm(xn, w_f, tm, LANES)
            b_f = jnp.zeros((1, LANES), F32).at[0, :n_heads].set(o_b_forget[i])
            logf, csum = _logf_cumsum(f_logit, b_f, seq, LANES)
            c_p, z_tail = _conv_prompt(proj, mp, seq, o_conv_w[i], t_seq)
            z_tail = z_tail.reshape(n_p, seq // t_seq, CONV_HALO, c)[:, -1, CONV_HALO - (CONV_WIDTH - 1):]
            xs = proj[mp:]
            conv_ctx = jnp.transpose(state_conv[i].astype(F32), (1, 0, 2))
            c_s, z_s = _conv_sample(conv_ctx, time_major(xs[:, :c]), time_major(xs[:, c:2 * c]),
                                    time_major(xs[:, 2 * c:3 * c]), o_conv_w[i])
            z_all = jnp.concatenate([conv_ctx, z_s], axis=0)
            c_row = csum[:mp, :n_heads].T.reshape(n_heads, 1, mp)
            att_p = _flash_prompt(proj, csum, c_row, n_p, seq, n_heads, tq)
            k_s = xs[:, 4 * c:5 * c]
            v_s = xs[:, 5 * c:6 * c]
            q_hq = jnp.transpose(xs[:, 3 * c:4 * c].reshape(n_s, dec, n_heads, HEAD_DIM), (0, 2, 1, 3))
            q_hq = q_hq.reshape(n_s, n_heads * dec, HEAD_DIM)
            pad_rows = LANES - dec * n_heads
            k_new = jnp.pad(k_s.reshape(n_s, dec * n_heads, HEAD_DIM), ((0, 0), (0, pad_rows), (0, 0)))
            v_new = jnp.pad(v_s.reshape(n_s, dec * n_heads, HEAD_DIM), ((0, 0), (0, pad_rows), (0, 0)))
            lf_new = jnp.pad(logf[mp:, :n_heads].reshape(n_s, 1, dec * n_heads), ((0, 0), (0, 0), (0, pad_rows)))
            logf_pages = cache_logf[i].astype(F32).reshape(cache_logf.shape[1], 1, page * n_heads)
            decay = _decay_past(logf_pages, page_table, n_heads).reshape(n_s * n_pages, 1, page * n_heads)
            att_s = _decode_attention(q_hq, cache_k, cache_v, i, page_table, decay, k_new, v_new, lf_new,
                                      n_heads, dec)
            att_s = jnp.transpose(att_s.reshape(n_s, n_heads, dec, HEAD_DIM), (0, 2, 1, 3)).reshape(ms, c)
            cd = jnp.concatenate([c_p, row_major(c_s)], axis=0)
            att = jnp.concatenate([att_p, att_s.astype(BF16)], axis=0)
            h, _, route = _proj_out(cd, att, o_w_out[i].astype(BF16), h, o_norm_mix_post[i], o_norm_ffn_pre[i],
                                    tm, w_router=o_w_router[i])
            slot_token, token_slots, tile_expert, tile_src, n_valid = _routing_tables(route, n_experts, tm_moe)
            x_sorted = _dispatch(h, o_norm_ffn_pre[i], slot_token, tm_moe)
            y_sorted = _swiglu_grouped(x_sorted, o_exp_gate[i], o_exp_up[i], o_exp_down[i],
                                       tile_expert, tile_src, n_valid, tm_moe)
            h = _combine(y_sorted, token_slots, route, h, o_norm_ffn_post[i], _tile(m, tm_moe))
            if layer + 1 < n_layers:
                xn = _norm_cast(h, e_norm_mix_pre[i + 1], tm)
            conv_p.append(z_tail)
            conv_s.append(jnp.transpose(z_all[z_all.shape[0] - (CONV_WIDTH - 1):], (1, 0, 2)))
            kp_l.append(proj[:mp, 4 * c:5 * c].reshape(n_p, seq, n_heads, HEAD_DIM))
            vp_l.append(proj[:mp, 5 * c:6 * c].reshape(n_p, seq, n_heads, HEAD_DIM))
            lp_l.append(logf[:mp, :n_heads].reshape(n_p, seq, n_heads))
            ks_l.append(k_s.reshape(n_s, dec, n_heads, HEAD_DIM))
            vs_l.append(v_s.reshape(n_s, dec, n_heads, HEAD_DIM))
            ls_l.append(logf[mp:, :n_heads].reshape(n_s, dec, n_heads))

    return (h[:mp].reshape(n_p, seq, d), h[mp:].reshape(n_s, dec, d),
            jnp.stack(pool_p), jnp.stack(pool_s), jnp.stack(gv_s), jnp.stack(conv_p), jnp.stack(conv_s),
            jnp.stack(kp_l), jnp.stack(vp_l), jnp.stack(lp_l), jnp.stack(ks_l), jnp.stack(vs_l), jnp.stack(ls_l))
```

```python
import functools
import math

import jax
import jax.numpy as jnp
from jax import lax
from jax.experimental import pallas as pl
from jax.experimental.pallas import tpu as pltpu

F32, BF16, I32 = jnp.float32, jnp.bfloat16, jnp.int32
RMS_EPS = 1e-6
POOL_WINDOWS = (2, 4, 8, 16)
POOL_HALO = 16
CONV_WIDTH = 3
CONV_HALO = 8
HEAD_DIM = 128
GMLP_CHUNK = 128
TOP_K = 2
LANES = 128
MASKED = -1e30
VMEM_LIMIT_BYTES = 56 * 1024 * 1024
HIGHEST = lax.Precision.HIGHEST


def _params(*semantics):
    return pltpu.CompilerParams(dimension_semantics=semantics, vmem_limit_bytes=VMEM_LIMIT_BYTES)


def _tile(n, pref, mult=8):
    t = min(n, pref)
    t -= t % mult
    while t > mult and n % t:
        t -= mult
    assert t > 0 and n % t == 0, (n, pref, mult)
    return t


def _rms(x, g):
    return x * lax.rsqrt(jnp.mean(x * x, axis=-1, keepdims=True) + RMS_EPS) * g


def _norm_cast_kernel(x_ref, g_ref, o_ref):
    o_ref[...] = _rms(x_ref[...], g_ref[...]).astype(o_ref.dtype)


def _norm_cast(x, g, tm):
    m, d = x.shape
    return pl.pallas_call(
        _norm_cast_kernel,
        grid=(m // tm,),
        in_specs=[pl.BlockSpec((tm, d), lambda i: (i, 0)), pl.BlockSpec((1, d), lambda i: (0, 0))],
        out_specs=pl.BlockSpec((tm, d), lambda i: (i, 0)),
        out_shape=jax.ShapeDtypeStruct((m, d), BF16),
        compiler_params=_params("parallel"),
        name="norm_cast",
    )(x, g.reshape(1, d))


def _mm_kernel(x_ref, w_ref, o_ref):
    o_ref[...] = jnp.dot(x_ref[...], w_ref[...], preferred_element_type=F32)


def _mm(x, w, tm, tn):
    m, k = x.shape
    n = w.shape[1]
    return pl.pallas_call(
        _mm_kernel,
        grid=(n // tn, m // tm),
        in_specs=[pl.BlockSpec((tm, k), lambda j, i: (i, 0)), pl.BlockSpec((k, tn), lambda j, i: (0, j))],
        out_specs=pl.BlockSpec((tm, tn), lambda j, i: (i, j)),
        out_shape=jax.ShapeDtypeStruct((m, n), F32),
        compiler_params=_params("parallel", "parallel"),
        name="proj_in",
    )(x, w)


def _top2(logits, n_experts):
    lane = lax.broadcasted_iota(I32, logits.shape, 1)
    neg_inf = jnp.float32(-jnp.inf)
    l1 = jnp.where(lane < n_experts, logits, neg_inf)
    m1 = jnp.max(l1, axis=-1, keepdims=True)
    i1 = jnp.min(jnp.where(l1 == m1, lane, LANES), axis=-1, keepdims=True)
    l2 = jnp.where(lane == i1, neg_inf, l1)
    m2 = jnp.max(l2, axis=-1, keepdims=True)
    i2 = jnp.min(jnp.where(l2 == m2, lane, LANES), axis=-1, keepdims=True)
    e2 = jnp.exp(m2 - m1)
    g1 = 1.0 / (1.0 + e2)
    g2 = e2 / (1.0 + e2)
    return jnp.where(lane == 0, g1,
                     jnp.where(lane == 1, g2,
                               jnp.where(lane == 2, i1.astype(F32),
                                         jnp.where(lane == 3, i2.astype(F32), 0.0))))


def _proj_out_kernel(a_ref, b_ref, w_ref, h_ref, gpost_ref, gnext_ref, *rest, n_experts):
    half = a_ref.shape[1]
    m = (jnp.dot(a_ref[...], w_ref[:half, :], preferred_element_type=F32)
         + jnp.dot(b_ref[...], w_ref[half:, :], preferred_element_type=F32))
    hn = h_ref[...] + _rms(m, gpost_ref[...])
    xn = _rms(hn, gnext_ref[...])
    xn_hi = xn.astype(BF16)
    if n_experts:
        wr_hi_ref, wr_lo_ref, hn_ref, xn_ref, route_ref = rest
        xn_lo = (xn - xn_hi.astype(F32)).astype(BF16)
        logits = (jnp.dot(xn_hi, wr_hi_ref[...], preferred_element_type=F32)
                  + jnp.dot(xn_lo, wr_hi_ref[...], preferred_element_type=F32)
                  + jnp.dot(xn_hi, wr_lo_ref[...], preferred_element_type=F32))
        route_ref[...] = _top2(logits, n_experts)
    else:
        hn_ref, xn_ref = rest
    hn_ref[...] = hn
    xn_ref[...] = xn_hi


def _proj_out(a, b, w, h, g_post, g_next, tm, w_router=None):
    m, d = h.shape
    half = a.shape[1]
    n_experts = 0 if w_router is None else w_router.shape[1]
    row = lambda i: (i, 0)
    fixed = lambda i: (0, 0)
    in_specs = [pl.BlockSpec((tm, half), row), pl.BlockSpec((tm, half), row), pl.BlockSpec((2 * half, d), fixed),
                pl.BlockSpec((tm, d), row), pl.BlockSpec((1, d), fixed), pl.BlockSpec((1, d), fixed)]
    out_specs = [pl.BlockSpec((tm, d), row), pl.BlockSpec((tm, d), row)]
    out_shape = [jax.ShapeDtypeStruct((m, d), F32), jax.ShapeDtypeStruct((m, d), BF16)]
    args = [a, b, w, h, g_post.reshape(1, d), g_next.reshape(1, d)]
    if n_experts:
        wr = jnp.zeros((d, LANES), F32).at[:, :n_experts].set(w_router.astype(F32))
        wr_hi = wr.astype(BF16)
        wr_lo = (wr - wr_hi.astype(F32)).astype(BF16)
        in_specs += [pl.BlockSpec((d, LANES), fixed), pl.BlockSpec((d, LANES), fixed)]
        out_specs.append(pl.BlockSpec((tm, LANES), row))
        out_shape.append(jax.ShapeDtypeStruct((m, LANES), F32))
        args += [wr_hi, wr_lo]
    return pl.pallas_call(
        functools.partial(_proj_out_kernel, n_experts=n_experts),
        grid=(m // tm,),
        in_specs=in_specs, out_specs=out_specs, out_shape=out_shape,
        compiler_params=_params("parallel"),
        name="proj_out",
    )(*args)


def _norm_residual_kernel(m_ref, h_ref, gpost_ref, gnext_ref, hn_ref, xn_ref):
    hn = h_ref[...] + _rms(m_ref[...], gpost_ref[...])
    hn_ref[...] = hn
    xn_ref[...] = _rms(hn, gnext_ref[...]).astype(xn_ref.dtype)


def _norm_residual(mix, h, g_post, g_next, tm):
    m, d = h.shape
    row = lambda i: (i, 0)
    fixed = lambda i: (0, 0)
    return pl.pallas_call(
        _norm_residual_kernel,
        grid=(m // tm,),
        in_specs=[pl.BlockSpec((tm, d), row), pl.BlockSpec((tm, d), row),
                  pl.BlockSpec((1, d), fixed), pl.BlockSpec((1, d), fixed)],
        out_specs=[pl.BlockSpec((tm, d), row), pl.BlockSpec((tm, d), row)],
        out_shape=[jax.ShapeDtypeStruct((m, d), F32), jax.ShapeDtypeStruct((m, d), BF16)],
        compiler_params=_params("parallel"),
        name="norm_residual",
    )(mix, h, g_post.reshape(1, d), g_next.reshape(1, d))


def _weights_changed(te_ref, t):
    return jnp.logical_or(t == 0, te_ref[t] != te_ref[jnp.maximum(t - 1, 0)])


def _ffn_up_kernel(te_ref, src_ref, nv_ref, x_ref, wg_ref, wu_ref, o_ref, wg_bf, wu_bf):
    t = pl.program_id(1)

    @pl.when(_weights_changed(te_ref, t))
    def _():
        wg_bf[...] = wg_ref[...].astype(BF16)
        wu_bf[...] = wu_ref[...].astype(BF16)

    @pl.when(t < nv_ref[0])
    def _():
        x = x_ref[...]
        g = jnp.dot(x, wg_bf[...], preferred_element_type=F32)
        u = jnp.dot(x, wu_bf[...], preferred_element_type=F32)
        o_ref[...] = (g * jax.nn.sigmoid(g) * u).astype(o_ref.dtype)

    @pl.when(t >= nv_ref[0])
    def _():
        o_ref[...] = jnp.zeros_like(o_ref)


def _ffn_up(x, w_gate, w_up, tile_expert, tile_src, n_valid, tm, tf):
    s, k = x.shape
    f = w_gate.shape[2]
    n_tiles = s // tm
    grid_spec = pltpu.PrefetchScalarGridSpec(
        num_scalar_prefetch=3,
        grid=(f // tf, n_tiles),
        in_specs=[pl.BlockSpec((tm, k), lambda j, t, te, src, nv: (src[t], 0)),
                  pl.BlockSpec((None, k, tf), lambda j, t, te, src, nv: (te[t], 0, j)),
                  pl.BlockSpec((None, k, tf), lambda j, t, te, src, nv: (te[t], 0, j))],
        out_specs=pl.BlockSpec((tm, tf), lambda j, t, te, src, nv: (t, j)),
        scratch_shapes=[pltpu.VMEM((k, tf), BF16), pltpu.VMEM((k, tf), BF16)],
    )
    return pl.pallas_call(
        _ffn_up_kernel, grid_spec=grid_spec,
        out_shape=jax.ShapeDtypeStruct((s, f), BF16),
        compiler_params=_params("arbitrary", "arbitrary"),
        name="ffn_up",
    )(tile_expert, tile_src, n_valid, x, w_gate, w_up)


def _ffn_down_kernel(te_ref, src_ref, nv_ref, a_ref, wd_ref, o_ref, wd_bf):
    t = pl.program_id(1)

    @pl.when(_weights_changed(te_ref, t))
    def _():
        wd_bf[...] = wd_ref[...].astype(BF16)

    @pl.when(t < nv_ref[0])
    def _():
        o_ref[...] = jnp.dot(a_ref[...], wd_bf[...], preferred_element_type=F32)

    @pl.when(t >= nv_ref[0])
    def _():
        o_ref[...] = jnp.zeros_like(o_ref)


def _ffn_down(a, w_down, tile_expert, tile_src, n_valid, tm, tn):
    s, f = a.shape
    d = w_down.shape[2]
    n_tiles = s // tm
    grid_spec = pltpu.PrefetchScalarGridSpec(
        num_scalar_prefetch=3,
        grid=(d // tn, n_tiles),
        in_specs=[pl.BlockSpec((tm, f), lambda j, t, te, src, nv: (src[t], 0)),
                  pl.BlockSpec((None, f, tn), lambda j, t, te, src, nv: (te[t], 0, j))],
        out_specs=pl.BlockSpec((tm, tn), lambda j, t, te, src, nv: (t, j)),
        scratch_shapes=[pltpu.VMEM((f, tn), BF16)],
    )
    return pl.pallas_call(
        _ffn_down_kernel, grid_spec=grid_spec,
        out_shape=jax.ShapeDtypeStruct((s, d), F32),
        compiler_params=_params("arbitrary", "arbitrary"),
        name="ffn_down",
    )(tile_expert, tile_src, n_valid, a, w_down)


def _swiglu_grouped(x, w_gate, w_up, w_down, tile_expert, tile_src, n_valid, tm):
    tf = _tile(w_gate.shape[2], 512, LANES)
    tn = _tile(w_down.shape[2], 512, LANES)
    act = _ffn_up(x, w_gate, w_up, tile_expert, tile_src, n_valid, tm, tf)
    return _ffn_down(act, w_down, tile_expert, tile_src, n_valid, tm, tn)


GATHER_UNROLL = 8


def _row_copy(src_hbm, row, buf, slot, r, sem):
    return pltpu.make_async_copy(src_hbm.at[pl.ds(row, 1), :], buf.at[slot, pl.ds(r, 1), :], sem.at[slot])


def _wait_slot(src_hbm, buf, slot, sem):
    pltpu.make_async_copy(src_hbm.at[pl.ds(0, buf.shape[1]), :], buf.at[slot], sem.at[slot]).wait()


def _dispatch_kernel(tok_ref, h_hbm, g_ref, o_ref, buf, sem, *, tm):
    t = pl.program_id(0)
    n_t = pl.num_programs(0)

    def issue(tile, slot):
        def body(r, c):
            _row_copy(h_hbm, tok_ref[tile * tm + r], buf, slot, r, sem).start()
            return c
        lax.fori_loop(0, tm, body, 0, unroll=GATHER_UNROLL)

    @pl.when(t == 0)
    def _():
        issue(0, 0)

    @pl.when(t + 1 < n_t)
    def _():
        issue(t + 1, (t + 1) % 2)

    slot = t % 2
    _wait_slot(h_hbm, buf, slot, sem)
    o_ref[...] = _rms(buf[slot], g_ref[...]).astype(o_ref.dtype)


def _dispatch(h, g, slot_token, tm):
    s = slot_token.shape[0]
    d = h.shape[1]
    grid_spec = pltpu.PrefetchScalarGridSpec(
        num_scalar_prefetch=1,
        grid=(s // tm,),
        in_specs=[pl.BlockSpec(memory_space=pl.ANY), pl.BlockSpec((1, d), lambda t, tok: (0, 0))],
        out_specs=pl.BlockSpec((tm, d), lambda t, tok: (t, 0)),
        scratch_shapes=[pltpu.VMEM((2, tm, d), F32), pltpu.SemaphoreType.DMA((2,))],
    )
    return pl.pallas_call(
        functools.partial(_dispatch_kernel, tm=tm), grid_spec=grid_spec,
        out_shape=jax.ShapeDtypeStruct((s, d), BF16),
        compiler_params=_params("arbitrary"),
        name="moe_dispatch",
    )(slot_token, h, g.reshape(1, d))


def _combine_kernel(slot_ref, y_hbm, route_ref, h_ref, gpost_ref, o_ref, buf, sem, *, tm):
    t = pl.program_id(0)
    n_t = pl.num_programs(0)

    def issue(tile, slot):
        def body(r, c):
            for k in range(TOP_K):
                _row_copy(y_hbm, slot_ref[k, tile * tm + r], buf, slot, k * tm + r, sem).start()
            return c
        lax.fori_loop(0, tm, body, 0, unroll=GATHER_UNROLL // TOP_K)

    @pl.when(t == 0)
    def _():
        issue(0, 0)

    @pl.when(t + 1 < n_t)
    def _():
        issue(t + 1, (t + 1) % 2)

    slot = t % 2
    _wait_slot(y_hbm, buf, slot, sem)
    route = route_ref[...]
    mix = route[:, 0:1] * buf[slot, 0:tm, :]
    for k in range(1, TOP_K):
        mix = mix + route[:, k:k + 1] * buf[slot, k * tm:(k + 1) * tm, :]
    o_ref[...] = h_ref[...] + _rms(mix, gpost_ref[...])


def _combine(y_sorted, token_slots, route, h, g_post, tm):
    m, d = h.shape
    grid_spec = pltpu.PrefetchScalarGridSpec(
        num_scalar_prefetch=1,
        grid=(m // tm,),
        in_specs=[pl.BlockSpec(memory_space=pl.ANY),
                  pl.BlockSpec((tm, LANES), lambda t, sl: (t, 0)),
                  pl.BlockSpec((tm, d), lambda t, sl: (t, 0)),
                  pl.BlockSpec((1, d), lambda t, sl: (0, 0))],
        out_specs=pl.BlockSpec((tm, d), lambda t, sl: (t, 0)),
        scratch_shapes=[pltpu.VMEM((2, TOP_K * tm, d), F32), pltpu.SemaphoreType.DMA((2,))],
    )
    return pl.pallas_call(
        functools.partial(_combine_kernel, tm=tm), grid_spec=grid_spec,
        out_shape=jax.ShapeDtypeStruct((m, d), F32),
        compiler_params=_params("arbitrary"),
        name="moe_combine",
    )(token_slots, y_sorted, route, h, g_post.reshape(1, d))


def _routing_tables(route, n_experts, tm):
    m = route.shape[0]
    eid = jnp.concatenate([route[:, 2 + k].astype(I32) for k in range(TOP_K)])
    onehot = (eid[:, None] == jnp.arange(n_experts, dtype=I32)[None, :]).astype(I32)
    rank = jnp.sum((jnp.cumsum(onehot, axis=0) - onehot) * onehot, axis=1)
    counts = jnp.sum(onehot, axis=0)
    tiles_e = (counts + tm - 1) // tm
    tile_end = jnp.cumsum(tiles_e)
    start = (tile_end - tiles_e) * tm
    slot = start[eid] + rank
    n_tiles = (TOP_K * m + n_experts * (tm - 1)) // tm
    token = jnp.tile(jnp.arange(m, dtype=I32), TOP_K)
    slot_token = jnp.zeros((n_tiles * tm,), I32).at[slot].set(token)
    n_valid = tile_end[-1].astype(I32)
    tile_id = jnp.arange(n_tiles, dtype=I32)
    tile_src = jnp.minimum(tile_id, n_valid - 1)
    tile_expert = jnp.minimum(jnp.sum((tile_end[None, :] <= tile_src[:, None]).astype(I32), axis=1), n_experts - 1)
    return slot_token, slot.reshape(TOP_K, m).astype(I32), tile_expert, tile_src, n_valid.reshape(1)


def _pool_prompt_kernel(halo_ref, p_ref, w_ref, s_ref, o_ref, buf_ref, *, tiles_per_seq):
    i = pl.program_id(0) % tiles_per_seq
    t = p_ref.shape[0]
    gw = w_ref.shape[1]
    buf_ref[0:POOL_HALO, :] = jnp.where(i == 0, 0.0, halo_ref[...])
    buf_ref[POOL_HALO:POOL_HALO + t, :] = p_ref[...]
    pos = i * t + lax.broadcasted_iota(I32, (t, 1), 0)
    for g, w in enumerate(POOL_WINDOWS):
        c0 = g * gw
        cur = buf_ref[POOL_HALO:POOL_HALO + t, c0:c0 + gw]
        acc = cur
        for j in range(1, w):
            acc = acc + buf_ref[POOL_HALO - j:POOL_HALO - j + t, c0:c0 + gw]
        cnt = jnp.minimum(pos + 1, w).astype(F32)
        d = acc / cnt - cur
        y = jnp.dot(d.astype(BF16), w_ref[g], preferred_element_type=F32)
        o_ref[:, c0:c0 + gw] = (y * s_ref[:, c0:c0 + gw]).astype(o_ref.dtype)


def _pool_prompt(proj, n_rows, seq, w_pool, scale, t):
    c = scale.shape[0]
    halo_per_tile = t // POOL_HALO
    return pl.pallas_call(
        functools.partial(_pool_prompt_kernel, tiles_per_seq=seq // t),
        grid=(n_rows // t,),
        in_specs=[pl.BlockSpec((POOL_HALO, c), lambda i: (jnp.maximum(i * halo_per_tile - 1, 0), 0)),
                  pl.BlockSpec((t, c), lambda i: (i, 0)),
                  pl.BlockSpec(w_pool.shape, lambda i: (0, 0, 0)),
                  pl.BlockSpec((1, c), lambda i: (0, 0))],
        out_specs=pl.BlockSpec((t, c), lambda i: (i, 0)),
        out_shape=jax.ShapeDtypeStruct((n_rows, c), BF16),
        scratch_shapes=[pltpu.VMEM((POOL_HALO + t, c), F32)],
        compiler_params=_params("parallel"),
        name="pool_prompt",
    )(proj, proj, w_pool, scale.reshape(1, c))


def _pool_sample_kernel(ctx_ref, w_ref, s_ref, o_ref, *, n_ctx, pos0):
    gw = w_ref.shape[1]
    for t in range(o_ref.shape[0]):
        hi = n_ctx + t + 1
        for g, w in enumerate(POOL_WINDOWS):
            c0 = g * gw
            lo = max(hi - w, 0)
            acc = ctx_ref[lo, :, c0:c0 + gw]
            for r in range(lo + 1, hi):
                acc = acc + ctx_ref[r, :, c0:c0 + gw]
            d = acc / float(min(pos0 + t + 1, w)) - ctx_ref[hi - 1, :, c0:c0 + gw]
            y = jnp.dot(d.astype(BF16), w_ref[g], preferred_element_type=F32)
            o_ref[t, :, c0:c0 + gw] = (y * s_ref[:, c0:c0 + gw]).astype(o_ref.dtype)


def _pool_sample(ctx_t, n_ctx, pos0, w_pool, scale):
    total, n, c = ctx_t.shape
    return pl.pallas_call(
        functools.partial(_pool_sample_kernel, n_ctx=n_ctx, pos0=pos0),
        out_shape=jax.ShapeDtypeStruct((total - n_ctx, n, c), BF16),
        compiler_params=pltpu.CompilerParams(vmem_limit_bytes=VMEM_LIMIT_BYTES),
        name="pool_sample",
    )(ctx_t, w_pool, scale.reshape(1, c))


def _gmlp_kernel(u_ref, v_ref, w_ref, b_ref, o_ref):
    t = u_ref.shape[0]
    n_heads = w_ref.shape[0]
    ck = w_ref.shape[1]
    row = lax.broadcasted_iota(I32, (ck, ck), 0)
    col = lax.broadcasted_iota(I32, (ck, ck), 1)
    for h in range(n_heads):
        c0 = h * HEAD_DIM
        wm = jnp.where(col <= row, w_ref[h], 0.0).astype(BF16)
        bias = b_ref[:, h:h + 1]
        for k in range(t // ck):
            r0 = k * ck
            v = v_ref[r0:r0 + ck, c0:c0 + HEAD_DIM].astype(BF16)
            mixed = jnp.dot(wm, v, preferred_element_type=F32) + bias
            o_ref[r0:r0 + ck, c0:c0 + HEAD_DIM] = (u_ref[r0:r0 + ck, c0:c0 + HEAD_DIM] * mixed).astype(o_ref.dtype)


def _gmlp(proj, n_prompt_rows, w_pair, b_pair, t):
    m = proj.shape[0]
    c = proj.shape[1] // 3
    n_heads = w_pair.shape[1]
    first_sample_tile = n_prompt_rows // t
    which = lambda i: jnp.where(i >= first_sample_tile, 1, 0)
    return pl.pallas_call(
        _gmlp_kernel,
        grid=(m // t,),
        in_specs=[pl.BlockSpec((t, c), lambda i: (i, 1)),
                  pl.BlockSpec((t, c), lambda i: (i, 2)),
                  pl.BlockSpec((None, n_heads, GMLP_CHUNK, GMLP_CHUNK), lambda i: (which(i), 0, 0, 0)),
                  pl.BlockSpec((None, GMLP_CHUNK, n_heads), lambda i: (which(i), 0, 0))],
        out_specs=pl.BlockSpec((t, c), lambda i: (i, 0)),
        out_shape=jax.ShapeDtypeStruct((m, c), BF16),
        compiler_params=_params("parallel"),
        name="gmlp",
    )(proj, proj, w_pair, b_pair)


def _conv_prompt_kernel(xc_ref, bg_ref, cg_ref, xch_ref, cgh_ref, w_ref, o_ref, tail_ref, buf_ref, *, tiles_per_seq):
    i = pl.program_id(0) % tiles_per_seq
    t = xc_ref.shape[0]
    z = cg_ref[...] * xc_ref[...]
    buf_ref[0:CONV_HALO, :] = jnp.where(i == 0, 0.0, cgh_ref[...] * xch_ref[...])
    buf_ref[CONV_HALO:CONV_HALO + t, :] = z
    y = w_ref[0:1, :] * buf_ref[CONV_HALO - 2:CONV_HALO - 2 + t, :]
    for j in range(1, CONV_WIDTH):
        y = y + w_ref[j:j + 1, :] * buf_ref[CONV_HALO - 2 + j:CONV_HALO - 2 + j + t, :]
    o_ref[...] = (bg_ref[...] * y).astype(o_ref.dtype)
    tail_ref[...] = z[t - CONV_HALO:, :]


def _conv_prompt(proj, n_rows, seq, conv_w, t):
    c = conv_w.shape[1]
    halo_per_tile = t // CONV_HALO
    halo_row = lambda i: jnp.maximum(i * halo_per_tile - 1, 0)
    return pl.pallas_call(
        functools.partial(_conv_prompt_kernel, tiles_per_seq=seq // t),
        grid=(n_rows // t,),
        in_specs=[pl.BlockSpec((t, c), lambda i: (i, 0)),
                  pl.BlockSpec((t, c), lambda i: (i, 1)),
                  pl.BlockSpec((t, c), lambda i: (i, 2)),
                  pl.BlockSpec((CONV_HALO, c), lambda i: (halo_row(i), 0)),
                  pl.BlockSpec((CONV_HALO, c), lambda i: (halo_row(i), 2)),
                  pl.BlockSpec((CONV_WIDTH, c), lambda i: (0, 0))],
        out_specs=[pl.BlockSpec((t, c), lambda i: (i, 0)),
                   pl.BlockSpec((CONV_HALO, c), lambda i: (i, 0))],
        out_shape=[jax.ShapeDtypeStruct((n_rows, c), BF16),
                   jax.ShapeDtypeStruct((n_rows // t * CONV_HALO, c), F32)],
        scratch_shapes=[pltpu.VMEM((CONV_HALO + t, c), F32)],
        compiler_params=_params("parallel"),
        name="conv_prompt",
    )(proj, proj, proj, proj, proj, conv_w)


def _conv_sample_kernel(ctx_ref, xc_ref, bg_ref, cg_ref, w_ref, o_ref, z_ref):
    n_ctx = ctx_ref.shape[0]
    length = xc_ref.shape[0]
    rows = [ctx_ref[r] for r in range(n_ctx)]
    for t in range(length):
        z = cg_ref[t] * xc_ref[t]
        z_ref[t] = z
        rows.append(z)
    for t in range(length):
        y = w_ref[0:1, :] * rows[t]
        for j in range(1, CONV_WIDTH):
            y = y + w_ref[j:j + 1, :] * rows[t + j]
        o_ref[t] = (bg_ref[t] * y).astype(o_ref.dtype)


def _conv_sample(ctx_t, xc_t, bg_t, cg_t, conv_w):
    length, n, c = xc_t.shape
    return pl.pallas_call(
        _conv_sample_kernel,
        out_shape=[jax.ShapeDtypeStruct((length, n, c), BF16), jax.ShapeDtypeStruct((length, n, c), F32)],
        compiler_params=pltpu.CompilerParams(vmem_limit_bytes=VMEM_LIMIT_BYTES),
        name="conv_sample",
    )(ctx_t, xc_t, bg_t, cg_t, conv_w)


def _log_sigmoid(x):
    return jnp.minimum(x, 0.0) - jnp.log1p(jnp.exp(-jnp.abs(x)))


def _logf_kernel(f_ref, b_ref, lf_ref, c_ref, carry_ref, *, blocks_per_seq):
    i = pl.program_id(0)
    blk = f_ref.shape[0]
    lf = _log_sigmoid(f_ref[...] + b_ref[...])
    lf_ref[...] = lf

    @pl.when(i % blocks_per_seq == 0)
    def _():
        carry_ref[...] = jnp.zeros_like(carry_ref)

    row = lax.broadcasted_iota(I32, (blk, blk), 0)
    col = lax.broadcasted_iota(I32, (blk, blk), 1)
    tri = jnp.where(col <= row, 1.0, 0.0).astype(F32)
    c = jnp.dot(tri, lf, precision=HIGHEST, preferred_element_type=F32) + carry_ref[0:1, :]
    c_ref[...] = c
    carry_ref[...] = jnp.broadcast_to(c[blk - 1:blk, :], carry_ref.shape)


def _logf_cumsum(f_logit, b_forget, seq, blk):
    m = f_logit.shape[0]
    return pl.pallas_call(
        functools.partial(_logf_kernel, blocks_per_seq=seq // blk),
        grid=(m // blk,),
        in_specs=[pl.BlockSpec((blk, LANES), lambda i: (i, 0)), pl.BlockSpec((1, LANES), lambda i: (0, 0))],
        out_specs=[pl.BlockSpec((blk, LANES), lambda i: (i, 0)), pl.BlockSpec((blk, LANES), lambda i: (i, 0))],
        out_shape=[jax.ShapeDtypeStruct((m, LANES), F32), jax.ShapeDtypeStruct((m, LANES), F32)],
        scratch_shapes=[pltpu.VMEM((8, LANES), F32)],
        compiler_params=_params("arbitrary"),
        name="logf_cumsum",
    )(f_logit, b_forget)


def _flash_kernel(q_ref, k_ref, v_ref, cc_ref, cr_ref, o_ref, *, tq):
    h = pl.program_id(1)
    qb = pl.program_id(2)
    q = (q_ref[...] * (1.0 / math.sqrt(HEAD_DIM))).astype(BF16)
    lane = lax.broadcasted_iota(I32, (tq, LANES), 1)
    cq = jnp.sum(jnp.where(lane == h, cc_ref[...], 0.0), axis=-1, keepdims=True)

    def scores(j):
        start = pl.multiple_of(j * tq, tq)
        k = k_ref[pl.ds(start, tq), :].astype(BF16)
        v = v_ref[pl.ds(start, tq), :].astype(BF16)
        s = lax.dot_general(q, k, (((1,), (1,)), ((), ())), preferred_element_type=F32)
        ck = cr_ref[:, pl.ds(start, tq)]
        return s + (cq - ck), v

    def update(carry, s, v):
        m, l, acc = carry
        m_new = jnp.maximum(m, jnp.max(s, axis=-1, keepdims=True))
        alpha = jnp.exp(m - m_new)
        p = jnp.exp(s - m_new)
        l = alpha * l + jnp.sum(p, axis=-1, keepdims=True)
        acc = alpha * acc + jnp.dot(p.astype(BF16), v, preferred_element_type=F32)
        return m_new, l, acc

    def body(j, carry):
        s, v = scores(j)
        return update(carry, s, v)

    init = (jnp.full((tq, 1), MASKED, F32), jnp.zeros((tq, 1), F32), jnp.zeros((tq, HEAD_DIM), F32))
    carry = lax.fori_loop(0, qb, body, init)
    s, v = scores(qb)
    row = lax.broadcasted_iota(I32, (tq, tq), 0)
    col = lax.broadcasted_iota(I32, (tq, tq), 1)
    s = jnp.where(col <= row, s, MASKED)
    _, l, acc = update(carry, s, v)
    o_ref[...] = (acc / l).astype(o_ref.dtype)


def _flash_prompt(proj, c_col, c_row, n_seq, seq, n_heads, tq):
    q_blk, k_blk, v_blk = 3 * n_heads, 4 * n_heads, 5 * n_heads
    nqb = seq // tq
    return pl.pallas_call(
        functools.partial(_flash_kernel, tq=tq),
        grid=(n_seq, n_heads, nqb),
        in_specs=[pl.BlockSpec((tq, HEAD_DIM), lambda n, h, i: (n * nqb + i, q_blk + h)),
                  pl.BlockSpec((seq, HEAD_DIM), lambda n, h, i: (n, k_blk + h)),
                  pl.BlockSpec((seq, HEAD_DIM), lambda n, h, i: (n, v_blk + h)),
                  pl.BlockSpec((tq, LANES), lambda n, h, i: (n * nqb + i, 0)),
                  pl.BlockSpec((None, 1, seq), lambda n, h, i: (h, 0, n))],
        out_specs=pl.BlockSpec((tq, HEAD_DIM), lambda n, h, i: (n * nqb + i, h)),
        out_shape=jax.ShapeDtypeStruct((n_seq * seq, n_heads * HEAD_DIM), BF16),
        compiler_params=_params("parallel", "parallel", "arbitrary"),
        name="fox_prompt",
    )(proj, proj, proj, c_col, c_row)


def _decay_past_kernel(pt_ref, *refs, n_pages, n_heads):
    lf_refs, o_ref = refs[:n_pages], refs[n_pages]
    width = o_ref.shape[2]
    lf = jnp.concatenate([r[...] for r in lf_refs], axis=0)
    lane = lax.broadcasted_iota(I32, lf.shape, 1)
    incl, total = lf, lf
    shift = n_heads
    while shift < width:
        incl = incl + jnp.where(lane + shift < width, pltpu.roll(incl, width - shift, 1), 0.0)
        total = total + pltpu.roll(total, shift, 1)
        shift *= 2
    prow = lax.broadcasted_iota(I32, (n_pages, n_pages), 0)
    pcol = lax.broadcasted_iota(I32, (n_pages, n_pages), 1)
    later = jnp.where(pcol > prow, 1.0, 0.0).astype(F32)
    carry = jnp.dot(later, total, precision=HIGHEST, preferred_element_type=F32)
    o_ref[0] = (incl - lf) + carry


def _decay_past(logf_pages, page_table, n_heads):
    n, n_pages = page_table.shape
    width = logf_pages.shape[2]
    grid_spec = pltpu.PrefetchScalarGridSpec(
        num_scalar_prefetch=1,
        grid=(n,),
        in_specs=[pl.BlockSpec((None, 1, width), functools.partial(lambda i, pt, p: (pt[i * n_pages + p], 0, 0), p=p))
                  for p in range(n_pages)],
        out_specs=pl.BlockSpec((1, n_pages, width), lambda i, pt: (i, 0, 0)),
    )
    return pl.pallas_call(
        functools.partial(_decay_past_kernel, n_pages=n_pages, n_heads=n_heads),
        grid_spec=grid_spec,
        out_shape=jax.ShapeDtypeStruct((n, n_pages, width), F32),
        compiler_params=_params("parallel"),
        name="fox_decay_past",
    )(page_table.reshape(-1), *([logf_pages] * n_pages))


def _decode_kernel(pt_ref, q_ref, *refs, n_heads, n_q, group):
    k_refs, v_refs = refs[:group], refs[group:2 * group]
    dp_ref, kn_ref, vn_ref, lfn_ref, o_ref, m_ref, l_ref, acc_ref = refs[2 * group:]
    j = pl.program_id(1)
    rows = n_heads * n_q
    width = k_refs[0].shape[0] * n_heads

    @pl.when(j == 0)
    def _():
        m_ref[...] = jnp.full_like(m_ref, MASKED)
        l_ref[...] = jnp.zeros_like(l_ref)
        acc_ref[...] = jnp.zeros_like(acc_ref)

    q = (q_ref[0] * (1.0 / math.sqrt(HEAD_DIM))).astype(BF16)
    lfn = lfn_ref[0]
    r_i = lax.broadcasted_iota(I32, (rows, LANES), 0)
    c_i = lax.broadcasted_iota(I32, (rows, LANES), 1)
    sel = (c_i % n_heads == r_i // n_q) & (c_i // n_heads <= r_i % n_q) & (c_i < n_q * n_heads)
    cq = jnp.sum(jnp.where(sel, jnp.broadcast_to(lfn, (rows, LANES)), 0.0), axis=-1, keepdims=True)

    def update(scores, values):
        m_old = m_ref[:, 0:1]
        m_new = m_old
        for s in scores:
            m_new = jnp.maximum(m_new, jnp.max(s, axis=-1, keepdims=True))
        alpha = jnp.exp(m_old - m_new)
        l = alpha * l_ref[:, 0:1]
        acc = alpha * acc_ref[...]
        for s, v in zip(scores, values):
            p = jnp.exp(s - m_new)
            l = l + jnp.sum(p, axis=-1, keepdims=True)
            acc = acc + jnp.dot(p.astype(BF16), v, preferred_element_type=F32)
        l_ref[...] = jnp.broadcast_to(l, l_ref.shape)
        acc_ref[...] = acc
        m_ref[...] = jnp.broadcast_to(m_new, m_ref.shape)

    r_w = lax.broadcasted_iota(I32, (rows, width), 0)
    c_w = lax.broadcasted_iota(I32, (rows, width), 1)
    same_head = c_w % n_heads == r_w // n_q
    scores, values = [], []
    for p in range(group):
        k = k_refs[p][...].reshape(width, HEAD_DIM).astype(BF16)
        s = lax.dot_general(q, k, (((1,), (1,)), ((), ())), preferred_element_type=F32)
        scores.append(jnp.where(same_head, s + dp_ref[0, p:p + 1, :] + cq, MASKED))
        values.append(v_refs[p][...].reshape(width, HEAD_DIM).astype(BF16))
    update(scores, values)

    @pl.when(j == pl.num_programs(1) - 1)
    def _():
        kn = kn_ref[0].astype(BF16)
        vn = vn_ref[0].astype(BF16)
        sn = lax.dot_general(q, kn, (((1,), (1,)), ((), ())), preferred_element_type=F32)
        lf8 = jnp.broadcast_to(lfn, (8, LANES))
        lane8 = lax.broadcasted_iota(I32, (8, LANES), 1)
        csum = lf8
        shift = n_heads
        while shift < n_q * n_heads:
            csum = csum + jnp.where(lane8 >= shift, pltpu.roll(csum, shift, 1), 0.0)
            shift *= 2
        update([jnp.where(sel, sn + (cq - csum[0:1, :]), MASKED)], [vn])
        o_ref[0] = acc_ref[...] / l_ref[:, 0:1]


def _decode_attention(q_hq, cache_k, cache_v, layer, page_table, decay_past, k_new, v_new, lf_new, n_heads, n_q):
    n, n_pages = page_table.shape
    assert n_pages >= 1
    page = cache_k.shape[2]
    rows = n_heads * n_q
    width = page * n_heads
    group = max(g for g in (1, 2, 4, 8) if n_pages % g == 0)

    def page_spec(p):
        return pl.BlockSpec((None, None, page, n_heads, HEAD_DIM),
                            lambda i, j, pt: (layer, pt[i * n_pages + j * group + p], 0, 0, 0))

    per_seq = lambda shape: pl.BlockSpec(shape, lambda i, j, pt: (i, 0, 0))
    grid_spec = pltpu.PrefetchScalarGridSpec(
        num_scalar_prefetch=1,
        grid=(n, n_pages // group),
        in_specs=[per_seq((1, rows, HEAD_DIM))] + [page_spec(p) for p in range(group)] * 2
                 + [pl.BlockSpec((1, group, width), lambda i, j, pt: (i, j, 0)),
                    per_seq((1, LANES, HEAD_DIM)), per_seq((1, LANES, HEAD_DIM)), per_seq((1, 1, LANES))],
        out_specs=per_seq((1, rows, HEAD_DIM)),
        scratch_shapes=[pltpu.VMEM((rows, LANES), F32), pltpu.VMEM((rows, LANES), F32),
                        pltpu.VMEM((rows, HEAD_DIM), F32)],
    )
    return pl.pallas_call(
        functools.partial(_decode_kernel, n_heads=n_heads, n_q=n_q, group=group),
        grid_spec=grid_spec,
        out_shape=jax.ShapeDtypeStruct((n, rows, HEAD_DIM), F32),
        compiler_params=_params("parallel", "arbitrary"),
        name="fox_decode",
    )(page_table.reshape(-1), q_hq, *([cache_k] * group), *([cache_v] * group), decay_past, k_new, v_new, lf_new)


def kernel(x_prompt, x_sample, state_pool, state_conv, cache_k, cache_v, cache_logf, page_table, e_norm_mix_pre, e_norm_mix_post, e_norm_ffn_pre, e_norm_ffn_post, e_w_in, e_w_pool, e_pool_scale, e_w_spatial, e_b_spatial, e_w_out, e_ffn_gate, e_ffn_up, e_ffn_down, o_norm_mix_pre, o_norm_mix_post, o_norm_ffn_pre, o_norm_ffn_post, o_w_in, o_conv_w, o_b_forget, o_w_out, o_w_router, o_exp_gate, o_exp_up, o_exp_down):
    n_p, seq, d = x_prompt.shape
    n_s, dec, _ = x_sample.shape
    mp, ms = n_p * seq, n_s * dec
    m = mp + ms
    c = d // 2
    n_heads = c // HEAD_DIM
    n_pages = page_table.shape[1]
    page = cache_k.shape[2]
    past_len = n_pages * page
    n_layers = e_w_in.shape[0] + o_w_in.shape[0]
    n_experts = o_w_router.shape[2]
    assert ms % GMLP_CHUNK == 0 and seq % GMLP_CHUNK == 0 and past_len % GMLP_CHUNK == 0
    assert GMLP_CHUNK % min(dec, GMLP_CHUNK) == 0 and dec % min(dec, GMLP_CHUNK) == 0
    assert dec * n_heads <= LANES and dec >= CONV_WIDTH - 1

    tm = _tile(math.gcd(mp, ms), 512, LANES)
    t_seq = _tile(math.gcd(seq, tm), 512, LANES)
    tq = _tile(seq, 512, LANES)
    tm_moe = 256
    dense_tiles = m // tm
    dense_te = jnp.zeros((dense_tiles,), I32)
    dense_src = jnp.arange(dense_tiles, dtype=I32)
    dense_nv = jnp.full((1,), dense_tiles, I32)

    h = jnp.concatenate([x_prompt.reshape(mp, d), x_sample.reshape(ms, d)], axis=0)
    xn = _norm_cast(h, e_norm_mix_pre[0], tm)

    pool_p, pool_s, gv_s, conv_p, conv_s = [], [], [], [], []
    kp_l, vp_l, lp_l, ks_l, vs_l, ls_l = [], [], [], [], [], []

    def time_major(x2d):
        return jnp.transpose(x2d.reshape(n_s, dec, -1), (1, 0, 2))

    def row_major(x3d):
        return jnp.transpose(x3d, (1, 0, 2)).reshape(ms, -1)

    for layer in range(n_layers):
        i = layer // 2
        if layer % 2 == 0:
            g_next = e_norm_ffn_pre[i]
            proj = _mm(xn, e_w_in[i].astype(BF16), tm, _tile(3 * c, 1024, LANES))
            p_s = proj[mp:, :c].reshape(n_s, dec, c)
            ctx = jnp.concatenate([state_pool[i].astype(F32), p_s], axis=1)
            n_ctx = state_pool.shape[2]
            w_pool = e_w_pool[i].astype(BF16)
            a_p = _pool_prompt(proj, mp, seq, w_pool, e_pool_scale[i], t_seq)
            a_s = row_major(_pool_sample(jnp.transpose(ctx, (1, 0, 2)), n_ctx, past_len, w_pool, e_pool_scale[i]))
            a = jnp.concatenate([a_p, a_s], axis=0)
            cl = min(dec, GMLP_CHUNK)
            reps = GMLP_CHUNK // cl
            w_samp = jnp.einsum("ab,hts->hatbs", jnp.eye(reps, dtype=F32),
                                e_w_spatial[i][:, :cl, :cl]).reshape(n_heads, GMLP_CHUNK, GMLP_CHUNK)
            b_samp = jnp.tile(e_b_spatial[i][:, :cl], (1, reps))
            w_pair = jnp.stack([e_w_spatial[i], w_samp])
            b_pair = jnp.stack([e_b_spatial[i].T, b_samp.T])
            b = _gmlp(proj, mp, w_pair, b_pair, tm)
            h, xn = _proj_out(a, b, e_w_out[i].astype(BF16), h, e_norm_mix_post[i], g_next, tm)
            g_after = o_norm_mix_pre[i] if layer + 1 < n_layers else jnp.ones((d,), F32)
            mix = _swiglu_grouped(xn, e_ffn_gate[i][None], e_ffn_up[i][None], e_ffn_down[i][None],
                                  dense_te, dense_src, dense_nv, tm)
            h, xn = _norm_residual(mix, h, e_norm_ffn_post[i], g_after, tm)
            pool_p.append(proj[:mp, :c].reshape(n_p, seq, c)[:, seq - n_ctx:])
            pool_s.append(ctx[:, ctx.shape[1] - n_ctx:])
            gv_s.append(proj[mp:, 2 * c:].reshape(n_s, dec, c))
        else:
            w_in = o_w_in[i]
            proj = _mm(xn, w_in[:, :6 * c].astype(BF16), tm, _tile(6 * c, 1024, LANES))
            w_f = jnp.zeros((d, LANES), BF16).at[:, :n_heads].set(w_in[:, 6 * c:].astype(BF16))
            f_logit = _mm(xn, w_f, tm, LANES)
            b_f = jnp.zeros((1, LANES), F32).at[0, :n_heads].set(o_b_forget[i])
            logf, csum = _logf_cumsum(f_logit, b_f, seq, LANES)
            c_p, z_tail = _conv_prompt(proj, mp, seq, o_conv_w[i], t_seq)
            z_tail = z_tail.reshape(n_p, seq // t_seq, CONV_HALO, c)[:, -1, CONV_HALO - (CONV_WIDTH - 1):]
            xs = proj[mp:]
            conv_ctx = jnp.transpose(state_conv[i].astype(F32), (1, 0, 2))
            c_s, z_s = _conv_sample(conv_ctx, time_major(xs[:, :c]), time_major(xs[:, c:2 * c]),
                                    time_major(xs[:, 2 * c:3 * c]), o_conv_w[i])
            z_all = jnp.concatenate([conv_ctx, z_s], axis=0)
            c_row = csum[:mp, :n_heads].T.reshape(n_heads, 1, mp)
            att_p = _flash_prompt(proj, csum, c_row, n_p, seq, n_heads, tq)
            k_s = xs[:, 4 * c:5 * c]
            v_s = xs[:, 5 * c:6 * c]
            q_hq = jnp.transpose(xs[:, 3 * c:4 * c].reshape(n_s, dec, n_heads, HEAD_DIM), (0, 2, 1, 3))
            q_hq = q_hq.reshape(n_s, n_heads * dec, HEAD_DIM)
            pad_rows = LANES - dec * n_heads
            k_new = jnp.pad(k_s.reshape(n_s, dec * n_heads, HEAD_DIM), ((0, 0), (0, pad_rows), (0, 0)))
            v_new = jnp.pad(v_s.reshape(n_s, dec * n_heads, HEAD_DIM), ((0, 0), (0, pad_rows), (0, 0)))
            lf_new = jnp.pad(logf[mp:, :n_heads].reshape(n_s, 1, dec * n_heads), ((0, 0), (0, 0), (0, pad_rows)))
            logf_pages = cache_logf[i].astype(F32).reshape(cache_logf.shape[1], 1, page * n_heads)
            decay = _decay_past(logf_pages, page_table, n_heads)
            att_s = _decode_attention(q_hq, cache_k, cache_v, i, page_table, decay, k_new, v_new, lf_new,
                                      n_heads, dec)
            att_s = jnp.transpose(att_s.reshape(n_s, n_heads, dec, HEAD_DIM), (0, 2, 1, 3)).reshape(ms, c)
            cd = jnp.concatenate([c_p, row_major(c_s)], axis=0)
            att = jnp.concatenate([att_p, att_s.astype(BF16)], axis=0)
            h, _, route = _proj_out(cd, att, o_w_out[i].astype(BF16), h, o_norm_mix_post[i], o_norm_ffn_pre[i],
                                    tm, w_router=o_w_router[i])
            slot_token, token_slots, tile_expert, tile_src, n_valid = _routing_tables(route, n_experts, tm_moe)
            x_sorted = _dispatch(h, o_norm_ffn_pre[i], slot_token, tm_moe)
            y_sorted = _swiglu_grouped(x_sorted, o_exp_gate[i], o_exp_up[i], o_exp_down[i],
                                       tile_expert, tile_src, n_valid, tm_moe)
            h = _combine(y_sorted, token_slots, route, h, o_norm_ffn_post[i], _tile(m, tm_moe))
            if layer + 1 < n_layers:
                xn = _norm_cast(h, e_norm_mix_pre[i + 1], tm)
            conv_p.append(z_tail)
            conv_s.append(jnp.transpose(z_all[z_all.shape[0] - (CONV_WIDTH - 1):], (1, 0, 2)))
            kp_l.append(proj[:mp, 4 * c:5 * c].reshape(n_p, seq, n_heads, HEAD_DIM))
            vp_l.append(proj[:mp, 5 * c:6 * c].reshape(n_p, seq, n_heads, HEAD_DIM))
            lp_l.append(logf[:mp, :n_heads].reshape(n_p, seq, n_heads))
            ks_l.append(k_s.reshape(n_s, dec, n_heads, HEAD_DIM))
            vs_l.append(v_s.reshape(n_s, dec, n_heads, HEAD_DIM))
            ls_l.append(logf[mp:, :n_heads].reshape(n_s, dec, n_heads))

    return (h[:mp].reshape(n_p, seq, d), h[mp:].reshape(n_s, dec, d),
            jnp.stack(pool_p), jnp.stack(pool_s), jnp.stack(gv_s), jnp.stack(conv_p), jnp.stack(conv_s),
            jnp.stack(kp_l), jnp.stack(vp_l), jnp.stack(lp_l), jnp.stack(ks_l), jnp.stack(vs_l), jnp.stack(ls_l))
```

```python
import functools
import math

import jax
import jax.numpy as jnp
from jax import lax
from jax.experimental import pallas as pl
from jax.experimental.pallas import tpu as pltpu

F32, BF16, I32 = jnp.float32, jnp.bfloat16, jnp.int32
RMS_EPS = 1e-6
POOL_WINDOWS = (2, 4, 8, 16)
POOL_HALO = 16
CONV_WIDTH = 3
CONV_HALO = 8
HEAD_DIM = 128
GMLP_CHUNK = 128
TOP_K = 2
LANES = 128
MASKED = -1e30
VMEM_LIMIT_BYTES = 56 * 1024 * 1024
HIGHEST = lax.Precision.HIGHEST


def _params(*semantics):
    return pltpu.CompilerParams(dimension_semantics=semantics, vmem_limit_bytes=VMEM_LIMIT_BYTES)


def _tile(n, pref, mult=8):
    t = min(n, pref)
    t -= t % mult
    while t > mult and n % t:
        t -= mult
    assert t > 0 and n % t == 0, (n, pref, mult)
    return t


def _rms(x, g):
    return x * lax.rsqrt(jnp.mean(x * x, axis=-1, keepdims=True) + RMS_EPS) * g


def _norm_cast_kernel(x_ref, g_ref, o_ref):
    o_ref[...] = _rms(x_ref[...], g_ref[...]).astype(o_ref.dtype)


def _norm_cast(x, g, tm):
    m, d = x.shape
    return pl.pallas_call(
        _norm_cast_kernel,
        grid=(m // tm,),
        in_specs=[pl.BlockSpec((tm, d), lambda i: (i, 0)), pl.BlockSpec((1, d), lambda i: (0, 0))],
        out_specs=pl.BlockSpec((tm, d), lambda i: (i, 0)),
        out_shape=jax.ShapeDtypeStruct((m, d), BF16),
        compiler_params=_params("parallel"),
        name="norm_cast",
    )(x, g.reshape(1, d))


def _mm_kernel(x_ref, w_ref, o_ref):
    o_ref[...] = jnp.dot(x_ref[...], w_ref[...], preferred_element_type=F32)


def _mm(x, w, tm, tn):
    m, k = x.shape
    n = w.shape[1]
    return pl.pallas_call(
        _mm_kernel,
        grid=(n // tn, m // tm),
        in_specs=[pl.BlockSpec((tm, k), lambda j, i: (i, 0)), pl.BlockSpec((k, tn), lambda j, i: (0, j))],
        out_specs=pl.BlockSpec((tm, tn), lambda j, i: (i, j)),
        out_shape=jax.ShapeDtypeStruct((m, n), F32),
        compiler_params=_params("parallel", "parallel"),
        name="proj_in",
    )(x, w)


def _top2(logits, n_experts):
    lane = lax.broadcasted_iota(I32, logits.shape, 1)
    neg_inf = jnp.float32(-jnp.inf)
    l1 = jnp.where(lane < n_experts, logits, neg_inf)
    m1 = jnp.max(l1, axis=-1, keepdims=True)
    i1 = jnp.min(jnp.where(l1 == m1, lane, LANES), axis=-1, keepdims=True)
    l2 = jnp.where(lane == i1, neg_inf, l1)
    m2 = jnp.max(l2, axis=-1, keepdims=True)
    i2 = jnp.min(jnp.where(l2 == m2, lane, LANES), axis=-1, keepdims=True)
    e2 = jnp.exp(m2 - m1)
    g1 = 1.0 / (1.0 + e2)
    g2 = e2 / (1.0 + e2)
    return jnp.where(lane == 0, g1,
                     jnp.where(lane == 1, g2,
                               jnp.where(lane == 2, i1.astype(F32),
                                         jnp.where(lane == 3, i2.astype(F32), 0.0))))


def _proj_out_kernel(a_ref, b_ref, w_ref, h_ref, gpost_ref, gnext_ref, *rest, n_experts):
    half = a_ref.shape[1]
    m = (jnp.dot(a_ref[...], w_ref[:half, :], preferred_element_type=F32)
         + jnp.dot(b_ref[...], w_ref[half:, :], preferred_element_type=F32))
    hn = h_ref[...] + _rms(m, gpost_ref[...])
    xn = _rms(hn, gnext_ref[...])
    xn_hi = xn.astype(BF16)
    if n_experts:
        wr_hi_ref, wr_lo_ref, hn_ref, xn_ref, route_ref = rest
        xn_lo = (xn - xn_hi.astype(F32)).astype(BF16)
        logits = (jnp.dot(xn_hi, wr_hi_ref[...], preferred_element_type=F32)
                  + jnp.dot(xn_lo, wr_hi_ref[...], preferred_element_type=F32)
                  + jnp.dot(xn_hi, wr_lo_ref[...], preferred_element_type=F32))
        route_ref[...] = _top2(logits, n_experts)
    else:
        hn_ref, xn_ref = rest
    hn_ref[...] = hn
    xn_ref[...] = xn_hi


def _proj_out(a, b, w, h, g_post, g_next, tm, w_router=None):
    m, d = h.shape
    half = a.shape[1]
    n_experts = 0 if w_router is None else w_router.shape[1]
    row = lambda i: (i, 0)
    fixed = lambda i: (0, 0)
    in_specs = [pl.BlockSpec((tm, half), row), pl.BlockSpec((tm, half), row), pl.BlockSpec((2 * half, d), fixed),
                pl.BlockSpec((tm, d), row), pl.BlockSpec((1, d), fixed), pl.BlockSpec((1, d), fixed)]
    out_specs = [pl.BlockSpec((tm, d), row), pl.BlockSpec((tm, d), row)]
    out_shape = [jax.ShapeDtypeStruct((m, d), F32), jax.ShapeDtypeStruct((m, d), BF16)]
    args = [a, b, w, h, g_post.reshape(1, d), g_next.reshape(1, d)]
    if n_experts:
        wr = jnp.zeros((d, LANES), F32).at[:, :n_experts].set(w_router.astype(F32))
        wr_hi = wr.astype(BF16)
        wr_lo = (wr - wr_hi.astype(F32)).astype(BF16)
        in_specs += [pl.BlockSpec((d, LANES), fixed), pl.BlockSpec((d, LANES), fixed)]
        out_specs.append(pl.BlockSpec((tm, LANES), row))
        out_shape.append(jax.ShapeDtypeStruct((m, LANES), F32))
        args += [wr_hi, wr_lo]
    return pl.pallas_call(
        functools.partial(_proj_out_kernel, n_experts=n_experts),
        grid=(m // tm,),
        in_specs=in_specs, out_specs=out_specs, out_shape=out_shape,
        compiler_params=_params("parallel"),
        name="proj_out",
    )(*args)


def _norm_residual_kernel(m_ref, h_ref, gpost_ref, gnext_ref, hn_ref, xn_ref):
    hn = h_ref[...] + _rms(m_ref[...], gpost_ref[...])
    hn_ref[...] = hn
    xn_ref[...] = _rms(hn, gnext_ref[...]).astype(xn_ref.dtype)


def _norm_residual(mix, h, g_post, g_next, tm):
    m, d = h.shape
    row = lambda i: (i, 0)
    fixed = lambda i: (0, 0)
    return pl.pallas_call(
        _norm_residual_kernel,
        grid=(m // tm,),
        in_specs=[pl.BlockSpec((tm, d), row), pl.BlockSpec((tm, d), row),
                  pl.BlockSpec((1, d), fixed), pl.BlockSpec((1, d), fixed)],
        out_specs=[pl.BlockSpec((tm, d), row), pl.BlockSpec((tm, d), row)],
        out_shape=[jax.ShapeDtypeStruct((m, d), F32), jax.ShapeDtypeStruct((m, d), BF16)],
        compiler_params=_params("parallel"),
        name="norm_residual",
    )(mix, h, g_post.reshape(1, d), g_next.reshape(1, d))


def _ring_step(te_ref, nv_ref, nxt_ref, wrap_ref, cnt_ref, copies, consume):
    j = pl.program_id(0)
    t = pl.program_id(1)

    @pl.when(jnp.logical_and(j == 0, t == 0))
    def _():
        cnt_ref[0] = 0
        for c in copies(te_ref[0], 0, 0):
            c.start()

    group_start = jnp.logical_or(t == 0, te_ref[t] != te_ref[jnp.maximum(t - 1, 0)])

    @pl.when(jnp.logical_and(group_start, t < nv_ref[0]))
    def _():
        slot = cnt_ref[0] % 2
        for c in copies(te_ref[t], j, slot):
            c.wait()
        consume(slot)
        j_next = j + wrap_ref[t]

        @pl.when(j_next < pl.num_programs(0))
        def _():
            for c in copies(nxt_ref[t], j_next, 1 - slot):
                c.start()
        cnt_ref[0] = cnt_ref[0] + 1


def _ffn_up_kernel(te_ref, src_ref, nv_ref, nxt_ref, wrap_ref, x_ref, wg_hbm, wu_hbm, o_ref,
                   wbuf, wg_bf, wu_bf, sem, cnt_ref):
    t = pl.program_id(1)
    tf = wg_bf.shape[1]

    def copies(e, j, slot):
        cols = pl.ds(pl.multiple_of(j * tf, tf), tf)
        return (pltpu.make_async_copy(wg_hbm.at[e, :, cols], wbuf.at[slot, 0], sem.at[slot, 0]),
                pltpu.make_async_copy(wu_hbm.at[e, :, cols], wbuf.at[slot, 1], sem.at[slot, 1]))

    def consume(slot):
        wg_bf[...] = wbuf[slot, 0].astype(BF16)
        wu_bf[...] = wbuf[slot, 1].astype(BF16)

    _ring_step(te_ref, nv_ref, nxt_ref, wrap_ref, cnt_ref, copies, consume)

    @pl.when(t < nv_ref[0])
    def _():
        x = x_ref[...]
        g = jnp.dot(x, wg_bf[...], preferred_element_type=F32)
        u = jnp.dot(x, wu_bf[...], preferred_element_type=F32)
        o_ref[...] = (g * jax.nn.sigmoid(g) * u).astype(o_ref.dtype)

    @pl.when(t >= nv_ref[0])
    def _():
        o_ref[...] = jnp.zeros_like(o_ref)


def _ffn_up(x, w_gate, w_up, tiles, tm, tf):
    s, k = x.shape
    f = w_gate.shape[2]
    n_tiles = s // tm
    grid_spec = pltpu.PrefetchScalarGridSpec(
        num_scalar_prefetch=5,
        grid=(f // tf, n_tiles),
        in_specs=[pl.BlockSpec((tm, k), lambda j, t, te, src, nv, nxt, wrap: (src[t], 0)),
                  pl.BlockSpec(memory_space=pl.ANY), pl.BlockSpec(memory_space=pl.ANY)],
        out_specs=pl.BlockSpec((tm, tf), lambda j, t, te, src, nv, nxt, wrap: (t, j)),
        scratch_shapes=[pltpu.VMEM((2, 2, k, tf), F32), pltpu.VMEM((k, tf), BF16), pltpu.VMEM((k, tf), BF16),
                        pltpu.SemaphoreType.DMA((2, 2)), pltpu.SMEM((1,), I32)],
    )
    return pl.pallas_call(
        _ffn_up_kernel, grid_spec=grid_spec,
        out_shape=jax.ShapeDtypeStruct((s, f), BF16),
        compiler_params=_params("arbitrary", "arbitrary"),
        name="ffn_up",
    )(*tiles, x, w_gate, w_up)


def _ffn_down_kernel(te_ref, src_ref, nv_ref, nxt_ref, wrap_ref, a_ref, wd_hbm, o_ref, wbuf, wd_bf, sem, cnt_ref):
    t = pl.program_id(1)
    tn = wd_bf.shape[1]

    def copies(e, j, slot):
        cols = pl.ds(pl.multiple_of(j * tn, tn), tn)
        return (pltpu.make_async_copy(wd_hbm.at[e, :, cols], wbuf.at[slot], sem.at[slot]),)

    def consume(slot):
        wd_bf[...] = wbuf[slot].astype(BF16)

    _ring_step(te_ref, nv_ref, nxt_ref, wrap_ref, cnt_ref, copies, consume)

    @pl.when(t < nv_ref[0])
    def _():
        o_ref[...] = jnp.dot(a_ref[...], wd_bf[...], preferred_element_type=F32)

    @pl.when(t >= nv_ref[0])
    def _():
        o_ref[...] = jnp.zeros_like(o_ref)


def _ffn_down(a, w_down, tiles, tm, tn):
    s, f = a.shape
    d = w_down.shape[2]
    n_tiles = s // tm
    grid_spec = pltpu.PrefetchScalarGridSpec(
        num_scalar_prefetch=5,
        grid=(d // tn, n_tiles),
        in_specs=[pl.BlockSpec((tm, f), lambda j, t, te, src, nv, nxt, wrap: (src[t], 0)),
                  pl.BlockSpec(memory_space=pl.ANY)],
        out_specs=pl.BlockSpec((tm, tn), lambda j, t, te, src, nv, nxt, wrap: (t, j)),
        scratch_shapes=[pltpu.VMEM((2, f, tn), F32), pltpu.VMEM((f, tn), BF16),
                        pltpu.SemaphoreType.DMA((2,)), pltpu.SMEM((1,), I32)],
    )
    return pl.pallas_call(
        _ffn_down_kernel, grid_spec=grid_spec,
        out_shape=jax.ShapeDtypeStruct((s, d), F32),
        compiler_params=_params("arbitrary", "arbitrary"),
        name="ffn_down",
    )(*tiles, a, w_down)


def _swiglu_grouped(x, w_gate, w_up, w_down, tiles, tm):
    tf = _tile(w_gate.shape[2], 512, LANES)
    tn = _tile(w_down.shape[2], 512, LANES)
    act = _ffn_up(x, w_gate, w_up, tiles, tm, tf)
    return _ffn_down(act, w_down, tiles, tm, tn)


def _dense_tiles(n_tiles):
    zeros = jnp.zeros((n_tiles,), I32)
    return (zeros, jnp.arange(n_tiles, dtype=I32), jnp.full((1,), n_tiles, I32), zeros, jnp.ones((n_tiles,), I32))


GATHER_UNROLL = 8


def _row_copy(src_hbm, row, buf, slot, r, sem):
    return pltpu.make_async_copy(src_hbm.at[pl.ds(row, 1), :], buf.at[slot, pl.ds(r, 1), :], sem.at[slot])


def _wait_slot(src_hbm, buf, slot, sem):
    pltpu.make_async_copy(src_hbm.at[pl.ds(0, buf.shape[1]), :], buf.at[slot], sem.at[slot]).wait()


def _dispatch_kernel(tok_ref, h_hbm, g_ref, o_ref, buf, sem, *, tm):
    t = pl.program_id(0)
    n_t = pl.num_programs(0)

    def issue(tile, slot):
        def body(r, c):
            _row_copy(h_hbm, tok_ref[tile * tm + r], buf, slot, r, sem).start()
            return c
        lax.fori_loop(0, tm, body, 0, unroll=GATHER_UNROLL)

    @pl.when(t == 0)
    def _():
        issue(0, 0)

    @pl.when(t + 1 < n_t)
    def _():
        issue(t + 1, (t + 1) % 2)

    slot = t % 2
    _wait_slot(h_hbm, buf, slot, sem)
    o_ref[...] = _rms(buf[slot], g_ref[...]).astype(o_ref.dtype)


def _dispatch(h, g, slot_token, tm):
    s = slot_token.shape[0]
    d = h.shape[1]
    grid_spec = pltpu.PrefetchScalarGridSpec(
        num_scalar_prefetch=1,
        grid=(s // tm,),
        in_specs=[pl.BlockSpec(memory_space=pl.ANY), pl.BlockSpec((1, d), lambda t, tok: (0, 0))],
        out_specs=pl.BlockSpec((tm, d), lambda t, tok: (t, 0)),
        scratch_shapes=[pltpu.VMEM((2, tm, d), F32), pltpu.SemaphoreType.DMA((2,))],
    )
    return pl.pallas_call(
        functools.partial(_dispatch_kernel, tm=tm), grid_spec=grid_spec,
        out_shape=jax.ShapeDtypeStruct((s, d), BF16),
        compiler_params=_params("arbitrary"),
        name="moe_dispatch",
    )(slot_token, h, g.reshape(1, d))


def _combine_kernel(slot_ref, y_hbm, route_ref, h_ref, gpost_ref, o_ref, buf, sem, *, tm):
    t = pl.program_id(0)
    n_t = pl.num_programs(0)

    def issue(tile, slot):
        def body(r, c):
            for k in range(TOP_K):
                _row_copy(y_hbm, slot_ref[k, tile * tm + r], buf, slot, k * tm + r, sem).start()
            return c
        lax.fori_loop(0, tm, body, 0, unroll=GATHER_UNROLL // TOP_K)

    @pl.when(t == 0)
    def _():
        issue(0, 0)

    @pl.when(t + 1 < n_t)
    def _():
        issue(t + 1, (t + 1) % 2)

    slot = t % 2
    _wait_slot(y_hbm, buf, slot, sem)
    route = route_ref[...]
    mix = route[:, 0:1] * buf[slot, 0:tm, :]
    for k in range(1, TOP_K):
        mix = mix + route[:, k:k + 1] * buf[slot, k * tm:(k + 1) * tm, :]
    o_ref[...] = h_ref[...] + _rms(mix, gpost_ref[...])


def _combine(y_sorted, token_slots, route, h, g_post, tm):
    m, d = h.shape
    grid_spec = pltpu.PrefetchScalarGridSpec(
        num_scalar_prefetch=1,
        grid=(m // tm,),
        in_specs=[pl.BlockSpec(memory_space=pl.ANY),
                  pl.BlockSpec((tm, LANES), lambda t, sl: (t, 0)),
                  pl.BlockSpec((tm, d), lambda t, sl: (t, 0)),
                  pl.BlockSpec((1, d), lambda t, sl: (0, 0))],
        out_specs=pl.BlockSpec((tm, d), lambda t, sl: (t, 0)),
        scratch_shapes=[pltpu.VMEM((2, TOP_K * tm, d), F32), pltpu.SemaphoreType.DMA((2,))],
    )
    return pl.pallas_call(
        functools.partial(_combine_kernel, tm=tm), grid_spec=grid_spec,
        out_shape=jax.ShapeDtypeStruct((m, d), F32),
        compiler_params=_params("arbitrary"),
        name="moe_combine",
    )(token_slots, y_sorted, route, h, g_post.reshape(1, d))


def _routing_tables(route, n_experts, tm):
    m = route.shape[0]
    eid = jnp.concatenate([route[:, 2 + k].astype(I32) for k in range(TOP_K)])
    blk = LANES
    assert (TOP_K * m) % blk == 0 and TOP_K * m < 2 ** 24
    onehot = (eid[:, None] == jnp.arange(n_experts, dtype=I32)[None, :]).astype(F32).reshape(-1, blk, n_experts)
    before = jnp.tril(jnp.ones((blk, blk), F32), -1)
    within = jnp.einsum("ts,bse->bte", before, onehot)
    block_total = jnp.sum(onehot, axis=1)
    block_start = jnp.cumsum(block_total, axis=0) - block_total
    rank = jnp.sum((within + block_start[:, None, :]) * onehot, axis=-1).reshape(-1).astype(I32)
    counts = jnp.sum(block_total, axis=0).astype(I32)
    tiles_e = (counts + tm - 1) // tm
    tile_end = jnp.cumsum(tiles_e)
    start = (tile_end - tiles_e) * tm
    slot = start[eid] + rank
    n_tiles = (TOP_K * m + n_experts * (tm - 1)) // tm
    token = jnp.tile(jnp.arange(m, dtype=I32), TOP_K)
    slot_token = jnp.zeros((n_tiles * tm,), I32).at[slot].set(token)
    n_valid = tile_end[-1].astype(I32)
    tile_id = jnp.arange(n_tiles, dtype=I32)
    tile_src = jnp.minimum(tile_id, n_valid - 1)
    tile_expert = jnp.minimum(jnp.sum((tile_end[None, :] <= tile_src[:, None]).astype(I32), axis=1), n_experts - 1)
    following, cur = [], jnp.int32(-1)
    for e in reversed(range(n_experts)):
        following.append(cur)
        cur = jnp.where(tiles_e[e] > 0, jnp.int32(e), cur)
    following = jnp.stack(following[::-1])
    wraps = following < 0
    following = jnp.where(wraps, cur, following)
    tiles = (tile_expert, tile_src, n_valid.reshape(1), following[tile_expert], wraps[tile_expert].astype(I32))
    return slot_token, slot.reshape(TOP_K, m).astype(I32), tiles


def _pool_prompt_kernel(halo_ref, p_ref, w_ref, s_ref, o_ref, buf_ref, *, tiles_per_seq):
    i = pl.program_id(0) % tiles_per_seq
    t = p_ref.shape[0]
    gw = w_ref.shape[1]
    buf_ref[0:POOL_HALO, :] = jnp.where(i == 0, 0.0, halo_ref[...])
    buf_ref[POOL_HALO:POOL_HALO + t, :] = p_ref[...]
    pos = i * t + lax.broadcasted_iota(I32, (t, 1), 0)
    for g, w in enumerate(POOL_WINDOWS):
        c0 = g * gw
        cur = buf_ref[POOL_HALO:POOL_HALO + t, c0:c0 + gw]
        acc = cur
        for j in range(1, w):
            acc = acc + buf_ref[POOL_HALO - j:POOL_HALO - j + t, c0:c0 + gw]
        cnt = jnp.minimum(pos + 1, w).astype(F32)
        d = acc / cnt - cur
        y = jnp.dot(d.astype(BF16), w_ref[g], preferred_element_type=F32)
        o_ref[:, c0:c0 + gw] = (y * s_ref[:, c0:c0 + gw]).astype(o_ref.dtype)


def _pool_prompt(proj, n_rows, seq, w_pool, scale, t):
    c = scale.shape[0]
    halo_per_tile = t // POOL_HALO
    return pl.pallas_call(
        functools.partial(_pool_prompt_kernel, tiles_per_seq=seq // t),
        grid=(n_rows // t,),
        in_specs=[pl.BlockSpec((POOL_HALO, c), lambda i: (jnp.maximum(i * halo_per_tile - 1, 0), 0)),
                  pl.BlockSpec((t, c), lambda i: (i, 0)),
                  pl.BlockSpec(w_pool.shape, lambda i: (0, 0, 0)),
                  pl.BlockSpec((1, c), lambda i: (0, 0))],
        out_specs=pl.BlockSpec((t, c), lambda i: (i, 0)),
        out_shape=jax.ShapeDtypeStruct((n_rows, c), BF16),
        scratch_shapes=[pltpu.VMEM((POOL_HALO + t, c), F32)],
        compiler_params=_params("parallel"),
        name="pool_prompt",
    )(proj, proj, w_pool, scale.reshape(1, c))


def _pool_sample_kernel(ctx_ref, w_ref, s_ref, o_ref, *, n_ctx, pos0):
    gw = w_ref.shape[1]
    for t in range(o_ref.shape[0]):
        hi = n_ctx + t + 1
        for g, w in enumerate(POOL_WINDOWS):
            c0 = g * gw
            lo = max(hi - w, 0)
            acc = ctx_ref[lo, :, c0:c0 + gw]
            for r in range(lo + 1, hi):
                acc = acc + ctx_ref[r, :, c0:c0 + gw]
            d = acc / float(min(pos0 + t + 1, w)) - ctx_ref[hi - 1, :, c0:c0 + gw]
            y = jnp.dot(d.astype(BF16), w_ref[g], preferred_element_type=F32)
            o_ref[t, :, c0:c0 + gw] = (y * s_ref[:, c0:c0 + gw]).astype(o_ref.dtype)


def _pool_sample(ctx_t, n_ctx, pos0, w_pool, scale):
    total, n, c = ctx_t.shape
    return pl.pallas_call(
        functools.partial(_pool_sample_kernel, n_ctx=n_ctx, pos0=pos0),
        out_shape=jax.ShapeDtypeStruct((total - n_ctx, n, c), BF16),
        compiler_params=pltpu.CompilerParams(vmem_limit_bytes=VMEM_LIMIT_BYTES),
        name="pool_sample",
    )(ctx_t, w_pool, scale.reshape(1, c))


def _gmlp_kernel(u_ref, v_ref, w_ref, b_ref, o_ref):
    t = u_ref.shape[0]
    n_heads = w_ref.shape[0]
    ck = w_ref.shape[1]
    row = lax.broadcasted_iota(I32, (ck, ck), 0)
    col = lax.broadcasted_iota(I32, (ck, ck), 1)
    for h in range(n_heads):
        c0 = h * HEAD_DIM
        wm = jnp.where(col <= row, w_ref[h], 0.0).astype(BF16)
        bias = b_ref[:, h:h + 1]
        for k in range(t // ck):
            r0 = k * ck
            v = v_ref[r0:r0 + ck, c0:c0 + HEAD_DIM].astype(BF16)
            mixed = jnp.dot(wm, v, preferred_element_type=F32) + bias
            o_ref[r0:r0 + ck, c0:c0 + HEAD_DIM] = (u_ref[r0:r0 + ck, c0:c0 + HEAD_DIM] * mixed).astype(o_ref.dtype)


def _gmlp(proj, n_prompt_rows, w_pair, b_pair, t):
    m = proj.shape[0]
    c = proj.shape[1] // 3
    n_heads = w_pair.shape[1]
    first_sample_tile = n_prompt_rows // t
    which = lambda i: jnp.where(i >= first_sample_tile, 1, 0)
    return pl.pallas_call(
        _gmlp_kernel,
        grid=(m // t,),
        in_specs=[pl.BlockSpec((t, c), lambda i: (i, 1)),
                  pl.BlockSpec((t, c), lambda i: (i, 2)),
                  pl.BlockSpec((None, n_heads, GMLP_CHUNK, GMLP_CHUNK), lambda i: (which(i), 0, 0, 0)),
                  pl.BlockSpec((None, GMLP_CHUNK, n_heads), lambda i: (which(i), 0, 0))],
        out_specs=pl.BlockSpec((t, c), lambda i: (i, 0)),
        out_shape=jax.ShapeDtypeStruct((m, c), BF16),
        compiler_params=_params("parallel"),
        name="gmlp",
    )(proj, proj, w_pair, b_pair)


def _conv_prompt_kernel(xc_ref, bg_ref, cg_ref, xch_ref, cgh_ref, w_ref, o_ref, tail_ref, buf_ref, *, tiles_per_seq):
    i = pl.program_id(0) % tiles_per_seq
    t = xc_ref.shape[0]
    z = cg_ref[...] * xc_ref[...]
    buf_ref[0:CONV_HALO, :] = jnp.where(i == 0, 0.0, cgh_ref[...] * xch_ref[...])
    buf_ref[CONV_HALO:CONV_HALO + t, :] = z
    y = w_ref[0:1, :] * buf_ref[CONV_HALO - 2:CONV_HALO - 2 + t, :]
    for j in range(1, CONV_WIDTH):
        y = y + w_ref[j:j + 1, :] * buf_ref[CONV_HALO - 2 + j:CONV_HALO - 2 + j + t, :]
    o_ref[...] = (bg_ref[...] * y).astype(o_ref.dtype)
    tail_ref[...] = z[t - CONV_HALO:, :]


def _conv_prompt(proj, n_rows, seq, conv_w, t):
    c = conv_w.shape[1]
    halo_per_tile = t // CONV_HALO
    halo_row = lambda i: jnp.maximum(i * halo_per_tile - 1, 0)
    return pl.pallas_call(
        functools.partial(_conv_prompt_kernel, tiles_per_seq=seq // t),
        grid=(n_rows // t,),
        in_specs=[pl.BlockSpec((t, c), lambda i: (i, 0)),
                  pl.BlockSpec((t, c), lambda i: (i, 1)),
                  pl.BlockSpec((t, c), lambda i: (i, 2)),
                  pl.BlockSpec((CONV_HALO, c), lambda i: (halo_row(i), 0)),
                  pl.BlockSpec((CONV_HALO, c), lambda i: (halo_row(i), 2)),
                  pl.BlockSpec((CONV_WIDTH, c), lambda i: (0, 0))],
        out_specs=[pl.BlockSpec((t, c), lambda i: (i, 0)),
                   pl.BlockSpec((CONV_HALO, c), lambda i: (i, 0))],
        out_shape=[jax.ShapeDtypeStruct((n_rows, c), BF16),
                   jax.ShapeDtypeStruct((n_rows // t * CONV_HALO, c), F32)],
        scratch_shapes=[pltpu.VMEM((CONV_HALO + t, c), F32)],
        compiler_params=_params("parallel"),
        name="conv_prompt",
    )(proj, proj, proj, proj, proj, conv_w)


def _conv_sample_kernel(ctx_ref, xc_ref, bg_ref, cg_ref, w_ref, o_ref, z_ref):
    n_ctx = ctx_ref.shape[0]
    length = xc_ref.shape[0]
    rows = [ctx_ref[r] for r in range(n_ctx)]
    for t in range(length):
        z = cg_ref[t] * xc_ref[t]
        z_ref[t] = z
        rows.append(z)
    for t in range(length):
        y = w_ref[0:1, :] * rows[t]
        for j in range(1, CONV_WIDTH):
            y = y + w_ref[j:j + 1, :] * rows[t + j]
        o_ref[t] = (bg_ref[t] * y).astype(o_ref.dtype)


def _conv_sample(ctx_t, xc_t, bg_t, cg_t, conv_w):
    length, n, c = xc_t.shape
    return pl.pallas_call(
        _conv_sample_kernel,
        out_shape=[jax.ShapeDtypeStruct((length, n, c), BF16), jax.ShapeDtypeStruct((length, n, c), F32)],
        compiler_params=pltpu.CompilerParams(vmem_limit_bytes=VMEM_LIMIT_BYTES),
        name="conv_sample",
    )(ctx_t, xc_t, bg_t, cg_t, conv_w)


def _log_sigmoid(x):
    return jnp.minimum(x, 0.0) - jnp.log1p(jnp.exp(-jnp.abs(x)))


def _logf_kernel(f_ref, b_ref, lf_ref, c_ref, carry_ref, *, blocks_per_seq):
    i = pl.program_id(0)
    blk = f_ref.shape[0]
    lf = _log_sigmoid(f_ref[...] + b_ref[...])
    lf_ref[...] = lf

    @pl.when(i % blocks_per_seq == 0)
    def _():
        carry_ref[...] = jnp.zeros_like(carry_ref)

    row = lax.broadcasted_iota(I32, (blk, blk), 0)
    col = lax.broadcasted_iota(I32, (blk, blk), 1)
    tri = jnp.where(col <= row, 1.0, 0.0).astype(F32)
    c = jnp.dot(tri, lf, precision=HIGHEST, preferred_element_type=F32) + carry_ref[0:1, :]
    c_ref[...] = c
    carry_ref[...] = jnp.broadcast_to(c[blk - 1:blk, :], carry_ref.shape)


def _logf_cumsum(f_logit, b_forget, seq, blk):
    m = f_logit.shape[0]
    return pl.pallas_call(
        functools.partial(_logf_kernel, blocks_per_seq=seq // blk),
        grid=(m // blk,),
        in_specs=[pl.BlockSpec((blk, LANES), lambda i: (i, 0)), pl.BlockSpec((1, LANES), lambda i: (0, 0))],
        out_specs=[pl.BlockSpec((blk, LANES), lambda i: (i, 0)), pl.BlockSpec((blk, LANES), lambda i: (i, 0))],
        out_shape=[jax.ShapeDtypeStruct((m, LANES), F32), jax.ShapeDtypeStruct((m, LANES), F32)],
        scratch_shapes=[pltpu.VMEM((8, LANES), F32)],
        compiler_params=_params("arbitrary"),
        name="logf_cumsum",
    )(f_logit, b_forget)


def _flash_kernel(q_ref, k_ref, v_ref, cc_ref, cr_ref, o_ref, *, tq):
    h = pl.program_id(1)
    qb = pl.program_id(2)
    q = (q_ref[...] * (1.0 / math.sqrt(HEAD_DIM))).astype(BF16)
    lane = lax.broadcasted_iota(I32, (tq, LANES), 1)
    cq = jnp.sum(jnp.where(lane == h, cc_ref[...], 0.0), axis=-1, keepdims=True)

    def scores(j):
        start = pl.multiple_of(j * tq, tq)
        k = k_ref[pl.ds(start, tq), :].astype(BF16)
        v = v_ref[pl.ds(start, tq), :].astype(BF16)
        s = lax.dot_general(q, k, (((1,), (1,)), ((), ())), preferred_element_type=F32)
        ck = cr_ref[:, pl.ds(start, tq)]
        return s + (cq - ck), v

    def update(carry, s, v):
        m, l, acc = carry
        m_new = jnp.maximum(m, jnp.max(s, axis=-1, keepdims=True))
        alpha = jnp.exp(m - m_new)
        p = jnp.exp(s - m_new)
        l = alpha * l + jnp.sum(p, axis=-1, keepdims=True)
        acc = alpha * acc + jnp.dot(p.astype(BF16), v, preferred_element_type=F32)
        return m_new, l, acc

    def body(j, carry):
        s, v = scores(j)
        return update(carry, s, v)

    init = (jnp.full((tq, 1), MASKED, F32), jnp.zeros((tq, 1), F32), jnp.zeros((tq, HEAD_DIM), F32))
    carry = lax.fori_loop(0, qb, body, init)
    s, v = scores(qb)
    row = lax.broadcasted_iota(I32, (tq, tq), 0)
    col = lax.broadcasted_iota(I32, (tq, tq), 1)
    s = jnp.where(col <= row, s, MASKED)
    _, l, acc = update(carry, s, v)
    o_ref[...] = (acc / l).astype(o_ref.dtype)


def _flash_prompt(proj, c_col, c_row, n_seq, seq, n_heads, tq):
    q_blk, k_blk, v_blk = 3 * n_heads, 4 * n_heads, 5 * n_heads
    nqb = seq // tq
    return pl.pallas_call(
        functools.partial(_flash_kernel, tq=tq),
        grid=(n_seq, n_heads, nqb),
        in_specs=[pl.BlockSpec((tq, HEAD_DIM), lambda n, h, i: (n * nqb + i, q_blk + h)),
                  pl.BlockSpec((seq, HEAD_DIM), lambda n, h, i: (n, k_blk + h)),
                  pl.BlockSpec((seq, HEAD_DIM), lambda n, h, i: (n, v_blk + h)),
                  pl.BlockSpec((tq, LANES), lambda n, h, i: (n * nqb + i, 0)),
                  pl.BlockSpec((None, 1, seq), lambda n, h, i: (h, 0, n))],
        out_specs=pl.BlockSpec((tq, HEAD_DIM), lambda n, h, i: (n * nqb + i, h)),
        out_shape=jax.ShapeDtypeStruct((n_seq * seq, n_heads * HEAD_DIM), BF16),
        compiler_params=_params("parallel", "parallel", "arbitrary"),
        name="fox_prompt",
    )(proj, proj, proj, c_col, c_row)


def _decay_of_later_rows(lf, later_groups, n_heads):
    group, width = lf.shape
    lane = lax.broadcasted_iota(I32, lf.shape, 1)
    incl, total = lf, lf
    shift = n_heads
    while shift < width:
        incl = incl + jnp.where(lane + shift < width, pltpu.roll(incl, width - shift, 1), 0.0)
        total = total + pltpu.roll(total, shift, 1)
        shift *= 2
    later_pages = jnp.zeros_like(lf)
    if group > 1:
        prow = lax.broadcasted_iota(I32, (group, group), 0)
        pcol = lax.broadcasted_iota(I32, (group, group), 1)
        later = jnp.where(pcol > prow, 1.0, 0.0).astype(F32)
        later_pages = jnp.dot(later, total, precision=HIGHEST, preferred_element_type=F32)
    return (incl - lf) + later_pages + later_groups, jnp.sum(total, axis=0, keepdims=True)


def _decode_kernel(pt_ref, q_ref, *refs, n_heads, n_q, group):
    k_refs, v_refs, lf_refs = refs[:group], refs[group:2 * group], refs[2 * group:3 * group]
    kn_ref, vn_ref, lfn_ref, o_ref, m_ref, l_ref, acc_ref, later_ref = refs[3 * group:]
    j = pl.program_id(1)
    rows = n_heads * n_q
    width = k_refs[0].shape[0] * n_heads

    @pl.when(j == 0)
    def _():
        m_ref[...] = jnp.full_like(m_ref, MASKED)
        l_ref[...] = jnp.zeros_like(l_ref)
        acc_ref[...] = jnp.zeros_like(acc_ref)
        later_ref[...] = jnp.zeros_like(later_ref)

    lf = jnp.concatenate([r[...] for r in lf_refs], axis=0)
    decay, group_total = _decay_of_later_rows(lf, later_ref[0:1, :], n_heads)
    later_ref[...] = later_ref[...] + group_total

    q = (q_ref[0] * (1.0 / math.sqrt(HEAD_DIM))).astype(BF16)
    lfn = lfn_ref[0]
    r_i = lax.broadcasted_iota(I32, (rows, LANES), 0)
    c_i = lax.broadcasted_iota(I32, (rows, LANES), 1)
    sel = (c_i % n_heads == r_i // n_q) & (c_i // n_heads <= r_i % n_q) & (c_i < n_q * n_heads)
    cq = jnp.sum(jnp.where(sel, jnp.broadcast_to(lfn, (rows, LANES)), 0.0), axis=-1, keepdims=True)

    def update(scores, values):
        m_old = m_ref[:, 0:1]
        m_new = m_old
        for s in scores:
            m_new = jnp.maximum(m_new, jnp.max(s, axis=-1, keepdims=True))
        alpha = jnp.exp(m_old - m_new)
        l = alpha * l_ref[:, 0:1]
        acc = alpha * acc_ref[...]
        for s, v in zip(scores, values):
            p = jnp.exp(s - m_new)
            l = l + jnp.sum(p, axis=-1, keepdims=True)
            acc = acc + jnp.dot(p.astype(BF16), v, preferred_element_type=F32)
        l_ref[...] = jnp.broadcast_to(l, l_ref.shape)
        acc_ref[...] = acc
        m_ref[...] = jnp.broadcast_to(m_new, m_ref.shape)

    r_w = lax.broadcasted_iota(I32, (rows, width), 0)
    c_w = lax.broadcasted_iota(I32, (rows, width), 1)
    same_head = c_w % n_heads == r_w // n_q
    scores, values = [], []
    for p in range(group):
        k = k_refs[p][...].reshape(width, HEAD_DIM).astype(BF16)
        s = lax.dot_general(q, k, (((1,), (1,)), ((), ())), preferred_element_type=F32)
        scores.append(jnp.where(same_head, s + decay[p:p + 1, :] + cq, MASKED))
        values.append(v_refs[p][...].reshape(width, HEAD_DIM).astype(BF16))
    update(scores, values)

    @pl.when(j == pl.num_programs(1) - 1)
    def _():
        kn = kn_ref[0].astype(BF16)
        vn = vn_ref[0].astype(BF16)
        sn = lax.dot_general(q, kn, (((1,), (1,)), ((), ())), preferred_element_type=F32)
        lf8 = jnp.broadcast_to(lfn, (8, LANES))
        lane8 = lax.broadcasted_iota(I32, (8, LANES), 1)
        csum = lf8
        shift = n_heads
        while shift < n_q * n_heads:
            csum = csum + jnp.where(lane8 >= shift, pltpu.roll(csum, shift, 1), 0.0)
            shift *= 2
        update([jnp.where(sel, sn + (cq - csum[0:1, :]), MASKED)], [vn])
        o_ref[0] = acc_ref[...] / l_ref[:, 0:1]


def _decode_attention(q_hq, cache_k, cache_v, logf_pages, layer, page_table, k_new, v_new, lf_new, n_heads, n_q):
    n, n_pages = page_table.shape
    assert n_pages >= 1
    page = cache_k.shape[2]
    rows = n_heads * n_q
    width = page * n_heads
    group = max(g for g in (1, 2, 4, 8) if n_pages % g == 0)
    n_groups = n_pages // group

    def physical_page(i, j, pt, p):
        return pt[i * n_pages + (n_groups - 1 - j) * group + p]

    def page_spec(p):
        return pl.BlockSpec((None, None, page, n_heads, HEAD_DIM),
                            lambda i, j, pt: (layer, physical_page(i, j, pt, p), 0, 0, 0))

    def logf_spec(p):
        return pl.BlockSpec((None, 1, width), lambda i, j, pt: (physical_page(i, j, pt, p), 0, 0))

    per_seq = lambda shape: pl.BlockSpec(shape, lambda i, j, pt: (i, 0, 0))
    grid_spec = pltpu.PrefetchScalarGridSpec(
        num_scalar_prefetch=1,
        grid=(n, n_groups),
        in_specs=[per_seq((1, rows, HEAD_DIM))] + [page_spec(p) for p in range(group)] * 2
                 + [logf_spec(p) for p in range(group)]
                 + [per_seq((1, LANES, HEAD_DIM)), per_seq((1, LANES, HEAD_DIM)), per_seq((1, 1, LANES))],
        out_specs=per_seq((1, rows, HEAD_DIM)),
        scratch_shapes=[pltpu.VMEM((rows, LANES), F32), pltpu.VMEM((rows, LANES), F32),
                        pltpu.VMEM((rows, HEAD_DIM), F32), pltpu.VMEM((8, width), F32)],
    )
    return pl.pallas_call(
        functools.partial(_decode_kernel, n_heads=n_heads, n_q=n_q, group=group),
        grid_spec=grid_spec,
        out_shape=jax.ShapeDtypeStruct((n, rows, HEAD_DIM), F32),
        compiler_params=_params("parallel", "arbitrary"),
        name="fox_decode",
    )(page_table.reshape(-1), q_hq, *([cache_k] * group), *([cache_v] * group), *([logf_pages] * group),
      k_new, v_new, lf_new)


def kernel(x_prompt, x_sample, state_pool, state_conv, cache_k, cache_v, cache_logf, page_table, e_norm_mix_pre, e_norm_mix_post, e_norm_ffn_pre, e_norm_ffn_post, e_w_in, e_w_pool, e_pool_scale, e_w_spatial, e_b_spatial, e_w_out, e_ffn_gate, e_ffn_up, e_ffn_down, o_norm_mix_pre, o_norm_mix_post, o_norm_ffn_pre, o_norm_ffn_post, o_w_in, o_conv_w, o_b_forget, o_w_out, o_w_router, o_exp_gate, o_exp_up, o_exp_down):
    n_p, seq, d = x_prompt.shape
    n_s, dec, _ = x_sample.shape
    mp, ms = n_p * seq, n_s * dec
    m = mp + ms
    c = d // 2
    n_heads = c // HEAD_DIM
    n_pages = page_table.shape[1]
    page = cache_k.shape[2]
    past_len = n_pages * page
    n_layers = e_w_in.shape[0] + o_w_in.shape[0]
    n_experts = o_w_router.shape[2]
    assert ms % GMLP_CHUNK == 0 and seq % GMLP_CHUNK == 0 and past_len % GMLP_CHUNK == 0
    assert GMLP_CHUNK % min(dec, GMLP_CHUNK) == 0 and dec % min(dec, GMLP_CHUNK) == 0
    assert dec * n_heads <= LANES and dec >= CONV_WIDTH - 1

    tm = _tile(math.gcd(mp, ms), 512, LANES)
    t_seq = _tile(math.gcd(seq, tm), 512, LANES)
    tq = _tile(seq, 512, LANES)
    tm_moe = 256
    dense_tiles = _dense_tiles(m // tm)

    h = jnp.concatenate([x_prompt.reshape(mp, d), x_sample.reshape(ms, d)], axis=0)
    xn = _norm_cast(h, e_norm_mix_pre[0], tm)

    pool_p, pool_s, gv_s, conv_p, conv_s = [], [], [], [], []
    kp_l, vp_l, lp_l, ks_l, vs_l, ls_l = [], [], [], [], [], []

    def time_major(x2d):
        return jnp.transpose(x2d.reshape(n_s, dec, -1), (1, 0, 2))

    def row_major(x3d):
        return jnp.transpose(x3d, (1, 0, 2)).reshape(ms, -1)

    for layer in range(n_layers):
        i = layer // 2
        if layer % 2 == 0:
            g_next = e_norm_ffn_pre[i]
            proj = _mm(xn, e_w_in[i].astype(BF16), tm, _tile(3 * c, 1024, LANES))
            p_s = proj[mp:, :c].reshape(n_s, dec, c)
            ctx = jnp.concatenate([state_pool[i].astype(F32), p_s], axis=1)
            n_ctx = state_pool.shape[2]
            w_pool = e_w_pool[i].astype(BF16)
            a_p = _pool_prompt(proj, mp, seq, w_pool, e_pool_scale[i], t_seq)
            a_s = row_major(_pool_sample(jnp.transpose(ctx, (1, 0, 2)), n_ctx, past_len, w_pool, e_pool_scale[i]))
            a = jnp.concatenate([a_p, a_s], axis=0)
            cl = min(dec, GMLP_CHUNK)
            reps = GMLP_CHUNK // cl
            w_samp = jnp.einsum("ab,hts->hatbs", jnp.eye(reps, dtype=F32),
                                e_w_spatial[i][:, :cl, :cl]).reshape(n_heads, GMLP_CHUNK, GMLP_CHUNK)
            b_samp = jnp.tile(e_b_spatial[i][:, :cl], (1, reps))
            w_pair = jnp.stack([e_w_spatial[i], w_samp])
            b_pair = jnp.stack([e_b_spatial[i].T, b_samp.T])
            b = _gmlp(proj, mp, w_pair, b_pair, tm)
            h, xn = _proj_out(a, b, e_w_out[i].astype(BF16), h, e_norm_mix_post[i], g_next, tm)
            g_after = o_norm_mix_pre[i] if layer + 1 < n_layers else jnp.ones((d,), F32)
            mix = _swiglu_grouped(xn, e_ffn_gate[i][None], e_ffn_up[i][None], e_ffn_down[i][None], dense_tiles, tm)
            h, xn = _norm_residual(mix, h, e_norm_ffn_post[i], g_after, tm)
            pool_p.append(proj[:mp, :c].reshape(n_p, seq, c)[:, seq - n_ctx:])
            pool_s.append(ctx[:, ctx.shape[1] - n_ctx:])
            gv_s.append(proj[mp:, 2 * c:].reshape(n_s, dec, c))
        else:
            w_in = o_w_in[i]
            proj = _mm(xn, w_in[:, :6 * c].astype(BF16), tm, _tile(6 * c, 1024, LANES))
            w_f = jnp.zeros((d, LANES), BF16).at[:, :n_heads].set(w_in[:, 6 * c:].astype(BF16))
            f_logit = _mm(xn, w_f, tm, LANES)
            b_f = jnp.zeros((1, LANES), F32).at[0, :n_heads].set(o_b_forget[i])
            logf, csum = _logf_cumsum(f_logit, b_f, seq, LANES)
            c_p, z_tail = _conv_prompt(proj, mp, seq, o_conv_w[i], t_seq)
            z_tail = z_tail.reshape(n_p, seq // t_seq, CONV_HALO, c)[:, -1, CONV_HALO - (CONV_WIDTH - 1):]
            xs = proj[mp:]
            conv_ctx = jnp.transpose(state_conv[i].astype(F32), (1, 0, 2))
            c_s, z_s = _conv_sample(conv_ctx, time_major(xs[:, :c]), time_major(xs[:, c:2 * c]),
                                    time_major(xs[:, 2 * c:3 * c]), o_conv_w[i])
            z_all = jnp.concatenate([conv_ctx, z_s], axis=0)
            c_row = csum[:mp, :n_heads].T.reshape(n_heads, 1, mp)
            att_p = _flash_prompt(proj, csum, c_row, n_p, seq, n_heads, tq)
            k_s = xs[:, 4 * c:5 * c]
            v_s = xs[:, 5 * c:6 * c]
            q_hq = jnp.transpose(xs[:, 3 * c:4 * c].reshape(n_s, dec, n_heads, HEAD_DIM), (0, 2, 1, 3))
            q_hq = q_hq.reshape(n_s, n_heads * dec, HEAD_DIM)
            pad_rows = LANES - dec * n_heads
            k_new = jnp.pad(k_s.reshape(n_s, dec * n_heads, HEAD_DIM), ((0, 0), (0, pad_rows), (0, 0)))
            v_new = jnp.pad(v_s.reshape(n_s, dec * n_heads, HEAD_DIM), ((0, 0), (0, pad_rows), (0, 0)))
            lf_new = jnp.pad(logf[mp:, :n_heads].reshape(n_s, 1, dec * n_heads), ((0, 0), (0, 0), (0, pad_rows)))
            logf_pages = cache_logf[i].astype(F32).reshape(cache_logf.shape[1], 1, page * n_heads)
            att_s = _decode_attention(q_hq, cache_k, cache_v, logf_pages, i, page_table, k_new, v_new, lf_new,
                                      n_heads, dec)
            att_s = jnp.transpose(att_s.reshape(n_s, n_heads, dec, HEAD_DIM), (0, 2, 1, 3)).reshape(ms, c)
            cd = jnp.concatenate([c_p, row_major(c_s)], axis=0)
            att = jnp.concatenate([att_p, att_s.astype(BF16)], axis=0)
            h, _, route = _proj_out(cd, att, o_w_out[i].astype(BF16), h, o_norm_mix_post[i], o_norm_ffn_pre[i],
                                    tm, w_router=o_w_router[i])
            slot_token, token_slots, expert_tiles = _routing_tables(route, n_experts, tm_moe)
            x_sorted = _dispatch(h, o_norm_ffn_pre[i], slot_token, tm_moe)
            y_sorted = _swiglu_grouped(x_sorted, o_exp_gate[i], o_exp_up[i], o_exp_down[i], expert_tiles, tm_moe)
            h = _combine(y_sorted, token_slots, route, h, o_norm_ffn_post[i], _tile(m, tm_moe))
            if layer + 1 < n_layers:
                xn = _norm_cast(h, e_norm_mix_pre[i + 1], tm)
            conv_p.append(z_tail)
            conv_s.append(jnp.transpose(z_all[z_all.shape[0] - (CONV_WIDTH - 1):], (1, 0, 2)))
            kp_l.append(proj[:mp, 4 * c:5 * c].reshape(n_p, seq, n_heads, HEAD_DIM))
            vp_l.append(proj[:mp, 5 * c:6 * c].reshape(n_p, seq, n_heads, HEAD_DIM))
            lp_l.append(logf[:mp, :n_heads].reshape(n_p, seq, n_heads))
            ks_l.append(k_s.reshape(n_s, dec, n_heads, HEAD_DIM))
            vs_l.append(v_s.reshape(n_s, dec, n_heads, HEAD_DIM))
            ls_l.append(logf[mp:, :n_heads].reshape(n_s, dec, n_heads))

    return (h[:mp].reshape(n_p, seq, d), h[mp:].reshape(n_s, dec, d),
            jnp.stack(pool_p), jnp.stack(pool_s), jnp.stack(gv_s), jnp.stack(conv_p), jnp.stack(conv_s),
            jnp.stack(kp_l), jnp.stack(vp_l), jnp.stack(lp_l), jnp.stack(ks_l), jnp.stack(vs_l), jnp.stack(ls_l))
```

```python
import functools
import math

import jax
import jax.numpy as jnp
from jax import lax
from jax.experimental import pallas as pl
from jax.experimental.pallas import tpu as pltpu

F32, BF16, I32 = jnp.float32, jnp.bfloat16, jnp.int32
RMS_EPS = 1e-6
POOL_WINDOWS = (2, 4, 8, 16)
POOL_HALO = 16
CONV_WIDTH = 3
CONV_HALO = 8
HEAD_DIM = 128
GMLP_CHUNK = 128
TOP_K = 2
LANES = 128
MASKED = -1e30
VMEM_LIMIT_BYTES = 56 * 1024 * 1024
HIGHEST = lax.Precision.HIGHEST


def _params(*semantics):
    return pltpu.CompilerParams(dimension_semantics=semantics, vmem_limit_bytes=VMEM_LIMIT_BYTES)


def _tile(n, pref, mult=8):
    t = min(n, pref)
    t -= t % mult
    while t > mult and n % t:
        t -= mult
    assert t > 0 and n % t == 0, (n, pref, mult)
    return t


def _rms(x, g):
    return x * lax.rsqrt(jnp.mean(x * x, axis=-1, keepdims=True) + RMS_EPS) * g


def _row_specs(parts, tm):
    if len(parts) == 1:
        return [pl.BlockSpec((tm, parts[0].shape[1]), lambda i, *_: (i, 0))]
    first_sample_tile = parts[0].shape[0] // tm
    return [pl.BlockSpec((tm, parts[0].shape[1]), lambda i, *_: (jnp.minimum(i, first_sample_tile - 1), 0)),
            pl.BlockSpec((tm, parts[1].shape[1]), lambda i, *_: (jnp.maximum(i - first_sample_tile, 0), 0))]


def _read_rows(refs, first_sample_tile):
    if len(refs) == 1:
        return refs[0][...]
    return jnp.where(pl.program_id(0) < first_sample_tile, refs[0][...], refs[1][...])


def _norm_cast_kernel(*refs, first_sample_tile):
    g_ref, o_ref = refs[-2:]
    o_ref[...] = _rms(_read_rows(refs[:-2], first_sample_tile), g_ref[...]).astype(o_ref.dtype)


def _norm_cast(x_parts, g, tm):
    m = sum(p.shape[0] for p in x_parts)
    d = x_parts[0].shape[1]
    return pl.pallas_call(
        functools.partial(_norm_cast_kernel, first_sample_tile=x_parts[0].shape[0] // tm),
        grid=(m // tm,),
        in_specs=_row_specs(x_parts, tm) + [pl.BlockSpec((1, d), lambda i: (0, 0))],
        out_specs=pl.BlockSpec((tm, d), lambda i: (i, 0)),
        out_shape=jax.ShapeDtypeStruct((m, d), BF16),
        compiler_params=_params("parallel"),
        name="norm_cast",
    )(*x_parts, g.reshape(1, d))


def _mm_kernel(x_ref, w_ref, o_ref, *copy_refs, copy_cols, copy_tiles):
    y = jnp.dot(x_ref[...], w_ref[...], preferred_element_type=F32)
    o_ref[...] = y
    for ref, col in zip(copy_refs, copy_cols):
        @pl.when(jnp.logical_and(pl.program_id(0) == col, pl.program_id(1) < copy_tiles))
        def _(ref=ref):
            ref[...] = y


def _mm(x, w, tm, tn, copy_cols=(), copy_rows=0):
    m, k = x.shape
    n = w.shape[1]
    copy_tiles = copy_rows // tm
    assert copy_rows % tm == 0

    def copy_spec(col):
        def index(j, i):
            row = jnp.where(j < col, 0, jnp.where(j > col, copy_tiles - 1, jnp.minimum(i, copy_tiles - 1)))
            return (row, 0)
        return pl.BlockSpec((tm, tn), index)

    out = pl.pallas_call(
        functools.partial(_mm_kernel, copy_cols=tuple(copy_cols), copy_tiles=copy_tiles),
        grid=(n // tn, m // tm),
        in_specs=[pl.BlockSpec((tm, k), lambda j, i: (i, 0)), pl.BlockSpec((k, tn), lambda j, i: (0, j))],
        out_specs=[pl.BlockSpec((tm, tn), lambda j, i: (i, j))] + [copy_spec(col) for col in copy_cols],
        out_shape=[jax.ShapeDtypeStruct((m, n), F32)] + [jax.ShapeDtypeStruct((copy_rows, tn), F32)] * len(copy_cols),
        compiler_params=_params("arbitrary", "arbitrary"),
        name="proj_in",
    )(x, w)
    return out if copy_cols else out[0]


def _top2(logits, n_experts):
    lane = lax.broadcasted_iota(I32, logits.shape, 1)
    neg_inf = jnp.float32(-jnp.inf)
    l1 = jnp.where(lane < n_experts, logits, neg_inf)
    m1 = jnp.max(l1, axis=-1, keepdims=True)
    i1 = jnp.min(jnp.where(l1 == m1, lane, LANES), axis=-1, keepdims=True)
    l2 = jnp.where(lane == i1, neg_inf, l1)
    m2 = jnp.max(l2, axis=-1, keepdims=True)
    i2 = jnp.min(jnp.where(l2 == m2, lane, LANES), axis=-1, keepdims=True)
    e2 = jnp.exp(m2 - m1)
    g1 = 1.0 / (1.0 + e2)
    g2 = e2 / (1.0 + e2)
    return jnp.where(lane == 0, g1,
                     jnp.where(lane == 1, g2,
                               jnp.where(lane == 2, i1.astype(F32),
                                         jnp.where(lane == 3, i2.astype(F32), 0.0))))


def _proj_out_kernel(*refs, n_experts, layout, first_sample_tile):
    na, nb, nh = layout
    a = _read_rows(refs[:na], first_sample_tile)
    b = _read_rows(refs[na:na + nb], first_sample_tile)
    h = _read_rows(refs[na + nb:na + nb + nh], first_sample_tile)
    w_ref, gpost_ref, gnext_ref, *rest = refs[na + nb + nh:]
    half = a.shape[1]
    m = (jnp.dot(a, w_ref[:half, :], preferred_element_type=F32)
         + jnp.dot(b, w_ref[half:, :], preferred_element_type=F32))
    hn = h + _rms(m, gpost_ref[...])
    xn = _rms(hn, gnext_ref[...])
    xn_hi = xn.astype(BF16)
    if n_experts:
        wr_hi_ref, wr_lo_ref, hn_ref, xn_ref, route_ref = rest
        xn_lo = (xn - xn_hi.astype(F32)).astype(BF16)
        logits = (jnp.dot(xn_hi, wr_hi_ref[...], preferred_element_type=F32)
                  + jnp.dot(xn_lo, wr_hi_ref[...], preferred_element_type=F32)
                  + jnp.dot(xn_hi, wr_lo_ref[...], preferred_element_type=F32))
        route_ref[...] = _top2(logits, n_experts)
    else:
        hn_ref, xn_ref = rest
    hn_ref[...] = hn
    xn_ref[...] = xn_hi


def _proj_out(a_parts, b_parts, w, h_parts, g_post, g_next, tm, n_prompt_rows, w_router=None):
    m = sum(p.shape[0] for p in h_parts)
    d = h_parts[0].shape[1]
    half = a_parts[0].shape[1]
    n_experts = 0 if w_router is None else w_router.shape[1]
    first_sample_tile = n_prompt_rows // tm
    assert all(len(p) == 1 or p[0].shape[0] == n_prompt_rows for p in (a_parts, b_parts, h_parts))
    row = lambda i: (i, 0)
    fixed = lambda i: (0, 0)
    in_specs = (_row_specs(a_parts, tm) + _row_specs(b_parts, tm) + _row_specs(h_parts, tm)
                + [pl.BlockSpec((2 * half, d), fixed), pl.BlockSpec((1, d), fixed), pl.BlockSpec((1, d), fixed)])
    out_specs = [pl.BlockSpec((tm, d), row), pl.BlockSpec((tm, d), row)]
    out_shape = [jax.ShapeDtypeStruct((m, d), F32), jax.ShapeDtypeStruct((m, d), BF16)]
    args = [*a_parts, *b_parts, *h_parts, w, g_post.reshape(1, d), g_next.reshape(1, d)]
    if n_experts:
        wr = jnp.zeros((d, LANES), F32).at[:, :n_experts].set(w_router.astype(F32))
        wr_hi = wr.astype(BF16)
        wr_lo = (wr - wr_hi.astype(F32)).astype(BF16)
        in_specs += [pl.BlockSpec((d, LANES), fixed), pl.BlockSpec((d, LANES), fixed)]
        out_specs.append(pl.BlockSpec((tm, LANES), row))
        out_shape.append(jax.ShapeDtypeStruct((m, LANES), F32))
        args += [wr_hi, wr_lo]
    return pl.pallas_call(
        functools.partial(_proj_out_kernel, n_experts=n_experts, first_sample_tile=first_sample_tile,
                          layout=(len(a_parts), len(b_parts), len(h_parts))),
        grid=(m // tm,),
        in_specs=in_specs, out_specs=out_specs, out_shape=out_shape,
        compiler_params=_params("parallel"),
        name="proj_out",
    )(*args)


def _norm_residual_kernel(m_ref, h_ref, gpost_ref, gnext_ref, hn_ref, xn_ref):
    hn = h_ref[...] + _rms(m_ref[...], gpost_ref[...])
    hn_ref[...] = hn
    xn_ref[...] = _rms(hn, gnext_ref[...]).astype(xn_ref.dtype)


def _norm_residual(mix, h, g_post, g_next, tm):
    m, d = h.shape
    row = lambda i: (i, 0)
    fixed = lambda i: (0, 0)
    return pl.pallas_call(
        _norm_residual_kernel,
        grid=(m // tm,),
        in_specs=[pl.BlockSpec((tm, d), row), pl.BlockSpec((tm, d), row),
                  pl.BlockSpec((1, d), fixed), pl.BlockSpec((1, d), fixed)],
        out_specs=[pl.BlockSpec((tm, d), row), pl.BlockSpec((tm, d), row)],
        out_shape=[jax.ShapeDtypeStruct((m, d), F32), jax.ShapeDtypeStruct((m, d), BF16)],
        compiler_params=_params("parallel"),
        name="norm_residual",
    )(mix, h, g_post.reshape(1, d), g_next.reshape(1, d))


def _ring_step(te_ref, nv_ref, nxt_ref, wrap_ref, cnt_ref, copies, consume):
    j = pl.program_id(0)
    t = pl.program_id(1)

    @pl.when(jnp.logical_and(j == 0, t == 0))
    def _():
        cnt_ref[0] = 0
        for c in copies(te_ref[0], 0, 0):
            c.start()

    group_start = jnp.logical_or(t == 0, te_ref[t] != te_ref[jnp.maximum(t - 1, 0)])

    @pl.when(jnp.logical_and(group_start, t < nv_ref[0]))
    def _():
        slot = cnt_ref[0] % 2
        for c in copies(te_ref[t], j, slot):
            c.wait()
        consume(slot)
        j_next = j + wrap_ref[t]

        @pl.when(j_next < pl.num_programs(0))
        def _():
            for c in copies(nxt_ref[t], j_next, 1 - slot):
                c.start()
        cnt_ref[0] = cnt_ref[0] + 1


def _ffn_up_kernel(te_ref, src_ref, nv_ref, nxt_ref, wrap_ref, x_ref, wg_hbm, wu_hbm, o_ref,
                   wbuf, wg_bf, wu_bf, sem, cnt_ref):
    t = pl.program_id(1)
    tf = wg_bf.shape[1]

    def copies(e, j, slot):
        cols = pl.ds(pl.multiple_of(j * tf, tf), tf)
        return (pltpu.make_async_copy(wg_hbm.at[e, :, cols], wbuf.at[slot, 0], sem.at[slot, 0]),
                pltpu.make_async_copy(wu_hbm.at[e, :, cols], wbuf.at[slot, 1], sem.at[slot, 1]))

    def consume(slot):
        wg_bf[...] = wbuf[slot, 0].astype(BF16)
        wu_bf[...] = wbuf[slot, 1].astype(BF16)

    _ring_step(te_ref, nv_ref, nxt_ref, wrap_ref, cnt_ref, copies, consume)

    @pl.when(t < nv_ref[0])
    def _():
        x = x_ref[...]
        g = jnp.dot(x, wg_bf[...], preferred_element_type=F32)
        u = jnp.dot(x, wu_bf[...], preferred_element_type=F32)
        o_ref[...] = (g * jax.nn.sigmoid(g) * u).astype(o_ref.dtype)

    @pl.when(t >= nv_ref[0])
    def _():
        o_ref[...] = jnp.zeros_like(o_ref)


def _ffn_up(x, w_gate, w_up, tiles, tm, tf):
    s, k = x.shape
    f = w_gate.shape[2]
    n_tiles = s // tm
    grid_spec = pltpu.PrefetchScalarGridSpec(
        num_scalar_prefetch=5,
        grid=(f // tf, n_tiles),
        in_specs=[pl.BlockSpec((tm, k), lambda j, t, te, src, nv, nxt, wrap: (src[t], 0)),
                  pl.BlockSpec(memory_space=pl.ANY), pl.BlockSpec(memory_space=pl.ANY)],
        out_specs=pl.BlockSpec((tm, tf), lambda j, t, te, src, nv, nxt, wrap: (t, j)),
        scratch_shapes=[pltpu.VMEM((2, 2, k, tf), F32), pltpu.VMEM((k, tf), BF16), pltpu.VMEM((k, tf), BF16),
                        pltpu.SemaphoreType.DMA((2, 2)), pltpu.SMEM((1,), I32)],
    )
    return pl.pallas_call(
        _ffn_up_kernel, grid_spec=grid_spec,
        out_shape=jax.ShapeDtypeStruct((s, f), BF16),
        compiler_params=_params("arbitrary", "arbitrary"),
        name="ffn_up",
    )(*tiles, x, w_gate, w_up)


def _ffn_down_kernel(te_ref, src_ref, nv_ref, nxt_ref, wrap_ref, a_ref, wd_hbm, o_ref, wbuf, wd_bf, sem, cnt_ref):
    t = pl.program_id(1)
    tn = wd_bf.shape[1]

    def copies(e, j, slot):
        cols = pl.ds(pl.multiple_of(j * tn, tn), tn)
        return (pltpu.make_async_copy(wd_hbm.at[e, :, cols], wbuf.at[slot], sem.at[slot]),)

    def consume(slot):
        wd_bf[...] = wbuf[slot].astype(BF16)

    _ring_step(te_ref, nv_ref, nxt_ref, wrap_ref, cnt_ref, copies, consume)

    @pl.when(t < nv_ref[0])
    def _():
        o_ref[...] = jnp.dot(a_ref[...], wd_bf[...], preferred_element_type=F32)

    @pl.when(t >= nv_ref[0])
    def _():
        o_ref[...] = jnp.zeros_like(o_ref)


def _ffn_down(a, w_down, tiles, tm, tn):
    s, f = a.shape
    d = w_down.shape[2]
    n_tiles = s // tm
    grid_spec = pltpu.PrefetchScalarGridSpec(
        num_scalar_prefetch=5,
        grid=(d // tn, n_tiles),
        in_specs=[pl.BlockSpec((tm, f), lambda j, t, te, src, nv, nxt, wrap: (src[t], 0)),
                  pl.BlockSpec(memory_space=pl.ANY)],
        out_specs=pl.BlockSpec((tm, tn), lambda j, t, te, src, nv, nxt, wrap: (t, j)),
        scratch_shapes=[pltpu.VMEM((2, f, tn), F32), pltpu.VMEM((f, tn), BF16),
                        pltpu.SemaphoreType.DMA((2,)), pltpu.SMEM((1,), I32)],
    )
    return pl.pallas_call(
        _ffn_down_kernel, grid_spec=grid_spec,
        out_shape=jax.ShapeDtypeStruct((s, d), F32),
        compiler_params=_params("arbitrary", "arbitrary"),
        name="ffn_down",
    )(*tiles, a, w_down)


def _swiglu_grouped(x, w_gate, w_up, w_down, tiles, tm):
    tf = _tile(w_gate.shape[2], 512, LANES)
    tn = _tile(w_down.shape[2], 512, LANES)
    act = _ffn_up(x, w_gate, w_up, tiles, tm, tf)
    return _ffn_down(act, w_down, tiles, tm, tn)


def _dense_tiles(n_tiles):
    zeros = jnp.zeros((n_tiles,), I32)
    return (zeros, jnp.arange(n_tiles, dtype=I32), jnp.full((1,), n_tiles, I32), zeros, jnp.ones((n_tiles,), I32))


GATHER_UNROLL = 8


def _row_copy(src_hbm, row, buf, slot, r, sem):
    return pltpu.make_async_copy(src_hbm.at[pl.ds(row, 1), :], buf.at[slot, pl.ds(r, 1), :], sem.at[slot])


def _wait_slot(src_hbm, buf, slot, sem):
    pltpu.make_async_copy(src_hbm.at[pl.ds(0, buf.shape[1]), :], buf.at[slot], sem.at[slot]).wait()


def _dispatch_kernel(tok_ref, h_hbm, g_ref, o_ref, buf, sem, *, tm):
    t = pl.program_id(0)
    n_t = pl.num_programs(0)

    def issue(tile, slot):
        def body(r, c):
            _row_copy(h_hbm, tok_ref[tile * tm + r], buf, slot, r, sem).start()
            return c
        lax.fori_loop(0, tm, body, 0, unroll=GATHER_UNROLL)

    @pl.when(t == 0)
    def _():
        issue(0, 0)

    @pl.when(t + 1 < n_t)
    def _():
        issue(t + 1, (t + 1) % 2)

    slot = t % 2
    _wait_slot(h_hbm, buf, slot, sem)
    o_ref[...] = _rms(buf[slot], g_ref[...]).astype(o_ref.dtype)


def _dispatch(h, g, slot_token, tm):
    s = slot_token.shape[0]
    d = h.shape[1]
    grid_spec = pltpu.PrefetchScalarGridSpec(
        num_scalar_prefetch=1,
        grid=(s // tm,),
        in_specs=[pl.BlockSpec(memory_space=pl.ANY), pl.BlockSpec((1, d), lambda t, tok: (0, 0))],
        out_specs=pl.BlockSpec((tm, d), lambda t, tok: (t, 0)),
        scratch_shapes=[pltpu.VMEM((2, tm, d), F32), pltpu.SemaphoreType.DMA((2,))],
    )
    return pl.pallas_call(
        functools.partial(_dispatch_kernel, tm=tm), grid_spec=grid_spec,
        out_shape=jax.ShapeDtypeStruct((s, d), BF16),
        compiler_params=_params("arbitrary"),
        name="moe_dispatch",
    )(slot_token, h, g.reshape(1, d))


def _combine_kernel(slot_ref, y_hbm, route_ref, h_ref, gpost_ref, *rest, tm, first_sample_tile):
    out_refs, (buf, sem) = rest[:-2], rest[-2:]
    t = pl.program_id(0)
    n_t = pl.num_programs(0)

    def issue(tile, slot):
        def body(r, c):
            for k in range(TOP_K):
                _row_copy(y_hbm, slot_ref[k, tile * tm + r], buf, slot, k * tm + r, sem).start()
            return c
        lax.fori_loop(0, tm, body, 0, unroll=GATHER_UNROLL // TOP_K)

    @pl.when(t == 0)
    def _():
        issue(0, 0)

    @pl.when(t + 1 < n_t)
    def _():
        issue(t + 1, (t + 1) % 2)

    slot = t % 2
    _wait_slot(y_hbm, buf, slot, sem)
    route = route_ref[...]
    mix = route[:, 0:1] * buf[slot, 0:tm, :]
    for k in range(1, TOP_K):
        mix = mix + route[:, k:k + 1] * buf[slot, k * tm:(k + 1) * tm, :]
    result = h_ref[...] + _rms(mix, gpost_ref[...])
    if len(out_refs) == 1:
        out_refs[0][...] = result
    else:
        @pl.when(t < first_sample_tile)
        def _():
            out_refs[0][...] = result

        @pl.when(t >= first_sample_tile)
        def _():
            out_refs[1][...] = result


def _combine(y_sorted, token_slots, route, h, g_post, tm, split_rows=None):
    m, d = h.shape
    if split_rows is None:
        first_sample_tile = 0
        out_specs = pl.BlockSpec((tm, d), lambda t, sl: (t, 0))
        out_shape = jax.ShapeDtypeStruct((m, d), F32)
    else:
        assert split_rows % tm == 0 and 0 < split_rows < m
        first_sample_tile = split_rows // tm
        out_specs = [pl.BlockSpec((tm, d), lambda t, sl: (jnp.minimum(t, first_sample_tile - 1), 0)),
                     pl.BlockSpec((tm, d), lambda t, sl: (jnp.maximum(t - first_sample_tile, 0), 0))]
        out_shape = [jax.ShapeDtypeStruct((split_rows, d), F32), jax.ShapeDtypeStruct((m - split_rows, d), F32)]
    grid_spec = pltpu.PrefetchScalarGridSpec(
        num_scalar_prefetch=1,
        grid=(m // tm,),
        in_specs=[pl.BlockSpec(memory_space=pl.ANY),
                  pl.BlockSpec((tm, LANES), lambda t, sl: (t, 0)),
                  pl.BlockSpec((tm, d), lambda t, sl: (t, 0)),
                  pl.BlockSpec((1, d), lambda t, sl: (0, 0))],
        out_specs=out_specs,
        scratch_shapes=[pltpu.VMEM((2, TOP_K * tm, d), F32), pltpu.SemaphoreType.DMA((2,))],
    )
    return pl.pallas_call(
        functools.partial(_combine_kernel, tm=tm, first_sample_tile=first_sample_tile), grid_spec=grid_spec,
        out_shape=out_shape,
        compiler_params=_params("arbitrary"),
        name="moe_combine",
    )(token_slots, y_sorted, route, h, g_post.reshape(1, d))


def _routing_tables(route, n_experts, tm):
    m = route.shape[0]
    eid = jnp.concatenate([route[:, 2 + k].astype(I32) for k in range(TOP_K)])
    blk = LANES
    assert (TOP_K * m) % blk == 0 and TOP_K * m < 2 ** 24
    onehot = (eid[:, None] == jnp.arange(n_experts, dtype=I32)[None, :]).astype(F32).reshape(-1, blk, n_experts)
    before = jnp.tril(jnp.ones((blk, blk), F32), -1)
    within = jnp.einsum("ts,bse->bte", before, onehot)
    block_total = jnp.sum(onehot, axis=1)
    block_start = jnp.cumsum(block_total, axis=0) - block_total
    rank = jnp.sum((within + block_start[:, None, :]) * onehot, axis=-1).reshape(-1).astype(I32)
    counts = jnp.sum(block_total, axis=0).astype(I32)
    tiles_e = (counts + tm - 1) // tm
    tile_end = jnp.cumsum(tiles_e)
    start = (tile_end - tiles_e) * tm
    slot = start[eid] + rank
    n_tiles = (TOP_K * m + n_experts * (tm - 1)) // tm
    token = jnp.tile(jnp.arange(m, dtype=I32), TOP_K)
    slot_token = jnp.zeros((n_tiles * tm,), I32).at[slot].set(token)
    n_valid = tile_end[-1].astype(I32)
    tile_id = jnp.arange(n_tiles, dtype=I32)
    tile_src = jnp.minimum(tile_id, n_valid - 1)
    tile_expert = jnp.minimum(jnp.sum((tile_end[None, :] <= tile_src[:, None]).astype(I32), axis=1), n_experts - 1)
    following, cur = [], jnp.int32(-1)
    for e in reversed(range(n_experts)):
        following.append(cur)
        cur = jnp.where(tiles_e[e] > 0, jnp.int32(e), cur)
    following = jnp.stack(following[::-1])
    wraps = following < 0
    following = jnp.where(wraps, cur, following)
    tiles = (tile_expert, tile_src, n_valid.reshape(1), following[tile_expert], wraps[tile_expert].astype(I32))
    return slot_token, slot.reshape(TOP_K, m).astype(I32), tiles


def _pool_prompt_kernel(halo_ref, p_ref, w_ref, s_ref, o_ref, buf_ref, *, tiles_per_seq):
    i = pl.program_id(0) % tiles_per_seq
    t = p_ref.shape[0]
    gw = w_ref.shape[1]
    buf_ref[0:POOL_HALO, :] = jnp.where(i == 0, 0.0, halo_ref[...])
    buf_ref[POOL_HALO:POOL_HALO + t, :] = p_ref[...]
    pos = i * t + lax.broadcasted_iota(I32, (t, 1), 0)
    for g, w in enumerate(POOL_WINDOWS):
        c0 = g * gw
        cur = buf_ref[POOL_HALO:POOL_HALO + t, c0:c0 + gw]
        acc = cur
        for j in range(1, w):
            acc = acc + buf_ref[POOL_HALO - j:POOL_HALO - j + t, c0:c0 + gw]
        cnt = jnp.minimum(pos + 1, w).astype(F32)
        d = acc / cnt - cur
        y = jnp.dot(d.astype(BF16), w_ref[g], preferred_element_type=F32)
        o_ref[:, c0:c0 + gw] = (y * s_ref[:, c0:c0 + gw]).astype(o_ref.dtype)


def _pool_prompt(proj, n_rows, seq, w_pool, scale, t):
    c = scale.shape[0]
    halo_per_tile = t // POOL_HALO
    return pl.pallas_call(
        functools.partial(_pool_prompt_kernel, tiles_per_seq=seq // t),
        grid=(n_rows // t,),
        in_specs=[pl.BlockSpec((POOL_HALO, c), lambda i: (jnp.maximum(i * halo_per_tile - 1, 0), 0)),
                  pl.BlockSpec((t, c), lambda i: (i, 0)),
                  pl.BlockSpec(w_pool.shape, lambda i: (0, 0, 0)),
                  pl.BlockSpec((1, c), lambda i: (0, 0))],
        out_specs=pl.BlockSpec((t, c), lambda i: (i, 0)),
        out_shape=jax.ShapeDtypeStruct((n_rows, c), BF16),
        scratch_shapes=[pltpu.VMEM((POOL_HALO + t, c), F32)],
        compiler_params=_params("parallel"),
        name="pool_prompt",
    )(proj, proj, w_pool, scale.reshape(1, c))


def _pool_sample_kernel(ctx_ref, w_ref, s_ref, o_ref, *, n_ctx, pos0):
    gw = w_ref.shape[1]
    for t in range(o_ref.shape[0]):
        hi = n_ctx + t + 1
        for g, w in enumerate(POOL_WINDOWS):
            c0 = g * gw
            lo = max(hi - w, 0)
            acc = ctx_ref[lo, :, c0:c0 + gw]
            for r in range(lo + 1, hi):
                acc = acc + ctx_ref[r, :, c0:c0 + gw]
            d = acc / float(min(pos0 + t + 1, w)) - ctx_ref[hi - 1, :, c0:c0 + gw]
            y = jnp.dot(d.astype(BF16), w_ref[g], preferred_element_type=F32)
            o_ref[t, :, c0:c0 + gw] = (y * s_ref[:, c0:c0 + gw]).astype(o_ref.dtype)


def _pool_sample(ctx_t, n_ctx, pos0, w_pool, scale):
    total, n, c = ctx_t.shape
    return pl.pallas_call(
        functools.partial(_pool_sample_kernel, n_ctx=n_ctx, pos0=pos0),
        out_shape=jax.ShapeDtypeStruct((total - n_ctx, n, c), BF16),
        compiler_params=pltpu.CompilerParams(vmem_limit_bytes=VMEM_LIMIT_BYTES),
        name="pool_sample",
    )(ctx_t, w_pool, scale.reshape(1, c))


def _gmlp_kernel(u_ref, v_ref, w_ref, b_ref, o_ref):
    t = u_ref.shape[0]
    n_heads = w_ref.shape[0]
    ck = w_ref.shape[1]
    row = lax.broadcasted_iota(I32, (ck, ck), 0)
    col = lax.broadcasted_iota(I32, (ck, ck), 1)
    for h in range(n_heads):
        c0 = h * HEAD_DIM
        wm = jnp.where(col <= row, w_ref[h], 0.0).astype(BF16)
        bias = b_ref[:, h:h + 1]
        for k in range(t // ck):
            r0 = k * ck
            v = v_ref[r0:r0 + ck, c0:c0 + HEAD_DIM].astype(BF16)
            mixed = jnp.dot(wm, v, preferred_element_type=F32) + bias
            o_ref[r0:r0 + ck, c0:c0 + HEAD_DIM] = (u_ref[r0:r0 + ck, c0:c0 + HEAD_DIM] * mixed).astype(o_ref.dtype)


def _gmlp(proj, n_prompt_rows, w_pair, b_pair, t):
    m = proj.shape[0]
    c = proj.shape[1] // 3
    n_heads = w_pair.shape[1]
    first_sample_tile = n_prompt_rows // t
    which = lambda i: jnp.where(i >= first_sample_tile, 1, 0)
    return pl.pallas_call(
        _gmlp_kernel,
        grid=(m // t,),
        in_specs=[pl.BlockSpec((t, c), lambda i: (i, 1)),
                  pl.BlockSpec((t, c), lambda i: (i, 2)),
                  pl.BlockSpec((None, n_heads, GMLP_CHUNK, GMLP_CHUNK), lambda i: (which(i), 0, 0, 0)),
                  pl.BlockSpec((None, GMLP_CHUNK, n_heads), lambda i: (which(i), 0, 0))],
        out_specs=pl.BlockSpec((t, c), lambda i: (i, 0)),
        out_shape=jax.ShapeDtypeStruct((m, c), BF16),
        compiler_params=_params("parallel"),
        name="gmlp",
    )(proj, proj, w_pair, b_pair)


def _conv_prompt_kernel(xc_ref, bg_ref, cg_ref, xch_ref, cgh_ref, w_ref, o_ref, tail_ref, buf_ref, *, tiles_per_seq):
    i = pl.program_id(0) % tiles_per_seq
    t = xc_ref.shape[0]
    z = cg_ref[...] * xc_ref[...]
    buf_ref[0:CONV_HALO, :] = jnp.where(i == 0, 0.0, cgh_ref[...] * xch_ref[...])
    buf_ref[CONV_HALO:CONV_HALO + t, :] = z
    y = w_ref[0:1, :] * buf_ref[CONV_HALO - 2:CONV_HALO - 2 + t, :]
    for j in range(1, CONV_WIDTH):
        y = y + w_ref[j:j + 1, :] * buf_ref[CONV_HALO - 2 + j:CONV_HALO - 2 + j + t, :]
    o_ref[...] = (bg_ref[...] * y).astype(o_ref.dtype)
    tail_ref[...] = z[t - CONV_HALO:, :]


def _conv_prompt(proj, n_rows, seq, conv_w, t):
    c = conv_w.shape[1]
    halo_per_tile = t // CONV_HALO
    halo_row = lambda i: jnp.maximum(i * halo_per_tile - 1, 0)
    return pl.pallas_call(
        functools.partial(_conv_prompt_kernel, tiles_per_seq=seq // t),
        grid=(n_rows // t,),
        in_specs=[pl.BlockSpec((t, c), lambda i: (i, 0)),
                  pl.BlockSpec((t, c), lambda i: (i, 1)),
                  pl.BlockSpec((t, c), lambda i: (i, 2)),
                  pl.BlockSpec((CONV_HALO, c), lambda i: (halo_row(i), 0)),
                  pl.BlockSpec((CONV_HALO, c), lambda i: (halo_row(i), 2)),
                  pl.BlockSpec((CONV_WIDTH, c), lambda i: (0, 0))],
        out_specs=[pl.BlockSpec((t, c), lambda i: (i, 0)),
                   pl.BlockSpec((CONV_HALO, c), lambda i: (i, 0))],
        out_shape=[jax.ShapeDtypeStruct((n_rows, c), BF16),
                   jax.ShapeDtypeStruct((n_rows // t * CONV_HALO, c), F32)],
        scratch_shapes=[pltpu.VMEM((CONV_HALO + t, c), F32)],
        compiler_params=_params("parallel"),
        name="conv_prompt",
    )(proj, proj, proj, proj, proj, conv_w)


def _conv_sample_kernel(ctx_ref, xc_ref, bg_ref, cg_ref, w_ref, o_ref, z_ref):
    n_ctx = ctx_ref.shape[0]
    length = xc_ref.shape[0]
    rows = [ctx_ref[r] for r in range(n_ctx)]
    for t in range(length):
        z = cg_ref[t] * xc_ref[t]
        z_ref[t] = z
        rows.append(z)
    for t in range(length):
        y = w_ref[0:1, :] * rows[t]
        for j in range(1, CONV_WIDTH):
            y = y + w_ref[j:j + 1, :] * rows[t + j]
        o_ref[t] = (bg_ref[t] * y).astype(o_ref.dtype)


def _conv_sample(ctx_t, xc_t, bg_t, cg_t, conv_w):
    length, n, c = xc_t.shape
    return pl.pallas_call(
        _conv_sample_kernel,
        out_shape=[jax.ShapeDtypeStruct((length, n, c), BF16), jax.ShapeDtypeStruct((length, n, c), F32)],
        compiler_params=pltpu.CompilerParams(vmem_limit_bytes=VMEM_LIMIT_BYTES),
        name="conv_sample",
    )(ctx_t, xc_t, bg_t, cg_t, conv_w)


def _log_sigmoid(x):
    return jnp.minimum(x, 0.0) - jnp.log1p(jnp.exp(-jnp.abs(x)))


def _logf_kernel(f_ref, b_ref, lf_ref, c_ref, carry_ref, *, blocks_per_seq):
    i = pl.program_id(0)
    blk = f_ref.shape[0]
    lf = _log_sigmoid(f_ref[...] + b_ref[...])
    lf_ref[...] = lf

    @pl.when(i % blocks_per_seq == 0)
    def _():
        carry_ref[...] = jnp.zeros_like(carry_ref)

    row = lax.broadcasted_iota(I32, (blk, blk), 0)
    col = lax.broadcasted_iota(I32, (blk, blk), 1)
    tri = jnp.where(col <= row, 1.0, 0.0).astype(F32)
    c = jnp.dot(tri, lf, precision=HIGHEST, preferred_element_type=F32) + carry_ref[0:1, :]
    c_ref[...] = c
    carry_ref[...] = jnp.broadcast_to(c[blk - 1:blk, :], carry_ref.shape)


def _logf_cumsum(f_logit, b_forget, seq, blk):
    m = f_logit.shape[0]
    return pl.pallas_call(
        functools.partial(_logf_kernel, blocks_per_seq=seq // blk),
        grid=(m // blk,),
        in_specs=[pl.BlockSpec((blk, LANES), lambda i: (i, 0)), pl.BlockSpec((1, LANES), lambda i: (0, 0))],
        out_specs=[pl.BlockSpec((blk, LANES), lambda i: (i, 0)), pl.BlockSpec((blk, LANES), lambda i: (i, 0))],
        out_shape=[jax.ShapeDtypeStruct((m, LANES), F32), jax.ShapeDtypeStruct((m, LANES), F32)],
        scratch_shapes=[pltpu.VMEM((8, LANES), F32)],
        compiler_params=_params("arbitrary"),
        name="logf_cumsum",
    )(f_logit, b_forget)


FLASH_HEADS = 2


def _flash_kernel(q_ref, k_ref, v_ref, cc_ref, cr_ref, o_ref, *, tq):
    hp = pl.program_id(1)
    qb = pl.program_id(2)
    lane = lax.broadcasted_iota(I32, (tq, LANES), 1)
    heads = []
    for hh in range(FLASH_HEADS):
        cols = slice(hh * HEAD_DIM, (hh + 1) * HEAD_DIM)
        q = (q_ref[:, cols] * (1.0 / math.sqrt(HEAD_DIM))).astype(BF16)
        cq = jnp.sum(jnp.where(lane == hp * FLASH_HEADS + hh, cc_ref[...], 0.0), axis=-1, keepdims=True)
        heads.append((cols, q, cq))

    def scores(hh, j):
        cols, q, cq = heads[hh]
        start = pl.multiple_of(j * tq, tq)
        k = k_ref[pl.ds(start, tq), cols].astype(BF16)
        v = v_ref[pl.ds(start, tq), cols].astype(BF16)
        s = lax.dot_general(q, k, (((1,), (1,)), ((), ())), preferred_element_type=F32)
        ck = cr_ref[hh, :, pl.ds(start, tq)]
        return s + (cq - ck), v

    def update(carry, s, v):
        m, l, acc = carry
        m_new = jnp.maximum(m, jnp.max(s, axis=-1, keepdims=True))
        alpha = jnp.exp(m - m_new)
        p = jnp.exp(s - m_new)
        l = alpha * l + jnp.sum(p, axis=-1, keepdims=True)
        acc = alpha * acc + jnp.dot(p.astype(BF16), v, preferred_element_type=F32)
        return m_new, l, acc

    def body(j, carries):
        return tuple(update(carries[hh], *scores(hh, j)) for hh in range(FLASH_HEADS))

    init = (jnp.full((tq, 1), MASKED, F32), jnp.zeros((tq, 1), F32), jnp.zeros((tq, HEAD_DIM), F32))
    carries = lax.fori_loop(0, qb, body, (init,) * FLASH_HEADS)
    row = lax.broadcasted_iota(I32, (tq, tq), 0)
    col = lax.broadcasted_iota(I32, (tq, tq), 1)
    for hh in range(FLASH_HEADS):
        s, v = scores(hh, qb)
        _, l, acc = update(carries[hh], jnp.where(col <= row, s, MASKED), v)
        o_ref[:, heads[hh][0]] = (acc / l).astype(o_ref.dtype)


def _flash_prompt(proj, c_col, c_row, n_seq, seq, n_heads, tq):
    assert n_heads % FLASH_HEADS == 0
    pairs = n_heads // FLASH_HEADS
    width = FLASH_HEADS * HEAD_DIM
    q_blk, k_blk, v_blk = 3 * pairs, 4 * pairs, 5 * pairs
    nqb = seq // tq
    return pl.pallas_call(
        functools.partial(_flash_kernel, tq=tq),
        grid=(n_seq, pairs, nqb),
        in_specs=[pl.BlockSpec((tq, width), lambda n, h, i: (n * nqb + i, q_blk + h)),
                  pl.BlockSpec((seq, width), lambda n, h, i: (n, k_blk + h)),
                  pl.BlockSpec((seq, width), lambda n, h, i: (n, v_blk + h)),
                  pl.BlockSpec((tq, LANES), lambda n, h, i: (n * nqb + i, 0)),
                  pl.BlockSpec((FLASH_HEADS, 1, seq), lambda n, h, i: (h, 0, n))],
        out_specs=pl.BlockSpec((tq, width), lambda n, h, i: (n * nqb + i, h)),
        out_shape=jax.ShapeDtypeStruct((n_seq * seq, n_heads * HEAD_DIM), BF16),
        compiler_params=_params("parallel", "parallel", "arbitrary"),
        name="fox_prompt",
    )(proj, proj, proj, c_col, c_row)


def _decay_of_later_rows(lf, later_groups, n_heads):
    group, width = lf.shape
    lane = lax.broadcasted_iota(I32, lf.shape, 1)
    incl, total = lf, lf
    shift = n_heads
    while shift < width:
        incl = incl + jnp.where(lane + shift < width, pltpu.roll(incl, width - shift, 1), 0.0)
        total = total + pltpu.roll(total, shift, 1)
        shift *= 2
    later_pages = jnp.zeros_like(lf)
    if group > 1:
        prow = lax.broadcasted_iota(I32, (group, group), 0)
        pcol = lax.broadcasted_iota(I32, (group, group), 1)
        later = jnp.where(pcol > prow, 1.0, 0.0).astype(F32)
        later_pages = jnp.dot(later, total, precision=HIGHEST, preferred_element_type=F32)
    return (incl - lf) + later_pages + later_groups, jnp.sum(total, axis=0, keepdims=True)


def _decode_kernel(pt_ref, q_ref, kn_ref, vn_ref, lfn_ref, *refs, n_heads, n_q, group, n_par):
    pages = 3 * group * n_par
    o_ref, m_ref, l_ref, acc_ref, later_ref = refs[pages:]
    j = pl.program_id(1)
    rows = n_heads * n_q
    width = refs[0].shape[0] * n_heads

    @pl.when(j == 0)
    def _():
        m_ref[...] = jnp.full_like(m_ref, MASKED)
        l_ref[...] = jnp.zeros_like(l_ref)
        acc_ref[...] = jnp.zeros_like(acc_ref)
        later_ref[...] = jnp.zeros_like(later_ref)

    r_i = lax.broadcasted_iota(I32, (rows, LANES), 0)
    c_i = lax.broadcasted_iota(I32, (rows, LANES), 1)
    sel = (c_i % n_heads == r_i // n_q) & (c_i // n_heads <= r_i % n_q) & (c_i < n_q * n_heads)
    r_w = lax.broadcasted_iota(I32, (rows, group * width), 0)
    c_w = lax.broadcasted_iota(I32, (rows, group * width), 1)
    same_head = c_w % n_heads == r_w // n_q

    def update(state, s, v):
        m_old, l, acc = state
        m_new = jnp.maximum(m_old, jnp.max(s, axis=-1, keepdims=True))
        alpha = jnp.exp(m_old - m_new)
        p = jnp.exp(s - m_new)
        l = alpha * l + jnp.sum(p, axis=-1, keepdims=True)
        acc = alpha * acc + jnp.dot(p.astype(BF16), v, preferred_element_type=F32)
        return m_new, l, acc

    def load_state(b):
        return m_ref[b, :, 0:1], l_ref[b, :, 0:1], acc_ref[b]

    def store_state(b, state):
        m, l, acc = state
        m_ref[b] = jnp.broadcast_to(m, m_ref.shape[1:])
        l_ref[b] = jnp.broadcast_to(l, l_ref.shape[1:])
        acc_ref[b] = acc

    states = [load_state(b) for b in range(n_par)]
    later = [later_ref[b] for b in range(n_par)]
    queries = []
    for b in range(n_par):
        base = 3 * group * b
        k_refs, v_refs = refs[base:base + group], refs[base + group:base + 2 * group]
        lf_refs = refs[base + 2 * group:base + 3 * group]
        lf = jnp.concatenate([r[...] for r in lf_refs], axis=0)
        decay, group_total = _decay_of_later_rows(lf, later[b][0:1, :], n_heads)
        later[b] = later[b] + group_total

        q = (q_ref[b] * (1.0 / math.sqrt(HEAD_DIM))).astype(BF16)
        lfn = lfn_ref[b]
        cq = jnp.sum(jnp.where(sel, jnp.broadcast_to(lfn, (rows, LANES)), 0.0), axis=-1, keepdims=True)
        queries.append((q, lfn, cq))

        k = jnp.concatenate([r[...].reshape(width, HEAD_DIM).astype(BF16) for r in k_refs], axis=0)
        v = jnp.concatenate([r[...].reshape(width, HEAD_DIM).astype(BF16) for r in v_refs], axis=0)
        decay_row = jnp.concatenate([decay[p:p + 1, :] for p in range(group)], axis=1)
        s = lax.dot_general(q, k, (((1,), (1,)), ((), ())), preferred_element_type=F32)
        s = jnp.where(same_head, s + decay_row + cq, MASKED)
        states[b] = update(states[b], s, v)
    for b in range(n_par):
        store_state(b, states[b])
        later_ref[b] = later[b]

    @pl.when(j == pl.num_programs(1) - 1)
    def _():
        for b in range(n_par):
            q, lfn, cq = queries[b]
            kn = kn_ref[b].astype(BF16)
            vn = vn_ref[b].astype(BF16)
            sn = lax.dot_general(q, kn, (((1,), (1,)), ((), ())), preferred_element_type=F32)
            lf8 = jnp.broadcast_to(lfn, (8, LANES))
            lane8 = lax.broadcasted_iota(I32, (8, LANES), 1)
            csum = lf8
            shift = n_heads
            while shift < n_q * n_heads:
                csum = csum + jnp.where(lane8 >= shift, pltpu.roll(csum, shift, 1), 0.0)
                shift *= 2
            _, l, acc = update(states[b], jnp.where(sel, sn + (cq - csum[0:1, :]), MASKED), vn)
            o_ref[b] = acc / l


def _decode_attention(q_hq, cache_k, cache_v, logf_pages, layer, page_table, k_new, v_new, lf_new, n_heads, n_q):
    n, n_pages = page_table.shape
    assert n_pages >= 1
    page = cache_k.shape[2]
    rows = n_heads * n_q
    width = page * n_heads
    group = max(g for g in (1, 2, 4, 8) if n_pages % g == 0)
    n_groups = n_pages // group
    n_par = 2 if n % 2 == 0 else 1

    def physical_page(i, j, pt, b, p):
        return pt[(i * n_par + b) * n_pages + (n_groups - 1 - j) * group + p]

    def page_spec(b, p):
        return pl.BlockSpec((None, None, page, n_heads, HEAD_DIM),
                            lambda i, j, pt: (layer, physical_page(i, j, pt, b, p), 0, 0, 0))

    def logf_spec(b, p):
        return pl.BlockSpec((None, 1, width), lambda i, j, pt: (physical_page(i, j, pt, b, p), 0, 0))

    per_seq = lambda shape: pl.BlockSpec((n_par,) + shape, lambda i, j, pt: (i, 0, 0))
    page_specs, page_args = [], []
    for b in range(n_par):
        page_specs += [page_spec(b, p) for p in range(group)] * 2 + [logf_spec(b, p) for p in range(group)]
        page_args += [cache_k] * group + [cache_v] * group + [logf_pages] * group
    grid_spec = pltpu.PrefetchScalarGridSpec(
        num_scalar_prefetch=1,
        grid=(n // n_par, n_groups),
        in_specs=[per_seq((rows, HEAD_DIM)), per_seq((LANES, HEAD_DIM)), per_seq((LANES, HEAD_DIM)),
                  per_seq((1, LANES))] + page_specs,
        out_specs=per_seq((rows, HEAD_DIM)),
        scratch_shapes=[pltpu.VMEM((n_par, rows, LANES), F32), pltpu.VMEM((n_par, rows, LANES), F32),
                        pltpu.VMEM((n_par, rows, HEAD_DIM), F32), pltpu.VMEM((n_par, 8, width), F32)],
    )
    return pl.pallas_call(
        functools.partial(_decode_kernel, n_heads=n_heads, n_q=n_q, group=group, n_par=n_par),
        grid_spec=grid_spec,
        out_shape=jax.ShapeDtypeStruct((n, rows, HEAD_DIM), F32),
        compiler_params=_params("parallel", "arbitrary"),
        name="fox_decode",
    )(page_table.reshape(-1), q_hq, k_new, v_new, lf_new, *page_args)


def kernel(x_prompt, x_sample, state_pool, state_conv, cache_k, cache_v, cache_logf, page_table, e_norm_mix_pre, e_norm_mix_post, e_norm_ffn_pre, e_norm_ffn_post, e_w_in, e_w_pool, e_pool_scale, e_w_spatial, e_b_spatial, e_w_out, e_ffn_gate, e_ffn_up, e_ffn_down, o_norm_mix_pre, o_norm_mix_post, o_norm_ffn_pre, o_norm_ffn_post, o_w_in, o_conv_w, o_b_forget, o_w_out, o_w_router, o_exp_gate, o_exp_up, o_exp_down):
    n_p, seq, d = x_prompt.shape
    n_s, dec, _ = x_sample.shape
    mp, ms = n_p * seq, n_s * dec
    m = mp + ms
    c = d // 2
    n_heads = c // HEAD_DIM
    n_pages = page_table.shape[1]
    page = cache_k.shape[2]
    past_len = n_pages * page
    n_layers = e_w_in.shape[0] + o_w_in.shape[0]
    n_experts = o_w_router.shape[2]
    assert ms % GMLP_CHUNK == 0 and seq % GMLP_CHUNK == 0 and past_len % GMLP_CHUNK == 0
    assert GMLP_CHUNK % min(dec, GMLP_CHUNK) == 0 and dec % min(dec, GMLP_CHUNK) == 0
    assert dec * n_heads <= LANES and dec >= CONV_WIDTH - 1

    tm = _tile(math.gcd(mp, ms), 512, LANES)
    t_seq = _tile(math.gcd(seq, tm), 512, LANES)
    tq = _tile(seq, 512, LANES)
    tm_moe = 256
    dense_tiles = _dense_tiles(m // tm)

    h = (x_prompt.reshape(mp, d), x_sample.reshape(ms, d))
    xn = _norm_cast(h, e_norm_mix_pre[0], tm)
    out_rows = None

    pool_p, pool_s, gv_s, conv_p, conv_s = [], [], [], [], []
    kp_l, vp_l, lp_l, ks_l, vs_l, ls_l = [], [], [], [], [], []

    def time_major(x2d):
        return jnp.transpose(x2d.reshape(n_s, dec, -1), (1, 0, 2))

    def row_major(x3d):
        return jnp.transpose(x3d, (1, 0, 2)).reshape(ms, -1)

    for layer in range(n_layers):
        i = layer // 2
        if layer % 2 == 0:
            g_next = e_norm_ffn_pre[i]
            proj = _mm(xn, e_w_in[i].astype(BF16), tm, _tile(3 * c, 1024, LANES))
            p_s = proj[mp:, :c].reshape(n_s, dec, c)
            ctx = jnp.concatenate([state_pool[i].astype(F32), p_s], axis=1)
            n_ctx = state_pool.shape[2]
            w_pool = e_w_pool[i].astype(BF16)
            a_p = _pool_prompt(proj, mp, seq, w_pool, e_pool_scale[i], t_seq)
            a_s = row_major(_pool_sample(jnp.transpose(ctx, (1, 0, 2)), n_ctx, past_len, w_pool, e_pool_scale[i]))
            cl = min(dec, GMLP_CHUNK)
            reps = GMLP_CHUNK // cl
            w_samp = jnp.einsum("ab,hts->hatbs", jnp.eye(reps, dtype=F32),
                                e_w_spatial[i][:, :cl, :cl]).reshape(n_heads, GMLP_CHUNK, GMLP_CHUNK)
            b_samp = jnp.tile(e_b_spatial[i][:, :cl], (1, reps))
            w_pair = jnp.stack([e_w_spatial[i], w_samp])
            b_pair = jnp.stack([e_b_spatial[i].T, b_samp.T])
            b = _gmlp(proj, mp, w_pair, b_pair, tm)
            h, xn = _proj_out((a_p, a_s), (b,), e_w_out[i].astype(BF16), h, e_norm_mix_post[i], g_next, tm, mp)
            g_after = o_norm_mix_pre[i] if layer + 1 < n_layers else jnp.ones((d,), F32)
            mix = _swiglu_grouped(xn, e_ffn_gate[i][None], e_ffn_up[i][None], e_ffn_down[i][None], dense_tiles, tm)
            h, xn = _norm_residual(mix, h, e_norm_ffn_post[i], g_after, tm)
            h = (h,)
            pool_p.append(proj[:mp, :c].reshape(n_p, seq, c)[:, seq - n_ctx:])
            pool_s.append(ctx[:, ctx.shape[1] - n_ctx:])
            gv_s.append(proj[mp:, 2 * c:].reshape(n_s, dec, c))
        else:
            w_in = o_w_in[i]
            proj, k_p, v_p = _mm(xn, w_in[:, :6 * c].astype(BF16), tm, c, copy_cols=(4, 5), copy_rows=mp)
            w_f = jnp.zeros((d, LANES), BF16).at[:, :n_heads].set(w_in[:, 6 * c:].astype(BF16))
            f_logit = _mm(xn, w_f, tm, LANES)
            b_f = jnp.zeros((1, LANES), F32).at[0, :n_heads].set(o_b_forget[i])
            logf, csum = _logf_cumsum(f_logit, b_f, seq, LANES)
            c_p, z_tail = _conv_prompt(proj, mp, seq, o_conv_w[i], t_seq)
            z_tail = z_tail.reshape(n_p, seq // t_seq, CONV_HALO, c)[:, -1, CONV_HALO - (CONV_WIDTH - 1):]
            xs = proj[mp:]
            conv_ctx = jnp.transpose(state_conv[i].astype(F32), (1, 0, 2))
            c_s, z_s = _conv_sample(conv_ctx, time_major(xs[:, :c]), time_major(xs[:, c:2 * c]),
                                    time_major(xs[:, 2 * c:3 * c]), o_conv_w[i])
            z_all = jnp.concatenate([conv_ctx, z_s], axis=0)
            c_row = csum[:mp, :n_heads].T.reshape(n_heads, 1, mp)
            att_p = _flash_prompt(proj, csum, c_row, n_p, seq, n_heads, tq)
            k_s = xs[:, 4 * c:5 * c]
            v_s = xs[:, 5 * c:6 * c]
            q_hq = jnp.transpose(xs[:, 3 * c:4 * c].reshape(n_s, dec, n_heads, HEAD_DIM), (0, 2, 1, 3))
            q_hq = q_hq.reshape(n_s, n_heads * dec, HEAD_DIM)
            pad_rows = LANES - dec * n_heads
            k_new = jnp.pad(k_s.reshape(n_s, dec * n_heads, HEAD_DIM), ((0, 0), (0, pad_rows), (0, 0)))
            v_new = jnp.pad(v_s.reshape(n_s, dec * n_heads, HEAD_DIM), ((0, 0), (0, pad_rows), (0, 0)))
            lf_new = jnp.pad(logf[mp:, :n_heads].reshape(n_s, 1, dec * n_heads), ((0, 0), (0, 0), (0, pad_rows)))
            logf_pages = cache_logf[i].astype(F32).reshape(cache_logf.shape[1], 1, page * n_heads)
            att_s = _decode_attention(q_hq, cache_k, cache_v, logf_pages, i, page_table, k_new, v_new, lf_new,
                                      n_heads, dec)
            att_s = jnp.transpose(att_s.reshape(n_s, n_heads, dec, HEAD_DIM), (0, 2, 1, 3)).reshape(ms, c)
            h, _, route = _proj_out((c_p, row_major(c_s)), (att_p, att_s.astype(BF16)), o_w_out[i].astype(BF16), h,
                                    o_norm_mix_post[i], o_norm_ffn_pre[i], tm, mp, w_router=o_w_router[i])
            slot_token, token_slots, expert_tiles = _routing_tables(route, n_experts, tm_moe)
            x_sorted = _dispatch(h, o_norm_ffn_pre[i], slot_token, tm_moe)
            y_sorted = _swiglu_grouped(x_sorted, o_exp_gate[i], o_exp_up[i], o_exp_down[i], expert_tiles, tm_moe)
            t_comb = _tile(math.gcd(mp, ms), tm_moe)
            if layer + 1 < n_layers:
                h = (_combine(y_sorted, token_slots, route, h, o_norm_ffn_post[i], t_comb),)
                xn = _norm_cast(h, e_norm_mix_pre[i + 1], tm)
            else:
                out_rows = _combine(y_sorted, token_slots, route, h, o_norm_ffn_post[i], t_comb, split_rows=mp)
            conv_p.append(z_tail)
            conv_s.append(jnp.transpose(z_all[z_all.shape[0] - (CONV_WIDTH - 1):], (1, 0, 2)))
            kp_l.append(k_p.reshape(n_p, seq, n_heads, HEAD_DIM))
            vp_l.append(v_p.reshape(n_p, seq, n_heads, HEAD_DIM))
            lp_l.append(logf[:mp, :n_heads].reshape(n_p, seq, n_heads))
            ks_l.append(k_s.reshape(n_s, dec, n_heads, HEAD_DIM))
            vs_l.append(v_s.reshape(n_s, dec, n_heads, HEAD_DIM))
            ls_l.append(logf[mp:, :n_heads].reshape(n_s, dec, n_heads))

    if out_rows is None:
        out_rows = (h[0][:mp], h[0][mp:])
    return (out_rows[0].reshape(n_p, seq, d), out_rows[1].reshape(n_s, dec, d),
            jnp.stack(pool_p), jnp.stack(pool_s), jnp.stack(gv_s), jnp.stack(conv_p), jnp.stack(conv_s),
            jnp.stack(kp_l), jnp.stack(vp_l), jnp.stack(lp_l), jnp.stack(ks_l), jnp.stack(vs_l), jnp.stack(ls_l))
```

```python
import functools
import math

import jax
import jax.numpy as jnp
from jax import lax
from jax.experimental import pallas as pl
from jax.experimental.pallas import tpu as pltpu

F32, BF16, I32 = jnp.float32, jnp.bfloat16, jnp.int32
RMS_EPS = 1e-6
POOL_WINDOWS = (2, 4, 8, 16)
POOL_HALO = 16
CONV_WIDTH = 3
CONV_HALO = 8
HEAD_DIM = 128
GMLP_CHUNK = 128
TOP_K = 2
LANES = 128
MASKED = -1e30
VMEM_LIMIT_BYTES = 56 * 1024 * 1024
HIGHEST = lax.Precision.HIGHEST


def _params(*semantics):
    return pltpu.CompilerParams(dimension_semantics=semantics, vmem_limit_bytes=VMEM_LIMIT_BYTES)


def _tile(n, pref, mult=8):
    t = min(n, pref)
    t -= t % mult
    while t > mult and n % t:
        t -= mult
    assert t > 0 and n % t == 0, (n, pref, mult)
    return t


def _rms(x, g):
    return x * lax.rsqrt(jnp.mean(x * x, axis=-1, keepdims=True) + RMS_EPS) * g


def _row_specs(parts, tm):
    if len(parts) == 1:
        return [pl.BlockSpec((tm, parts[0].shape[1]), lambda i, *_: (i, 0))]
    first_sample_tile = parts[0].shape[0] // tm
    return [pl.BlockSpec((tm, parts[0].shape[1]), lambda i, *_: (jnp.minimum(i, first_sample_tile - 1), 0)),
            pl.BlockSpec((tm, parts[1].shape[1]), lambda i, *_: (jnp.maximum(i - first_sample_tile, 0), 0))]


def _read_rows(refs, first_sample_tile):
    if len(refs) == 1:
        return refs[0][...]
    return jnp.where(pl.program_id(0) < first_sample_tile, refs[0][...], refs[1][...])


def _norm_cast_kernel(*refs, first_sample_tile):
    g_ref, o_ref = refs[-2:]
    o_ref[...] = _rms(_read_rows(refs[:-2], first_sample_tile), g_ref[...]).astype(o_ref.dtype)


def _norm_cast(x_parts, g, tm):
    m = sum(p.shape[0] for p in x_parts)
    d = x_parts[0].shape[1]
    return pl.pallas_call(
        functools.partial(_norm_cast_kernel, first_sample_tile=x_parts[0].shape[0] // tm),
        grid=(m // tm,),
        in_specs=_row_specs(x_parts, tm) + [pl.BlockSpec((1, d), lambda i: (0, 0))],
        out_specs=pl.BlockSpec((tm, d), lambda i: (i, 0)),
        out_shape=jax.ShapeDtypeStruct((m, d), BF16),
        compiler_params=_params("parallel"),
        name="norm_cast",
    )(*x_parts, g.reshape(1, d))


def _mm_kernel(x_ref, w_ref, o_ref, *rest, copy_cols, copy_tiles):
    copy_refs, w_bf = rest[:-1], rest[-1]

    @pl.when(pl.program_id(1) == 0)
    def _():
        w_bf[...] = w_ref[...].astype(BF16)

    y = jnp.dot(x_ref[...], w_bf[...], preferred_element_type=F32)
    o_ref[...] = y
    for ref, col in zip(copy_refs, copy_cols):
        @pl.when(jnp.logical_and(pl.program_id(0) == col, pl.program_id(1) < copy_tiles))
        def _(ref=ref):
            ref[...] = y


def _mm(x, w, n_cols, tm, tn, copy_cols=(), copy_rows=0):
    m, k = x.shape
    n = n_cols
    copy_tiles = copy_rows // tm
    assert copy_rows % tm == 0 and n % tn == 0 and n <= w.shape[1]

    def copy_spec(col):
        def index(j, i):
            row = jnp.where(j < col, 0, jnp.where(j > col, copy_tiles - 1, jnp.minimum(i, copy_tiles - 1)))
            return (row, 0)
        return pl.BlockSpec((tm, tn), index)

    out = pl.pallas_call(
        functools.partial(_mm_kernel, copy_cols=tuple(copy_cols), copy_tiles=copy_tiles),
        grid=(n // tn, m // tm),
        in_specs=[pl.BlockSpec((tm, k), lambda j, i: (i, 0)), pl.BlockSpec((k, tn), lambda j, i: (0, j))],
        out_specs=[pl.BlockSpec((tm, tn), lambda j, i: (i, j))] + [copy_spec(col) for col in copy_cols],
        out_shape=[jax.ShapeDtypeStruct((m, n), F32)] + [jax.ShapeDtypeStruct((copy_rows, tn), F32)] * len(copy_cols),
        scratch_shapes=[pltpu.VMEM((k, tn), BF16)],
        compiler_params=_params("arbitrary", "arbitrary"),
        name="proj_in",
    )(x, w)
    return out if copy_cols else out[0]


def _top2(logits, n_experts):
    lane = lax.broadcasted_iota(I32, logits.shape, 1)
    neg_inf = jnp.float32(-jnp.inf)
    l1 = jnp.where(lane < n_experts, logits, neg_inf)
    m1 = jnp.max(l1, axis=-1, keepdims=True)
    i1 = jnp.min(jnp.where(l1 == m1, lane, LANES), axis=-1, keepdims=True)
    l2 = jnp.where(lane == i1, neg_inf, l1)
    m2 = jnp.max(l2, axis=-1, keepdims=True)
    i2 = jnp.min(jnp.where(l2 == m2, lane, LANES), axis=-1, keepdims=True)
    e2 = jnp.exp(m2 - m1)
    g1 = 1.0 / (1.0 + e2)
    g2 = e2 / (1.0 + e2)
    return jnp.where(lane == 0, g1,
                     jnp.where(lane == 1, g2,
                               jnp.where(lane == 2, i1.astype(F32),
                                         jnp.where(lane == 3, i2.astype(F32), 0.0))))


def _proj_out_kernel(*refs, n_experts, layout, first_sample_tile):
    na, nb, nh = layout
    a = _read_rows(refs[:na], first_sample_tile)
    b = _read_rows(refs[na:na + nb], first_sample_tile)
    h = _read_rows(refs[na + nb:na + nb + nh], first_sample_tile)
    w_ref, gpost_ref, gnext_ref, *rest = refs[na + nb + nh:]
    half = a.shape[1]
    m = (jnp.dot(a, w_ref[:half, :], preferred_element_type=F32)
         + jnp.dot(b, w_ref[half:, :], preferred_element_type=F32))
    hn = h + _rms(m, gpost_ref[...])
    xn = _rms(hn, gnext_ref[...])
    xn_hi = xn.astype(BF16)
    if n_experts:
        wr_hi_ref, wr_lo_ref, hn_ref, xn_ref, route_ref = rest
        xn_lo = (xn - xn_hi.astype(F32)).astype(BF16)
        logits = (jnp.dot(xn_hi, wr_hi_ref[...], preferred_element_type=F32)
                  + jnp.dot(xn_lo, wr_hi_ref[...], preferred_element_type=F32)
                  + jnp.dot(xn_hi, wr_lo_ref[...], preferred_element_type=F32))
        route_ref[...] = _top2(logits, n_experts)
    else:
        hn_ref, xn_ref = rest
    hn_ref[...] = hn
    xn_ref[...] = xn_hi


def _proj_out(a_parts, b_parts, w, h_parts, g_post, g_next, tm, n_prompt_rows, w_router=None):
    m = sum(p.shape[0] for p in h_parts)
    d = h_parts[0].shape[1]
    half = a_parts[0].shape[1]
    n_experts = 0 if w_router is None else w_router.shape[1]
    first_sample_tile = n_prompt_rows // tm
    assert all(len(p) == 1 or p[0].shape[0] == n_prompt_rows for p in (a_parts, b_parts, h_parts))
    row = lambda i: (i, 0)
    fixed = lambda i: (0, 0)
    in_specs = (_row_specs(a_parts, tm) + _row_specs(b_parts, tm) + _row_specs(h_parts, tm)
                + [pl.BlockSpec((2 * half, d), fixed), pl.BlockSpec((1, d), fixed), pl.BlockSpec((1, d), fixed)])
    out_specs = [pl.BlockSpec((tm, d), row), pl.BlockSpec((tm, d), row)]
    out_shape = [jax.ShapeDtypeStruct((m, d), F32), jax.ShapeDtypeStruct((m, d), BF16)]
    args = [*a_parts, *b_parts, *h_parts, w, g_post.reshape(1, d), g_next.reshape(1, d)]
    if n_experts:
        wr = jnp.zeros((d, LANES), F32).at[:, :n_experts].set(w_router.astype(F32))
        wr_hi = wr.astype(BF16)
        wr_lo = (wr - wr_hi.astype(F32)).astype(BF16)
        in_specs += [pl.BlockSpec((d, LANES), fixed), pl.BlockSpec((d, LANES), fixed)]
        out_specs.append(pl.BlockSpec((tm, LANES), row))
        out_shape.append(jax.ShapeDtypeStruct((m, LANES), F32))
        args += [wr_hi, wr_lo]
    return pl.pallas_call(
        functools.partial(_proj_out_kernel, n_experts=n_experts, first_sample_tile=first_sample_tile,
                          layout=(len(a_parts), len(b_parts), len(h_parts))),
        grid=(m // tm,),
        in_specs=in_specs, out_specs=out_specs, out_shape=out_shape,
        compiler_params=_params("parallel"),
        name="proj_out",
    )(*args)


def _norm_residual_kernel(m_ref, h_ref, gpost_ref, gnext_ref, hn_ref, xn_ref):
    hn = h_ref[...] + _rms(m_ref[...], gpost_ref[...])
    hn_ref[...] = hn
    xn_ref[...] = _rms(hn, gnext_ref[...]).astype(xn_ref.dtype)


def _norm_residual(mix, h, g_post, g_next, tm):
    m, d = h.shape
    row = lambda i: (i, 0)
    fixed = lambda i: (0, 0)
    return pl.pallas_call(
        _norm_residual_kernel,
        grid=(m // tm,),
        in_specs=[pl.BlockSpec((tm, d), row), pl.BlockSpec((tm, d), row),
                  pl.BlockSpec((1, d), fixed), pl.BlockSpec((1, d), fixed)],
        out_specs=[pl.BlockSpec((tm, d), row), pl.BlockSpec((tm, d), row)],
        out_shape=[jax.ShapeDtypeStruct((m, d), F32), jax.ShapeDtypeStruct((m, d), BF16)],
        compiler_params=_params("parallel"),
        name="norm_residual",
    )(mix, h, g_post.reshape(1, d), g_next.reshape(1, d))


def _ring_step(te_ref, nv_ref, nxt_ref, wrap_ref, cnt_ref, copies, consume):
    j = pl.program_id(0)
    t = pl.program_id(1)

    @pl.when(jnp.logical_and(j == 0, t == 0))
    def _():
        cnt_ref[0] = 0
        for c in copies(te_ref[0], 0, 0):
            c.start()

    group_start = jnp.logical_or(t == 0, te_ref[t] != te_ref[jnp.maximum(t - 1, 0)])

    @pl.when(jnp.logical_and(group_start, t < nv_ref[0]))
    def _():
        slot = cnt_ref[0] % 2
        for c in copies(te_ref[t], j, slot):
            c.wait()
        consume(slot)
        j_next = j + wrap_ref[t]

        @pl.when(j_next < pl.num_programs(0))
        def _():
            for c in copies(nxt_ref[t], j_next, 1 - slot):
                c.start()
        cnt_ref[0] = cnt_ref[0] + 1


def _ffn_up_kernel(te_ref, src_ref, nv_ref, nxt_ref, wrap_ref, half_ref, x_ref, wg_hbm, wu_hbm, o_ref,
                   wbuf, wg_bf, wu_bf, sem, cnt_ref):
    t = pl.program_id(1)
    tf = wg_bf.shape[1]

    def copies(e, j, slot):
        cols = pl.ds(pl.multiple_of(j * tf, tf), tf)
        return (pltpu.make_async_copy(wg_hbm.at[e, :, cols], wbuf.at[slot, 0], sem.at[slot, 0]),
                pltpu.make_async_copy(wu_hbm.at[e, :, cols], wbuf.at[slot, 1], sem.at[slot, 1]))

    def consume(slot):
        wg_bf[...] = wbuf[slot, 0].astype(BF16)
        wu_bf[...] = wbuf[slot, 1].astype(BF16)

    _ring_step(te_ref, nv_ref, nxt_ref, wrap_ref, cnt_ref, copies, consume)

    def rows(r):
        x = x_ref[r, :]
        g = jnp.dot(x, wg_bf[...], preferred_element_type=F32)
        u = jnp.dot(x, wu_bf[...], preferred_element_type=F32)
        o_ref[r, :] = (g * jax.nn.sigmoid(g) * u).astype(o_ref.dtype)

    _tile_rows(t, nv_ref, half_ref, o_ref, rows)


def _tile_rows(t, nv_ref, half_ref, o_ref, rows):
    tm = o_ref.shape[0]
    valid = t < nv_ref[0]
    half = half_ref[t] == 1

    @pl.when(jnp.logical_and(valid, jnp.logical_not(half)))
    def _():
        rows(slice(None))

    @pl.when(jnp.logical_and(valid, half))
    def _():
        rows(slice(0, tm // 2))
        o_ref[tm // 2:, :] = jnp.zeros((tm - tm // 2, o_ref.shape[1]), o_ref.dtype)

    @pl.when(jnp.logical_not(valid))
    def _():
        o_ref[...] = jnp.zeros_like(o_ref)


def _ffn_up(x, w_gate, w_up, tiles, tm, tf):
    s, k = x.shape
    f = w_gate.shape[2]
    n_tiles = s // tm
    grid_spec = pltpu.PrefetchScalarGridSpec(
        num_scalar_prefetch=len(tiles),
        grid=(f // tf, n_tiles),
        in_specs=[pl.BlockSpec((tm, k), lambda j, t, te, src, *_: (src[t], 0)),
                  pl.BlockSpec(memory_space=pl.ANY), pl.BlockSpec(memory_space=pl.ANY)],
        out_specs=pl.BlockSpec((tm, tf), lambda j, t, *_: (t, j)),
        scratch_shapes=[pltpu.VMEM((2, 2, k, tf), F32), pltpu.VMEM((k, tf), BF16), pltpu.VMEM((k, tf), BF16),
                        pltpu.SemaphoreType.DMA((2, 2)), pltpu.SMEM((1,), I32)],
    )
    return pl.pallas_call(
        _ffn_up_kernel, grid_spec=grid_spec,
        out_shape=jax.ShapeDtypeStruct((s, f), BF16),
        compiler_params=_params("arbitrary", "arbitrary"),
        name="ffn_up",
    )(*tiles, x, w_gate, w_up)


def _ffn_down_kernel(te_ref, src_ref, nv_ref, nxt_ref, wrap_ref, half_ref, a_ref, wd_hbm, o_ref,
                     wbuf, wd_bf, sem, cnt_ref):
    t = pl.program_id(1)
    tn = wd_bf.shape[1]

    def copies(e, j, slot):
        cols = pl.ds(pl.multiple_of(j * tn, tn), tn)
        return (pltpu.make_async_copy(wd_hbm.at[e, :, cols], wbuf.at[slot], sem.at[slot]),)

    def consume(slot):
        wd_bf[...] = wbuf[slot].astype(BF16)

    _ring_step(te_ref, nv_ref, nxt_ref, wrap_ref, cnt_ref, copies, consume)

    def rows(r):
        o_ref[r, :] = jnp.dot(a_ref[r, :], wd_bf[...], preferred_element_type=F32)

    _tile_rows(t, nv_ref, half_ref, o_ref, rows)


def _ffn_down(a, w_down, tiles, tm, tn):
    s, f = a.shape
    d = w_down.shape[2]
    n_tiles = s // tm
    grid_spec = pltpu.PrefetchScalarGridSpec(
        num_scalar_prefetch=len(tiles),
        grid=(d // tn, n_tiles),
        in_specs=[pl.BlockSpec((tm, f), lambda j, t, te, src, *_: (src[t], 0)),
                  pl.BlockSpec(memory_space=pl.ANY)],
        out_specs=pl.BlockSpec((tm, tn), lambda j, t, *_: (t, j)),
        scratch_shapes=[pltpu.VMEM((2, f, tn), F32), pltpu.VMEM((f, tn), BF16),
                        pltpu.SemaphoreType.DMA((2,)), pltpu.SMEM((1,), I32)],
    )
    return pl.pallas_call(
        _ffn_down_kernel, grid_spec=grid_spec,
        out_shape=jax.ShapeDtypeStruct((s, d), F32),
        compiler_params=_params("arbitrary", "arbitrary"),
        name="ffn_down",
    )(*tiles, a, w_down)


def _swiglu_grouped(x, w_gate, w_up, w_down, tiles, tm):
    tf = _tile(w_gate.shape[2], 512, LANES)
    tn = _tile(w_down.shape[2], 512, LANES)
    act = _ffn_up(x, w_gate, w_up, tiles, tm, tf)
    return _ffn_down(act, w_down, tiles, tm, tn)


def _dense_tiles(n_tiles):
    zeros = jnp.zeros((n_tiles,), I32)
    return (zeros, jnp.arange(n_tiles, dtype=I32), jnp.full((1,), n_tiles, I32), zeros, jnp.ones((n_tiles,), I32),
            zeros)


GATHER_UNROLL = 8


def _row_copy(src_hbm, row, buf, slot, r, sem):
    return pltpu.make_async_copy(src_hbm.at[pl.ds(row, 1), :], buf.at[slot, pl.ds(r, 1), :], sem.at[slot])


def _wait_slot(src_hbm, buf, slot, sem):
    pltpu.make_async_copy(src_hbm.at[pl.ds(0, buf.shape[1]), :], buf.at[slot], sem.at[slot]).wait()


def _dispatch_kernel(tok_ref, h_hbm, g_ref, o_ref, buf, sem, *, tm):
    t = pl.program_id(0)
    n_t = pl.num_programs(0)

    def issue(tile, slot):
        def body(r, c):
            _row_copy(h_hbm, tok_ref[tile * tm + r], buf, slot, r, sem).start()
            return c
        lax.fori_loop(0, tm, body, 0, unroll=GATHER_UNROLL)

    @pl.when(t == 0)
    def _():
        issue(0, 0)

    @pl.when(t + 1 < n_t)
    def _():
        issue(t + 1, (t + 1) % 2)

    slot = t % 2
    _wait_slot(h_hbm, buf, slot, sem)
    o_ref[...] = _rms(buf[slot], g_ref[...]).astype(o_ref.dtype)


def _dispatch(h, g, slot_token, tm):
    s = slot_token.shape[0]
    d = h.shape[1]
    grid_spec = pltpu.PrefetchScalarGridSpec(
        num_scalar_prefetch=1,
        grid=(s // tm,),
        in_specs=[pl.BlockSpec(memory_space=pl.ANY), pl.BlockSpec((1, d), lambda t, tok: (0, 0))],
        out_specs=pl.BlockSpec((tm, d), lambda t, tok: (t, 0)),
        scratch_shapes=[pltpu.VMEM((2, tm, d), F32), pltpu.SemaphoreType.DMA((2,))],
    )
    return pl.pallas_call(
        functools.partial(_dispatch_kernel, tm=tm), grid_spec=grid_spec,
        out_shape=jax.ShapeDtypeStruct((s, d), BF16),
        compiler_params=_params("arbitrary"),
        name="moe_dispatch",
    )(slot_token, h, g.reshape(1, d))


def _combine_kernel(slot_ref, y_hbm, route_ref, h_ref, gpost_ref, *rest, tm, first_sample_tile):
    out_refs, (buf, sem) = rest[:-2], rest[-2:]
    t = pl.program_id(0)
    n_t = pl.num_programs(0)

    def issue(tile, slot):
        def body(r, c):
            for k in range(TOP_K):
                _row_copy(y_hbm, slot_ref[k, tile * tm + r], buf, slot, k * tm + r, sem).start()
            return c
        lax.fori_loop(0, tm, body, 0, unroll=GATHER_UNROLL // TOP_K)

    @pl.when(t == 0)
    def _():
        issue(0, 0)

    @pl.when(t + 1 < n_t)
    def _():
        issue(t + 1, (t + 1) % 2)

    slot = t % 2
    _wait_slot(y_hbm, buf, slot, sem)
    route = route_ref[...]
    mix = route[:, 0:1] * buf[slot, 0:tm, :]
    for k in range(1, TOP_K):
        mix = mix + route[:, k:k + 1] * buf[slot, k * tm:(k + 1) * tm, :]
    result = h_ref[...] + _rms(mix, gpost_ref[...])
    if len(out_refs) == 1:
        out_refs[0][...] = result
    else:
        @pl.when(t < first_sample_tile)
        def _():
            out_refs[0][...] = result

        @pl.when(t >= first_sample_tile)
        def _():
            out_refs[1][...] = result


def _combine(y_sorted, token_slots, route, h, g_post, tm, split_rows=None):
    m, d = h.shape
    if split_rows is None:
        first_sample_tile = 0
        out_specs = pl.BlockSpec((tm, d), lambda t, sl: (t, 0))
        out_shape = jax.ShapeDtypeStruct((m, d), F32)
    else:
        assert split_rows % tm == 0 and 0 < split_rows < m
        first_sample_tile = split_rows // tm
        out_specs = [pl.BlockSpec((tm, d), lambda t, sl: (jnp.minimum(t, first_sample_tile - 1), 0)),
                     pl.BlockSpec((tm, d), lambda t, sl: (jnp.maximum(t - first_sample_tile, 0), 0))]
        out_shape = [jax.ShapeDtypeStruct((split_rows, d), F32), jax.ShapeDtypeStruct((m - split_rows, d), F32)]
    grid_spec = pltpu.PrefetchScalarGridSpec(
        num_scalar_prefetch=1,
        grid=(m // tm,),
        in_specs=[pl.BlockSpec(memory_space=pl.ANY),
                  pl.BlockSpec((tm, LANES), lambda t, sl: (t, 0)),
                  pl.BlockSpec((tm, d), lambda t, sl: (t, 0)),
                  pl.BlockSpec((1, d), lambda t, sl: (0, 0))],
        out_specs=out_specs,
        scratch_shapes=[pltpu.VMEM((2, TOP_K * tm, d), F32), pltpu.SemaphoreType.DMA((2,))],
    )
    return pl.pallas_call(
        functools.partial(_combine_kernel, tm=tm, first_sample_tile=first_sample_tile), grid_spec=grid_spec,
        out_shape=out_shape,
        compiler_params=_params("arbitrary"),
        name="moe_combine",
    )(token_slots, y_sorted, route, h, g_post.reshape(1, d))


def _routing_tables(route, n_experts, tm):
    m = route.shape[0]
    eid = jnp.concatenate([route[:, 2 + k].astype(I32) for k in range(TOP_K)])
    blk = LANES
    assert (TOP_K * m) % blk == 0 and TOP_K * m < 2 ** 24
    onehot = (eid[:, None] == jnp.arange(n_experts, dtype=I32)[None, :]).astype(F32).reshape(-1, blk, n_experts)
    before = jnp.tril(jnp.ones((blk, blk), F32), -1)
    within = jnp.einsum("ts,bse->bte", before, onehot)
    block_total = jnp.sum(onehot, axis=1)
    block_start = jnp.cumsum(block_total, axis=0) - block_total
    rank = jnp.sum((within + block_start[:, None, :]) * onehot, axis=-1).reshape(-1).astype(I32)
    counts = jnp.sum(block_total, axis=0).astype(I32)
    tiles_e = (counts + tm - 1) // tm
    tile_end = jnp.cumsum(tiles_e)
    start = (tile_end - tiles_e) * tm
    slot = start[eid] + rank
    n_tiles = (TOP_K * m + n_experts * (tm - 1)) // tm
    token = jnp.tile(jnp.arange(m, dtype=I32), TOP_K)
    slot_token = jnp.zeros((n_tiles * tm,), I32).at[slot].set(token)
    n_valid = tile_end[-1].astype(I32)
    tile_id = jnp.arange(n_tiles, dtype=I32)
    tile_src = jnp.minimum(tile_id, n_valid - 1)
    tile_expert = jnp.minimum(jnp.sum((tile_end[None, :] <= tile_src[:, None]).astype(I32), axis=1), n_experts - 1)
    following, cur = [], jnp.int32(-1)
    for e in reversed(range(n_experts)):
        following.append(cur)
        cur = jnp.where(tiles_e[e] > 0, jnp.int32(e), cur)
    following = jnp.stack(following[::-1])
    wraps = following < 0
    following = jnp.where(wraps, cur, following)
    rows_in_last = counts - (tiles_e - 1) * tm
    is_last = tile_src == tile_end[tile_expert] - 1
    half = jnp.logical_and(is_last, rows_in_last[tile_expert] <= tm // 2).astype(I32)
    tiles = (tile_expert, tile_src, n_valid.reshape(1), following[tile_expert], wraps[tile_expert].astype(I32), half)
    return slot_token, slot.reshape(TOP_K, m).astype(I32), tiles


def _pool_prompt_kernel(halo_ref, p_ref, w_ref, s_ref, o_ref, buf_ref, *, tiles_per_seq):
    i = pl.program_id(0) % tiles_per_seq
    t = p_ref.shape[0]
    gw = w_ref.shape[1]
    buf_ref[0:POOL_HALO, :] = jnp.where(i == 0, 0.0, halo_ref[...])
    buf_ref[POOL_HALO:POOL_HALO + t, :] = p_ref[...]
    pos = i * t + lax.broadcasted_iota(I32, (t, 1), 0)
    for g, w in enumerate(POOL_WINDOWS):
        c0 = g * gw
        cur = buf_ref[POOL_HALO:POOL_HALO + t, c0:c0 + gw]
        acc = cur
        for j in range(1, w):
            acc = acc + buf_ref[POOL_HALO - j:POOL_HALO - j + t, c0:c0 + gw]
        cnt = jnp.minimum(pos + 1, w).astype(F32)
        d = acc / cnt - cur
        y = jnp.dot(d.astype(BF16), w_ref[g], preferred_element_type=F32)
        o_ref[:, c0:c0 + gw] = (y * s_ref[:, c0:c0 + gw]).astype(o_ref.dtype)


def _pool_prompt(proj, n_rows, seq, w_pool, scale, t):
    c = scale.shape[0]
    halo_per_tile = t // POOL_HALO
    return pl.pallas_call(
        functools.partial(_pool_prompt_kernel, tiles_per_seq=seq // t),
        grid=(n_rows // t,),
        in_specs=[pl.BlockSpec((POOL_HALO, c), lambda i: (jnp.maximum(i * halo_per_tile - 1, 0), 0)),
                  pl.BlockSpec((t, c), lambda i: (i, 0)),
                  pl.BlockSpec(w_pool.shape, lambda i: (0, 0, 0)),
                  pl.BlockSpec((1, c), lambda i: (0, 0))],
        out_specs=pl.BlockSpec((t, c), lambda i: (i, 0)),
        out_shape=jax.ShapeDtypeStruct((n_rows, c), BF16),
        scratch_shapes=[pltpu.VMEM((POOL_HALO + t, c), F32)],
        compiler_params=_params("parallel"),
        name="pool_prompt",
    )(proj, proj, w_pool, scale.reshape(1, c))


def _pool_sample_kernel(ctx_ref, w_ref, s_ref, o_ref, *, n_ctx, pos0):
    gw = w_ref.shape[1]
    for t in range(o_ref.shape[0]):
        hi = n_ctx + t + 1
        for g, w in enumerate(POOL_WINDOWS):
            c0 = g * gw
            lo = max(hi - w, 0)
            acc = ctx_ref[lo, :, c0:c0 + gw]
            for r in range(lo + 1, hi):
                acc = acc + ctx_ref[r, :, c0:c0 + gw]
            d = acc / float(min(pos0 + t + 1, w)) - ctx_ref[hi - 1, :, c0:c0 + gw]
            y = jnp.dot(d.astype(BF16), w_ref[g], preferred_element_type=F32)
            o_ref[t, :, c0:c0 + gw] = (y * s_ref[:, c0:c0 + gw]).astype(o_ref.dtype)


def _pool_sample(ctx_t, n_ctx, pos0, w_pool, scale):
    total, n, c = ctx_t.shape
    return pl.pallas_call(
        functools.partial(_pool_sample_kernel, n_ctx=n_ctx, pos0=pos0),
        out_shape=jax.ShapeDtypeStruct((total - n_ctx, n, c), BF16),
        compiler_params=pltpu.CompilerParams(vmem_limit_bytes=VMEM_LIMIT_BYTES),
        name="pool_sample",
    )(ctx_t, w_pool, scale.reshape(1, c))


def _gmlp_kernel(u_ref, v_ref, w_ref, b_ref, o_ref):
    t = u_ref.shape[0]
    n_heads = w_ref.shape[0]
    ck = w_ref.shape[1]
    row = lax.broadcasted_iota(I32, (ck, ck), 0)
    col = lax.broadcasted_iota(I32, (ck, ck), 1)
    for h in range(n_heads):
        c0 = h * HEAD_DIM
        wm = jnp.where(col <= row, w_ref[h], 0.0).astype(BF16)
        bias = b_ref[:, h:h + 1]
        for k in range(t // ck):
            r0 = k * ck
            v = v_ref[r0:r0 + ck, c0:c0 + HEAD_DIM].astype(BF16)
            mixed = jnp.dot(wm, v, preferred_element_type=F32) + bias
            o_ref[r0:r0 + ck, c0:c0 + HEAD_DIM] = (u_ref[r0:r0 + ck, c0:c0 + HEAD_DIM] * mixed).astype(o_ref.dtype)


def _gmlp(proj, n_prompt_rows, w_pair, b_pair, t):
    m = proj.shape[0]
    c = proj.shape[1] // 3
    n_heads = w_pair.shape[1]
    first_sample_tile = n_prompt_rows // t
    which = lambda i: jnp.where(i >= first_sample_tile, 1, 0)
    return pl.pallas_call(
        _gmlp_kernel,
        grid=(m // t,),
        in_specs=[pl.BlockSpec((t, c), lambda i: (i, 1)),
                  pl.BlockSpec((t, c), lambda i: (i, 2)),
                  pl.BlockSpec((None, n_heads, GMLP_CHUNK, GMLP_CHUNK), lambda i: (which(i), 0, 0, 0)),
                  pl.BlockSpec((None, GMLP_CHUNK, n_heads), lambda i: (which(i), 0, 0))],
        out_specs=pl.BlockSpec((t, c), lambda i: (i, 0)),
        out_shape=jax.ShapeDtypeStruct((m, c), BF16),
        compiler_params=_params("parallel"),
        name="gmlp",
    )(proj, proj, w_pair, b_pair)


def _conv_prompt_kernel(xc_ref, bg_ref, cg_ref, xch_ref, cgh_ref, w_ref, o_ref, tail_ref, buf_ref, *, tiles_per_seq):
    i = pl.program_id(0) % tiles_per_seq
    t = xc_ref.shape[0]
    z = cg_ref[...] * xc_ref[...]
    buf_ref[0:CONV_HALO, :] = jnp.where(i == 0, 0.0, cgh_ref[...] * xch_ref[...])
    buf_ref[CONV_HALO:CONV_HALO + t, :] = z
    y = w_ref[0:1, :] * buf_ref[CONV_HALO - 2:CONV_HALO - 2 + t, :]
    for j in range(1, CONV_WIDTH):
        y = y + w_ref[j:j + 1, :] * buf_ref[CONV_HALO - 2 + j:CONV_HALO - 2 + j + t, :]
    o_ref[...] = (bg_ref[...] * y).astype(o_ref.dtype)
    tail_ref[...] = z[t - CONV_HALO:, :]


def _conv_prompt(proj, n_rows, seq, conv_w, t):
    c = conv_w.shape[1]
    halo_per_tile = t // CONV_HALO
    halo_row = lambda i: jnp.maximum(i * halo_per_tile - 1, 0)
    return pl.pallas_call(
        functools.partial(_conv_prompt_kernel, tiles_per_seq=seq // t),
        grid=(n_rows // t,),
        in_specs=[pl.BlockSpec((t, c), lambda i: (i, 0)),
                  pl.BlockSpec((t, c), lambda i: (i, 1)),
                  pl.BlockSpec((t, c), lambda i: (i, 2)),
                  pl.BlockSpec((CONV_HALO, c), lambda i: (halo_row(i), 0)),
                  pl.BlockSpec((CONV_HALO, c), lambda i: (halo_row(i), 2)),
                  pl.BlockSpec((CONV_WIDTH, c), lambda i: (0, 0))],
        out_specs=[pl.BlockSpec((t, c), lambda i: (i, 0)),
                   pl.BlockSpec((CONV_HALO, c), lambda i: (i, 0))],
        out_shape=[jax.ShapeDtypeStruct((n_rows, c), BF16),
                   jax.ShapeDtypeStruct((n_rows // t * CONV_HALO, c), F32)],
        scratch_shapes=[pltpu.VMEM((CONV_HALO + t, c), F32)],
        compiler_params=_params("parallel"),
        name="conv_prompt",
    )(proj, proj, proj, proj, proj, conv_w)


def _conv_sample_kernel(ctx_ref, xc_ref, bg_ref, cg_ref, w_ref, o_ref, z_ref):
    n_ctx = ctx_ref.shape[0]
    length = xc_ref.shape[0]
    rows = [ctx_ref[r] for r in range(n_ctx)]
    for t in range(length):
        z = cg_ref[t] * xc_ref[t]
        z_ref[t] = z
        rows.append(z)
    for t in range(length):
        y = w_ref[0:1, :] * rows[t]
        for j in range(1, CONV_WIDTH):
            y = y + w_ref[j:j + 1, :] * rows[t + j]
        o_ref[t] = (bg_ref[t] * y).astype(o_ref.dtype)


def _conv_sample(ctx_t, xc_t, bg_t, cg_t, conv_w):
    length, n, c = xc_t.shape
    return pl.pallas_call(
        _conv_sample_kernel,
        out_shape=[jax.ShapeDtypeStruct((length, n, c), BF16), jax.ShapeDtypeStruct((length, n, c), F32)],
        compiler_params=pltpu.CompilerParams(vmem_limit_bytes=VMEM_LIMIT_BYTES),
        name="conv_sample",
    )(ctx_t, xc_t, bg_t, cg_t, conv_w)


def _log_sigmoid(x):
    return jnp.minimum(x, 0.0) - jnp.log1p(jnp.exp(-jnp.abs(x)))


def _logf_kernel(f_ref, b_ref, lf_ref, c_ref, carry_ref, *, blocks_per_seq):
    i = pl.program_id(0)
    blk = f_ref.shape[0]
    lf = _log_sigmoid(f_ref[...] + b_ref[...])
    lf_ref[...] = lf

    @pl.when(i % blocks_per_seq == 0)
    def _():
        carry_ref[...] = jnp.zeros_like(carry_ref)

    row = lax.broadcasted_iota(I32, (blk, blk), 0)
    col = lax.broadcasted_iota(I32, (blk, blk), 1)
    tri = jnp.where(col <= row, 1.0, 0.0).astype(F32)
    c = jnp.dot(tri, lf, precision=HIGHEST, preferred_element_type=F32) + carry_ref[0:1, :]
    c_ref[...] = c
    carry_ref[...] = jnp.broadcast_to(c[blk - 1:blk, :], carry_ref.shape)


def _logf_cumsum(f_logit, b_forget, seq, blk):
    m = f_logit.shape[0]
    return pl.pallas_call(
        functools.partial(_logf_kernel, blocks_per_seq=seq // blk),
        grid=(m // blk,),
        in_specs=[pl.BlockSpec((blk, LANES), lambda i: (i, 0)), pl.BlockSpec((1, LANES), lambda i: (0, 0))],
        out_specs=[pl.BlockSpec((blk, LANES), lambda i: (i, 0)), pl.BlockSpec((blk, LANES), lambda i: (i, 0))],
        out_shape=[jax.ShapeDtypeStruct((m, LANES), F32), jax.ShapeDtypeStruct((m, LANES), F32)],
        scratch_shapes=[pltpu.VMEM((8, LANES), F32)],
        compiler_params=_params("arbitrary"),
        name="logf_cumsum",
    )(f_logit, b_forget)


FLASH_HEADS = 2


def _flash_kernel(q_ref, k_ref, v_ref, cc_ref, cr_ref, o_ref, *, tq):
    hp = pl.program_id(1)
    qb = pl.program_id(2)
    lane = lax.broadcasted_iota(I32, (tq, LANES), 1)
    heads = []
    for hh in range(FLASH_HEADS):
        cols = slice(hh * HEAD_DIM, (hh + 1) * HEAD_DIM)
        q = (q_ref[:, cols] * (1.0 / math.sqrt(HEAD_DIM))).astype(BF16)
        cq = jnp.sum(jnp.where(lane == hp * FLASH_HEADS + hh, cc_ref[...], 0.0), axis=-1, keepdims=True)
        heads.append((cols, q, cq))

    def scores(hh, j):
        cols, q, cq = heads[hh]
        start = pl.multiple_of(j * tq, tq)
        k = k_ref[pl.ds(start, tq), cols].astype(BF16)
        v = v_ref[pl.ds(start, tq), cols].astype(BF16)
        s = lax.dot_general(q, k, (((1,), (1,)), ((), ())), preferred_element_type=F32)
        ck = cr_ref[hh, :, pl.ds(start, tq)]
        return s + (cq - ck), v

    def update(carry, s, v):
        m, l, acc = carry
        m_new = jnp.maximum(m, jnp.max(s, axis=-1, keepdims=True))
        alpha = jnp.exp(m - m_new)
        p = jnp.exp(s - m_new)
        l = alpha * l + jnp.sum(p, axis=-1, keepdims=True)
        acc = alpha * acc + jnp.dot(p.astype(BF16), v, preferred_element_type=F32)
        return m_new, l, acc

    def body(j, carries):
        return tuple(update(carries[hh], *scores(hh, j)) for hh in range(FLASH_HEADS))

    init = (jnp.full((tq, 1), MASKED, F32), jnp.zeros((tq, 1), F32), jnp.zeros((tq, HEAD_DIM), F32))
    carries = lax.fori_loop(0, qb, body, (init,) * FLASH_HEADS)
    row = lax.broadcasted_iota(I32, (tq, tq), 0)
    col = lax.broadcasted_iota(I32, (tq, tq), 1)
    for hh in range(FLASH_HEADS):
        s, v = scores(hh, qb)
        _, l, acc = update(carries[hh], jnp.where(col <= row, s, MASKED), v)
        o_ref[:, heads[hh][0]] = (acc / l).astype(o_ref.dtype)


def _flash_prompt(proj, c_col, c_row, n_seq, seq, n_heads, tq):
    assert n_heads % FLASH_HEADS == 0
    pairs = n_heads // FLASH_HEADS
    width = FLASH_HEADS * HEAD_DIM
    q_blk, k_blk, v_blk = 3 * pairs, 4 * pairs, 5 * pairs
    nqb = seq // tq
    return pl.pallas_call(
        functools.partial(_flash_kernel, tq=tq),
        grid=(n_seq, pairs, nqb),
        in_specs=[pl.BlockSpec((tq, width), lambda n, h, i: (n * nqb + i, q_blk + h)),
                  pl.BlockSpec((seq, width), lambda n, h, i: (n, k_blk + h)),
                  pl.BlockSpec((seq, width), lambda n, h, i: (n, v_blk + h)),
                  pl.BlockSpec((tq, LANES), lambda n, h, i: (n * nqb + i, 0)),
                  pl.BlockSpec((FLASH_HEADS, 1, seq), lambda n, h, i: (h, 0, n))],
        out_specs=pl.BlockSpec((tq, width), lambda n, h, i: (n * nqb + i, h)),
        out_shape=jax.ShapeDtypeStruct((n_seq * seq, n_heads * HEAD_DIM), BF16),
        compiler_params=_params("parallel", "parallel", "arbitrary"),
        name="fox_prompt",
    )(proj, proj, proj, c_col, c_row)


def _decay_of_later_rows(lf, later_groups, n_heads):
    group, width = lf.shape
    lane = lax.broadcasted_iota(I32, lf.shape, 1)
    incl, total = lf, lf
    shift = n_heads
    while shift < width:
        incl = incl + jnp.where(lane + shift < width, pltpu.roll(incl, width - shift, 1), 0.0)
        total = total + pltpu.roll(total, shift, 1)
        shift *= 2
    later_pages = jnp.zeros_like(lf)
    if group > 1:
        prow = lax.broadcasted_iota(I32, (group, group), 0)
        pcol = lax.broadcasted_iota(I32, (group, group), 1)
        later = jnp.where(pcol > prow, 1.0, 0.0).astype(F32)
        later_pages = jnp.dot(later, total, precision=HIGHEST, preferred_element_type=F32)
    return (incl - lf) + later_pages + later_groups, jnp.sum(total, axis=0, keepdims=True)


def _decode_kernel(pt_ref, q_ref, kn_ref, vn_ref, lfn_ref, *refs, n_heads, n_q, group, n_par):
    pages = 3 * group * n_par
    o_ref, m_ref, l_ref, acc_ref, later_ref = refs[pages:]
    j = pl.program_id(1)
    rows = n_heads * n_q
    width = refs[0].shape[0] * n_heads

    @pl.when(j == 0)
    def _():
        m_ref[...] = jnp.full_like(m_ref, MASKED)
        l_ref[...] = jnp.zeros_like(l_ref)
        acc_ref[...] = jnp.zeros_like(acc_ref)
        later_ref[...] = jnp.zeros_like(later_ref)

    r_i = lax.broadcasted_iota(I32, (rows, LANES), 0)
    c_i = lax.broadcasted_iota(I32, (rows, LANES), 1)
    sel = (c_i % n_heads == r_i // n_q) & (c_i // n_heads <= r_i % n_q) & (c_i < n_q * n_heads)
    r_w = lax.broadcasted_iota(I32, (rows, group * width), 0)
    c_w = lax.broadcasted_iota(I32, (rows, group * width), 1)
    same_head = c_w % n_heads == r_w // n_q

    def update(state, s, v):
        m_old, l, acc = state
        m_new = jnp.maximum(m_old, jnp.max(s, axis=-1, keepdims=True))
        alpha = jnp.exp(m_old - m_new)
        p = jnp.exp(s - m_new)
        l = alpha * l + jnp.sum(p, axis=-1, keepdims=True)
        acc = alpha * acc + jnp.dot(p.astype(BF16), v, preferred_element_type=F32)
        return m_new, l, acc

    def load_state(b):
        return m_ref[b, :, 0:1], l_ref[b, :, 0:1], acc_ref[b]

    def store_state(b, state):
        m, l, acc = state
        m_ref[b] = jnp.broadcast_to(m, m_ref.shape[1:])
        l_ref[b] = jnp.broadcast_to(l, l_ref.shape[1:])
        acc_ref[b] = acc

    states = [load_state(b) for b in range(n_par)]
    later = [later_ref[b] for b in range(n_par)]
    queries = []
    for b in range(n_par):
        base = 3 * group * b
        k_refs, v_refs = refs[base:base + group], refs[base + group:base + 2 * group]
        lf_refs = refs[base + 2 * group:base + 3 * group]
        lf = jnp.concatenate([r[...] for r in lf_refs], axis=0)
        decay, group_total = _decay_of_later_rows(lf, later[b][0:1, :], n_heads)
        later[b] = later[b] + group_total

        q = (q_ref[b] * (1.0 / math.sqrt(HEAD_DIM))).astype(BF16)
        lfn = lfn_ref[b]
        cq = jnp.sum(jnp.where(sel, jnp.broadcast_to(lfn, (rows, LANES)), 0.0), axis=-1, keepdims=True)
        queries.append((q, lfn, cq))

        k = jnp.concatenate([r[...].reshape(width, HEAD_DIM).astype(BF16) for r in k_refs], axis=0)
        v = jnp.concatenate([r[...].reshape(width, HEAD_DIM).astype(BF16) for r in v_refs], axis=0)
        decay_row = jnp.concatenate([decay[p:p + 1, :] for p in range(group)], axis=1)
        s = lax.dot_general(q, k, (((1,), (1,)), ((), ())), preferred_element_type=F32)
        s = jnp.where(same_head, s + decay_row + cq, MASKED)
        states[b] = update(states[b], s, v)
    for b in range(n_par):
        store_state(b, states[b])
        later_ref[b] = later[b]

    @pl.when(j == pl.num_programs(1) - 1)
    def _():
        for b in range(n_par):
            q, lfn, cq = queries[b]
            kn = kn_ref[b].astype(BF16)
            vn = vn_ref[b].astype(BF16)
            sn = lax.dot_general(q, kn, (((1,), (1,)), ((), ())), preferred_element_type=F32)
            lf8 = jnp.broadcast_to(lfn, (8, LANES))
            lane8 = lax.broadcasted_iota(I32, (8, LANES), 1)
            csum = lf8
            shift = n_heads
            while shift < n_q * n_heads:
                csum = csum + jnp.where(lane8 >= shift, pltpu.roll(csum, shift, 1), 0.0)
                shift *= 2
            _, l, acc = update(states[b], jnp.where(sel, sn + (cq - csum[0:1, :]), MASKED), vn)
            o_ref[b] = acc / l


def _decode_attention(q_hq, cache_k, cache_v, logf_pages, layer, page_table, k_new, v_new, lf_new, n_heads, n_q):
    n, n_pages = page_table.shape
    assert n_pages >= 1
    page = cache_k.shape[2]
    rows = n_heads * n_q
    width = page * n_heads
    group = max(g for g in (1, 2, 4, 8) if n_pages % g == 0)
    n_groups = n_pages // group
    n_par = 2 if n % 2 == 0 else 1

    def physical_page(i, j, pt, b, p):
        return pt[(i * n_par + b) * n_pages + (n_groups - 1 - j) * group + p]

    def page_spec(b, p):
        return pl.BlockSpec((None, None, page, n_heads, HEAD_DIM),
                            lambda i, j, pt: (layer, physical_page(i, j, pt, b, p), 0, 0, 0))

    def logf_spec(b, p):
        return pl.BlockSpec((None, 1, width), lambda i, j, pt: (physical_page(i, j, pt, b, p), 0, 0))

    per_seq = lambda shape: pl.BlockSpec((n_par,) + shape, lambda i, j, pt: (i, 0, 0))
    page_specs, page_args = [], []
    for b in range(n_par):
        page_specs += [page_spec(b, p) for p in range(group)] * 2 + [logf_spec(b, p) for p in range(group)]
        page_args += [cache_k] * group + [cache_v] * group + [logf_pages] * group
    grid_spec = pltpu.PrefetchScalarGridSpec(
        num_scalar_prefetch=1,
        grid=(n // n_par, n_groups),
        in_specs=[per_seq((rows, HEAD_DIM)), per_seq((LANES, HEAD_DIM)), per_seq((LANES, HEAD_DIM)),
                  per_seq((1, LANES))] + page_specs,
        out_specs=per_seq((rows, HEAD_DIM)),
        scratch_shapes=[pltpu.VMEM((n_par, rows, LANES), F32), pltpu.VMEM((n_par, rows, LANES), F32),
                        pltpu.VMEM((n_par, rows, HEAD_DIM), F32), pltpu.VMEM((n_par, 8, width), F32)],
    )
    return pl.pallas_call(
        functools.partial(_decode_kernel, n_heads=n_heads, n_q=n_q, group=group, n_par=n_par),
        grid_spec=grid_spec,
        out_shape=jax.ShapeDtypeStruct((n, rows, HEAD_DIM), F32),
        compiler_params=_params("parallel", "arbitrary"),
        name="fox_decode",
    )(page_table.reshape(-1), q_hq, k_new, v_new, lf_new, *page_args)


def kernel(x_prompt, x_sample, state_pool, state_conv, cache_k, cache_v, cache_logf, page_table, e_norm_mix_pre, e_norm_mix_post, e_norm_ffn_pre, e_norm_ffn_post, e_w_in, e_w_pool, e_pool_scale, e_w_spatial, e_b_spatial, e_w_out, e_ffn_gate, e_ffn_up, e_ffn_down, o_norm_mix_pre, o_norm_mix_post, o_norm_ffn_pre, o_norm_ffn_post, o_w_in, o_conv_w, o_b_forget, o_w_out, o_w_router, o_exp_gate, o_exp_up, o_exp_down):
    n_p, seq, d = x_prompt.shape
    n_s, dec, _ = x_sample.shape
    mp, ms = n_p * seq, n_s * dec
    m = mp + ms
    c = d // 2
    n_heads = c // HEAD_DIM
    n_pages = page_table.shape[1]
    page = cache_k.shape[2]
    past_len = n_pages * page
    n_layers = e_w_in.shape[0] + o_w_in.shape[0]
    n_experts = o_w_router.shape[2]
    assert ms % GMLP_CHUNK == 0 and seq % GMLP_CHUNK == 0 and past_len % GMLP_CHUNK == 0
    assert GMLP_CHUNK % min(dec, GMLP_CHUNK) == 0 and dec % min(dec, GMLP_CHUNK) == 0
    assert dec * n_heads <= LANES and dec >= CONV_WIDTH - 1

    tm = _tile(math.gcd(mp, ms), 512, LANES)
    t_seq = _tile(math.gcd(seq, tm), 512, LANES)
    tq = _tile(seq, 512, LANES)
    tm_moe = 512
    dense_tiles = _dense_tiles(m // tm)

    h = (x_prompt.reshape(mp, d), x_sample.reshape(ms, d))
    xn = _norm_cast(h, e_norm_mix_pre[0], tm)
    out_rows = None

    pool_p, pool_s, gv_s, conv_p, conv_s = [], [], [], [], []
    kp_l, vp_l, lp_l, ks_l, vs_l, ls_l = [], [], [], [], [], []

    def time_major(x2d):
        return jnp.transpose(x2d.reshape(n_s, dec, -1), (1, 0, 2))

    def row_major(x3d):
        return jnp.transpose(x3d, (1, 0, 2)).reshape(ms, -1)

    for layer in range(n_layers):
        i = layer // 2
        if layer % 2 == 0:
            g_next = e_norm_ffn_pre[i]
            proj = _mm(xn, e_w_in[i], 3 * c, tm, _tile(3 * c, 1024, LANES))
            p_s = proj[mp:, :c].reshape(n_s, dec, c)
            ctx = jnp.concatenate([state_pool[i].astype(F32), p_s], axis=1)
            n_ctx = state_pool.shape[2]
            w_pool = e_w_pool[i].astype(BF16)
            a_p = _pool_prompt(proj, mp, seq, w_pool, e_pool_scale[i], t_seq)
            a_s = row_major(_pool_sample(jnp.transpose(ctx, (1, 0, 2)), n_ctx, past_len, w_pool, e_pool_scale[i]))
            cl = min(dec, GMLP_CHUNK)
            reps = GMLP_CHUNK // cl
            w_samp = jnp.einsum("ab,hts->hatbs", jnp.eye(reps, dtype=F32),
                                e_w_spatial[i][:, :cl, :cl]).reshape(n_heads, GMLP_CHUNK, GMLP_CHUNK)
            b_samp = jnp.tile(e_b_spatial[i][:, :cl], (1, reps))
            w_pair = jnp.stack([e_w_spatial[i], w_samp])
            b_pair = jnp.stack([e_b_spatial[i].T, b_samp.T])
            b = _gmlp(proj, mp, w_pair, b_pair, tm)
            h, xn = _proj_out((a_p, a_s), (b,), e_w_out[i].astype(BF16), h, e_norm_mix_post[i], g_next, tm, mp)
            g_after = o_norm_mix_pre[i] if layer + 1 < n_layers else jnp.ones((d,), F32)
            mix = _swiglu_grouped(xn, e_ffn_gate[i][None], e_ffn_up[i][None], e_ffn_down[i][None], dense_tiles, tm)
            h, xn = _norm_residual(mix, h, e_norm_ffn_post[i], g_after, tm)
            h = (h,)
            pool_p.append(jnp.stack([proj[(s + 1) * seq - n_ctx:(s + 1) * seq, :c] for s in range(n_p)]))
            pool_s.append(ctx[:, ctx.shape[1] - n_ctx:])
            gv_s.append(proj[mp:, 2 * c:].reshape(n_s, dec, c))
        else:
            w_in = o_w_in[i]
            proj, k_p, v_p = _mm(xn, w_in, 6 * c, tm, c, copy_cols=(4, 5), copy_rows=mp)
            w_f = jnp.zeros((d, LANES), F32).at[:, :n_heads].set(w_in[:, 6 * c:])
            f_logit = _mm(xn, w_f, LANES, tm, LANES)
            b_f = jnp.zeros((1, LANES), F32).at[0, :n_heads].set(o_b_forget[i])
            logf, csum = _logf_cumsum(f_logit, b_f, seq, _tile(math.gcd(seq, m), 512, LANES))
            c_p, z_tail = _conv_prompt(proj, mp, seq, o_conv_w[i], t_seq)
            z_tail = z_tail.reshape(n_p, seq // t_seq, CONV_HALO, c)[:, -1, CONV_HALO - (CONV_WIDTH - 1):]
            xs = proj[mp:]
            conv_ctx = jnp.transpose(state_conv[i].astype(F32), (1, 0, 2))
            c_s, z_s = _conv_sample(conv_ctx, time_major(xs[:, :c]), time_major(xs[:, c:2 * c]),
                                    time_major(xs[:, 2 * c:3 * c]), o_conv_w[i])
            z_all = jnp.concatenate([conv_ctx, z_s], axis=0)
            c_row = csum[:mp, :n_heads].T.reshape(n_heads, 1, mp)
            att_p = _flash_prompt(proj, csum, c_row, n_p, seq, n_heads, tq)
            k_s = xs[:, 4 * c:5 * c]
            v_s = xs[:, 5 * c:6 * c]
            q_hq = jnp.transpose(xs[:, 3 * c:4 * c].reshape(n_s, dec, n_heads, HEAD_DIM), (0, 2, 1, 3))
            q_hq = q_hq.reshape(n_s, n_heads * dec, HEAD_DIM)
            pad_rows = LANES - dec * n_heads
            k_new = jnp.pad(k_s.reshape(n_s, dec * n_heads, HEAD_DIM), ((0, 0), (0, pad_rows), (0, 0)))
            v_new = jnp.pad(v_s.reshape(n_s, dec * n_heads, HEAD_DIM), ((0, 0), (0, pad_rows), (0, 0)))
            lf_new = jnp.pad(logf[mp:, :n_heads].reshape(n_s, 1, dec * n_heads), ((0, 0), (0, 0), (0, pad_rows)))
            logf_pages = cache_logf[i].astype(F32).reshape(cache_logf.shape[1], 1, page * n_heads)
            att_s = _decode_attention(q_hq, cache_k, cache_v, logf_pages, i, page_table, k_new, v_new, lf_new,
                                      n_heads, dec)
            att_s = jnp.transpose(att_s.reshape(n_s, n_heads, dec, HEAD_DIM), (0, 2, 1, 3)).reshape(ms, c)
            h, _, route = _proj_out((c_p, row_major(c_s)), (att_p, att_s.astype(BF16)), o_w_out[i].astype(BF16), h,
                                    o_norm_mix_post[i], o_norm_ffn_pre[i], tm, mp, w_router=o_w_router[i])
            slot_token, token_slots, expert_tiles = _routing_tables(route, n_experts, tm_moe)
            x_sorted = _dispatch(h, o_norm_ffn_pre[i], slot_token, tm_moe)
            y_sorted = _swiglu_grouped(x_sorted, o_exp_gate[i], o_exp_up[i], o_exp_down[i], expert_tiles, tm_moe)
            t_comb = _tile(math.gcd(mp, ms), 256)
            if layer + 1 < n_layers:
                h = (_combine(y_sorted, token_slots, route, h, o_norm_ffn_post[i], t_comb),)
                xn = _norm_cast(h, e_norm_mix_pre[i + 1], tm)
            else:
                out_rows = _combine(y_sorted, token_slots, route, h, o_norm_ffn_post[i], t_comb, split_rows=mp)
            conv_p.append(z_tail)
            conv_s.append(jnp.transpose(z_all[z_all.shape[0] - (CONV_WIDTH - 1):], (1, 0, 2)))
            kp_l.append(k_p.reshape(n_p, seq, n_heads, HEAD_DIM))
            vp_l.append(v_p.reshape(n_p, seq, n_heads, HEAD_DIM))
            lp_l.append(logf[:mp, :n_heads].reshape(n_p, seq, n_heads))
            ks_l.append(k_s.reshape(n_s, dec, n_heads, HEAD_DIM))
            vs_l.append(v_s.reshape(n_s, dec, n_heads, HEAD_DIM))
            ls_l.append(logf[mp:, :n_heads].reshape(n_s, dec, n_heads))

    if out_rows is None:
        out_rows = (h[0][:mp], h[0][mp:])
    return (out_rows[0].reshape(n_p, seq, d), out_rows[1].reshape(n_s, dec, d),
            jnp.stack(pool_p), jnp.stack(pool_s), jnp.stack(gv_s), jnp.stack(conv_p), jnp.stack(conv_s),
            jnp.stack(kp_l), jnp.stack(vp_l), jnp.stack(lp_l), jnp.stack(ks_l), jnp.stack(vs_l), jnp.stack(ls_l))
```

```python
import functools
import math

import jax
import jax.numpy as jnp
from jax import lax
from jax.experimental import pallas as pl
from jax.experimental.pallas import tpu as pltpu

F32, BF16, I32 = jnp.float32, jnp.bfloat16, jnp.int32
RMS_EPS = 1e-6
POOL_WINDOWS = (2, 4, 8, 16)
POOL_HALO = 16
CONV_WIDTH = 3
CONV_HALO = 8
HEAD_DIM = 128
GMLP_CHUNK = 128
TOP_K = 2
LANES = 128
MASKED = -1e30
VMEM_LIMIT_BYTES = 56 * 1024 * 1024
HIGHEST = lax.Precision.HIGHEST


def _params(*semantics):
    return pltpu.CompilerParams(dimension_semantics=semantics, vmem_limit_bytes=VMEM_LIMIT_BYTES)


def _tile(n, pref, mult=8):
    t = min(n, pref)
    t -= t % mult
    while t > mult and n % t:
        t -= mult
    assert t > 0 and n % t == 0, (n, pref, mult)
    return t


def _rms(x, g):
    return x * lax.rsqrt(jnp.mean(x * x, axis=-1, keepdims=True) + RMS_EPS) * g


def _row_specs(parts, tm):
    if len(parts) == 1:
        return [pl.BlockSpec((tm, parts[0].shape[1]), lambda i, *_: (i, 0))]
    first_sample_tile = parts[0].shape[0] // tm
    return [pl.BlockSpec((tm, parts[0].shape[1]), lambda i, *_: (jnp.minimum(i, first_sample_tile - 1), 0)),
            pl.BlockSpec((tm, parts[1].shape[1]), lambda i, *_: (jnp.maximum(i - first_sample_tile, 0), 0))]


def _read_rows(refs, first_sample_tile, tile):
    if len(refs) == 1:
        return refs[0][...]
    return jnp.where(tile < first_sample_tile, refs[0][...], refs[1][...])


def _norm_cast_kernel(*refs, first_sample_tile):
    g_ref, o_ref = refs[-2:]
    x = _read_rows(refs[:-2], first_sample_tile, pl.program_id(0))
    o_ref[...] = _rms(x, g_ref[...]).astype(o_ref.dtype)


def _norm_cast(x_parts, g, tm):
    m = sum(p.shape[0] for p in x_parts)
    d = x_parts[0].shape[1]
    return pl.pallas_call(
        functools.partial(_norm_cast_kernel, first_sample_tile=x_parts[0].shape[0] // tm),
        grid=(m // tm,),
        in_specs=_row_specs(x_parts, tm) + [pl.BlockSpec((1, d), lambda i: (0, 0))],
        out_specs=pl.BlockSpec((tm, d), lambda i: (i, 0)),
        out_shape=jax.ShapeDtypeStruct((m, d), BF16),
        compiler_params=_params("parallel"),
        name="norm_cast",
    )(*x_parts, g.reshape(1, d))


def _mm_kernel(x_ref, w_ref, o_ref, *rest, copy_cols, copy_tiles):
    copy_refs, w_bf = rest[:-1], rest[-1]

    @pl.when(pl.program_id(1) == 0)
    def _():
        w_bf[...] = w_ref[...].astype(BF16)

    y = jnp.dot(x_ref[...], w_bf[...], preferred_element_type=F32)
    o_ref[...] = y
    for ref, col in zip(copy_refs, copy_cols):
        @pl.when(jnp.logical_and(pl.program_id(0) == col, pl.program_id(1) < copy_tiles))
        def _(ref=ref):
            ref[...] = y


def _mm(x, w, n_cols, tm, tn, copy_cols=(), copy_rows=0, w_layer=0):
    m, k = x.shape
    n = n_cols
    copy_tiles = copy_rows // tm
    assert copy_rows % tm == 0 and n % tn == 0 and n <= w.shape[-1]
    if w.ndim == 3:
        w_spec = pl.BlockSpec((None, k, tn), lambda j, i: (w_layer, 0, j))
    else:
        w_spec = pl.BlockSpec((k, tn), lambda j, i: (0, j))

    def copy_spec(col):
        def index(j, i):
            row = jnp.where(j < col, 0, jnp.where(j > col, copy_tiles - 1, jnp.minimum(i, copy_tiles - 1)))
            return (row, 0)
        return pl.BlockSpec((tm, tn), index)

    out = pl.pallas_call(
        functools.partial(_mm_kernel, copy_cols=tuple(copy_cols), copy_tiles=copy_tiles),
        grid=(n // tn, m // tm),
        in_specs=[pl.BlockSpec((tm, k), lambda j, i: (i, 0)), w_spec],
        out_specs=[pl.BlockSpec((tm, tn), lambda j, i: (i, j))] + [copy_spec(col) for col in copy_cols],
        out_shape=[jax.ShapeDtypeStruct((m, n), F32)] + [jax.ShapeDtypeStruct((copy_rows, tn), F32)] * len(copy_cols),
        scratch_shapes=[pltpu.VMEM((k, tn), BF16)],
        compiler_params=_params("arbitrary", "arbitrary"),
        name="proj_in",
    )(x, w)
    return out if copy_cols else out[0]


def _top2(logits, n_experts):
    lane = lax.broadcasted_iota(I32, logits.shape, 1)
    neg_inf = jnp.float32(-jnp.inf)
    l1 = jnp.where(lane < n_experts, logits, neg_inf)
    m1 = jnp.max(l1, axis=-1, keepdims=True)
    i1 = jnp.min(jnp.where(l1 == m1, lane, LANES), axis=-1, keepdims=True)
    l2 = jnp.where(lane == i1, neg_inf, l1)
    m2 = jnp.max(l2, axis=-1, keepdims=True)
    i2 = jnp.min(jnp.where(l2 == m2, lane, LANES), axis=-1, keepdims=True)
    e2 = jnp.exp(m2 - m1)
    g1 = 1.0 / (1.0 + e2)
    g2 = e2 / (1.0 + e2)
    return jnp.where(lane == 0, g1,
                     jnp.where(lane == 1, g2,
                               jnp.where(lane == 2, i1.astype(F32),
                                         jnp.where(lane == 3, i2.astype(F32), 0.0))))


def _proj_out_kernel(*refs, n_experts, layout, first_sample_tile):
    na, nb, nh = layout
    tile = pl.program_id(0)
    a = _read_rows(refs[:na], first_sample_tile, tile)
    b = _read_rows(refs[na:na + nb], first_sample_tile, tile)
    h = _read_rows(refs[na + nb:na + nb + nh], first_sample_tile, tile)
    w_ref, gpost_ref, gnext_ref, *rest = refs[na + nb + nh:]
    half = a.shape[1]
    m = (jnp.dot(a, w_ref[:half, :], preferred_element_type=F32)
         + jnp.dot(b, w_ref[half:, :], preferred_element_type=F32))
    hn = h + _rms(m, gpost_ref[...])
    xn = _rms(hn, gnext_ref[...])
    xn_hi = xn.astype(BF16)
    if n_experts:
        wr_hi_ref, wr_lo_ref, hn_ref, xn_ref, route_ref = rest
        xn_lo = (xn - xn_hi.astype(F32)).astype(BF16)
        logits = (jnp.dot(xn_hi, wr_hi_ref[...], preferred_element_type=F32)
                  + jnp.dot(xn_lo, wr_hi_ref[...], preferred_element_type=F32)
                  + jnp.dot(xn_hi, wr_lo_ref[...], preferred_element_type=F32))
        route_ref[...] = _top2(logits, n_experts)
    else:
        hn_ref, xn_ref = rest
    hn_ref[...] = hn
    xn_ref[...] = xn_hi


def _proj_out(a_parts, b_parts, w, h_parts, g_post, g_next, tm, n_prompt_rows, w_router=None):
    m = sum(p.shape[0] for p in h_parts)
    d = h_parts[0].shape[1]
    half = a_parts[0].shape[1]
    n_experts = 0 if w_router is None else w_router.shape[1]
    first_sample_tile = n_prompt_rows // tm
    assert all(len(p) == 1 or p[0].shape[0] == n_prompt_rows for p in (a_parts, b_parts, h_parts))
    row = lambda i: (i, 0)
    fixed = lambda i: (0, 0)
    in_specs = (_row_specs(a_parts, tm) + _row_specs(b_parts, tm) + _row_specs(h_parts, tm)
                + [pl.BlockSpec((2 * half, d), fixed), pl.BlockSpec((1, d), fixed), pl.BlockSpec((1, d), fixed)])
    out_specs = [pl.BlockSpec((tm, d), row), pl.BlockSpec((tm, d), row)]
    out_shape = [jax.ShapeDtypeStruct((m, d), F32), jax.ShapeDtypeStruct((m, d), BF16)]
    args = [*a_parts, *b_parts, *h_parts, w, g_post.reshape(1, d), g_next.reshape(1, d)]
    if n_experts:
        wr = jnp.zeros((d, LANES), F32).at[:, :n_experts].set(w_router.astype(F32))
        wr_hi = wr.astype(BF16)
        wr_lo = (wr - wr_hi.astype(F32)).astype(BF16)
        in_specs += [pl.BlockSpec((d, LANES), fixed), pl.BlockSpec((d, LANES), fixed)]
        out_specs.append(pl.BlockSpec((tm, LANES), row))
        out_shape.append(jax.ShapeDtypeStruct((m, LANES), F32))
        args += [wr_hi, wr_lo]
    return pl.pallas_call(
        functools.partial(_proj_out_kernel, n_experts=n_experts, first_sample_tile=first_sample_tile,
                          layout=(len(a_parts), len(b_parts), len(h_parts))),
        grid=(m // tm,),
        in_specs=in_specs, out_specs=out_specs, out_shape=out_shape,
        compiler_params=_params("parallel"),
        name="proj_out",
    )(*args)


def _norm_residual_kernel(m_ref, h_ref, gpost_ref, gnext_ref, hn_ref, xn_ref):
    hn = h_ref[...] + _rms(m_ref[...], gpost_ref[...])
    hn_ref[...] = hn
    xn_ref[...] = _rms(hn, gnext_ref[...]).astype(xn_ref.dtype)


def _norm_residual(mix, h, g_post, g_next, tm):
    m, d = h.shape
    row = lambda i: (i, 0)
    fixed = lambda i: (0, 0)
    return pl.pallas_call(
        _norm_residual_kernel,
        grid=(m // tm,),
        in_specs=[pl.BlockSpec((tm, d), row), pl.BlockSpec((tm, d), row),
                  pl.BlockSpec((1, d), fixed), pl.BlockSpec((1, d), fixed)],
        out_specs=[pl.BlockSpec((tm, d), row), pl.BlockSpec((tm, d), row)],
        out_shape=[jax.ShapeDtypeStruct((m, d), F32), jax.ShapeDtypeStruct((m, d), BF16)],
        compiler_params=_params("parallel"),
        name="norm_residual",
    )(mix, h, g_post.reshape(1, d), g_next.reshape(1, d))


def _ring_step(te_ref, nv_ref, nxt_ref, wrap_ref, cnt_ref, copies, consume):
    j = pl.program_id(0)
    t = pl.program_id(1)

    @pl.when(jnp.logical_and(j == 0, t == 0))
    def _():
        cnt_ref[0] = 0
        for c in copies(te_ref[0], 0, 0):
            c.start()

    group_start = jnp.logical_or(t == 0, te_ref[t] != te_ref[jnp.maximum(t - 1, 0)])

    @pl.when(jnp.logical_and(group_start, t < nv_ref[0]))
    def _():
        slot = cnt_ref[0] % 2
        for c in copies(te_ref[t], j, slot):
            c.wait()
        consume(slot)
        j_next = j + wrap_ref[t]

        @pl.when(j_next < pl.num_programs(0))
        def _():
            for c in copies(nxt_ref[t], j_next, 1 - slot):
                c.start()
        cnt_ref[0] = cnt_ref[0] + 1


def _ffn_up_kernel(te_ref, src_ref, nv_ref, nxt_ref, wrap_ref, half_ref, x_ref, wg_hbm, wu_hbm, o_ref,
                   wbuf, wg_bf, wu_bf, sem, cnt_ref):
    t = pl.program_id(1)
    tf = wg_bf.shape[1]

    def copies(e, j, slot):
        cols = pl.ds(pl.multiple_of(j * tf, tf), tf)
        return (pltpu.make_async_copy(wg_hbm.at[e, :, cols], wbuf.at[slot, 0], sem.at[slot, 0]),
                pltpu.make_async_copy(wu_hbm.at[e, :, cols], wbuf.at[slot, 1], sem.at[slot, 1]))

    def consume(slot):
        wg_bf[...] = wbuf[slot, 0].astype(BF16)
        wu_bf[...] = wbuf[slot, 1].astype(BF16)

    _ring_step(te_ref, nv_ref, nxt_ref, wrap_ref, cnt_ref, copies, consume)

    def rows(r):
        x = x_ref[r, :]
        g = jnp.dot(x, wg_bf[...], preferred_element_type=F32)
        u = jnp.dot(x, wu_bf[...], preferred_element_type=F32)
        o_ref[r, :] = (g * jax.nn.sigmoid(g) * u).astype(o_ref.dtype)

    _tile_rows(t, nv_ref, half_ref, o_ref, rows)


def _tile_rows(t, nv_ref, half_ref, o_ref, rows):
    tm = o_ref.shape[0]
    valid = t < nv_ref[0]
    half = half_ref[t] == 1

    @pl.when(jnp.logical_and(valid, jnp.logical_not(half)))
    def _():
        rows(slice(None))

    @pl.when(jnp.logical_and(valid, half))
    def _():
        rows(slice(0, tm // 2))
        o_ref[tm // 2:, :] = jnp.zeros((tm - tm // 2, o_ref.shape[1]), o_ref.dtype)

    @pl.when(jnp.logical_not(valid))
    def _():
        o_ref[...] = jnp.zeros_like(o_ref)


def _ffn_up(x, w_gate, w_up, tiles, tm, tf):
    s, k = x.shape
    f = w_gate.shape[2]
    n_tiles = s // tm
    grid_spec = pltpu.PrefetchScalarGridSpec(
        num_scalar_prefetch=len(tiles),
        grid=(f // tf, n_tiles),
        in_specs=[pl.BlockSpec((tm, k), lambda j, t, te, src, *_: (src[t], 0)),
                  pl.BlockSpec(memory_space=pl.ANY), pl.BlockSpec(memory_space=pl.ANY)],
        out_specs=pl.BlockSpec((tm, tf), lambda j, t, *_: (t, j)),
        scratch_shapes=[pltpu.VMEM((2, 2, k, tf), F32), pltpu.VMEM((k, tf), BF16), pltpu.VMEM((k, tf), BF16),
                        pltpu.SemaphoreType.DMA((2, 2)), pltpu.SMEM((1,), I32)],
    )
    return pl.pallas_call(
        _ffn_up_kernel, grid_spec=grid_spec,
        out_shape=jax.ShapeDtypeStruct((s, f), BF16),
        compiler_params=_params("arbitrary", "arbitrary"),
        name="ffn_up",
    )(*tiles, x, w_gate, w_up)


def _ffn_down_kernel(te_ref, src_ref, nv_ref, nxt_ref, wrap_ref, half_ref, a_ref, wd_hbm, o_ref,
                     wbuf, wd_bf, sem, cnt_ref):
    t = pl.program_id(1)
    tn = wd_bf.shape[1]

    def copies(e, j, slot):
        cols = pl.ds(pl.multiple_of(j * tn, tn), tn)
        return (pltpu.make_async_copy(wd_hbm.at[e, :, cols], wbuf.at[slot], sem.at[slot]),)

    def consume(slot):
        wd_bf[...] = wbuf[slot].astype(BF16)

    _ring_step(te_ref, nv_ref, nxt_ref, wrap_ref, cnt_ref, copies, consume)

    def rows(r):
        o_ref[r, :] = jnp.dot(a_ref[r, :], wd_bf[...], preferred_element_type=F32)

    _tile_rows(t, nv_ref, half_ref, o_ref, rows)


def _ffn_down(a, w_down, tiles, tm, tn):
    s, f = a.shape
    d = w_down.shape[2]
    n_tiles = s // tm
    grid_spec = pltpu.PrefetchScalarGridSpec(
        num_scalar_prefetch=len(tiles),
        grid=(d // tn, n_tiles),
        in_specs=[pl.BlockSpec((tm, f), lambda j, t, te, src, *_: (src[t], 0)),
                  pl.BlockSpec(memory_space=pl.ANY)],
        out_specs=pl.BlockSpec((tm, tn), lambda j, t, *_: (t, j)),
        scratch_shapes=[pltpu.VMEM((2, f, tn), F32), pltpu.VMEM((f, tn), BF16),
                        pltpu.SemaphoreType.DMA((2,)), pltpu.SMEM((1,), I32)],
    )
    return pl.pallas_call(
        _ffn_down_kernel, grid_spec=grid_spec,
        out_shape=jax.ShapeDtypeStruct((s, d), F32),
        compiler_params=_params("arbitrary", "arbitrary"),
        name="ffn_down",
    )(*tiles, a, w_down)


def _swiglu_grouped(x, w_gate, w_up, w_down, tiles, tm):
    tf = _tile(w_gate.shape[2], 512, LANES)
    tn = _tile(w_down.shape[2], 512, LANES)
    act = _ffn_up(x, w_gate, w_up, tiles, tm, tf)
    return _ffn_down(act, w_down, tiles, tm, tn)


def _dense_tiles(n_tiles):
    zeros = jnp.zeros((n_tiles,), I32)
    return (zeros, jnp.arange(n_tiles, dtype=I32), jnp.full((1,), n_tiles, I32), zeros, jnp.ones((n_tiles,), I32),
            zeros)


GATHER_UNROLL = 8


def _row_copy(src_hbm, row, buf, slot, r, sem):
    return pltpu.make_async_copy(src_hbm.at[pl.ds(row, 1), :], buf.at[slot, pl.ds(r, 1), :], sem.at[slot])


def _wait_slot(src_hbm, buf, slot, sem):
    pltpu.make_async_copy(src_hbm.at[pl.ds(0, buf.shape[1]), :], buf.at[slot], sem.at[slot]).wait()


def _dispatch_kernel(tok_ref, h_hbm, g_ref, o_ref, buf, sem, *, tm):
    t = pl.program_id(0)
    n_t = pl.num_programs(0)

    def issue(tile, slot):
        def body(blk, c):
            for u in range(GATHER_UNROLL):
                r = blk * GATHER_UNROLL + u
                _row_copy(h_hbm, tok_ref[tile * tm + r], buf, slot, r, sem).start(priority=u % 2)
            return c
        lax.fori_loop(0, tm // GATHER_UNROLL, body, 0)

    @pl.when(t == 0)
    def _():
        issue(0, 0)

    @pl.when(t + 1 < n_t)
    def _():
        issue(t + 1, (t + 1) % 2)

    slot = t % 2
    _wait_slot(h_hbm, buf, slot, sem)
    o_ref[...] = _rms(buf[slot], g_ref[...]).astype(o_ref.dtype)


def _dispatch(h, g, slot_token, tm):
    s = slot_token.shape[0]
    d = h.shape[1]
    grid_spec = pltpu.PrefetchScalarGridSpec(
        num_scalar_prefetch=1,
        grid=(s // tm,),
        in_specs=[pl.BlockSpec(memory_space=pl.ANY), pl.BlockSpec((1, d), lambda t, tok: (0, 0))],
        out_specs=pl.BlockSpec((tm, d), lambda t, tok: (t, 0)),
        scratch_shapes=[pltpu.VMEM((2, tm, d), F32), pltpu.SemaphoreType.DMA((2,))],
    )
    return pl.pallas_call(
        functools.partial(_dispatch_kernel, tm=tm), grid_spec=grid_spec,
        out_shape=jax.ShapeDtypeStruct((s, d), BF16),
        compiler_params=_params("arbitrary"),
        name="moe_dispatch",
    )(slot_token, h, g.reshape(1, d))


def _combine_kernel(slot_ref, y_hbm, route_ref, h_ref, gpost_ref, *rest, tm, first_sample_tile):
    out_refs, (buf, sem) = rest[:-2], rest[-2:]
    t = pl.program_id(0)
    n_t = pl.num_programs(0)

    def issue(tile, slot):
        def body(blk, c):
            for u in range(GATHER_UNROLL // TOP_K):
                r = blk * (GATHER_UNROLL // TOP_K) + u
                for k in range(TOP_K):
                    _row_copy(y_hbm, slot_ref[k, tile * tm + r], buf, slot, k * tm + r, sem).start(priority=k % 2)
            return c
        lax.fori_loop(0, tm // (GATHER_UNROLL // TOP_K), body, 0)

    @pl.when(t == 0)
    def _():
        issue(0, 0)

    @pl.when(t + 1 < n_t)
    def _():
        issue(t + 1, (t + 1) % 2)

    slot = t % 2
    _wait_slot(y_hbm, buf, slot, sem)
    route = route_ref[...]
    mix = route[:, 0:1] * buf[slot, 0:tm, :]
    for k in range(1, TOP_K):
        mix = mix + route[:, k:k + 1] * buf[slot, k * tm:(k + 1) * tm, :]
    result = h_ref[...] + _rms(mix, gpost_ref[...])
    if len(out_refs) == 1:
        out_refs[0][...] = result
    else:
        @pl.when(t < first_sample_tile)
        def _():
            out_refs[0][...] = result

        @pl.when(t >= first_sample_tile)
        def _():
            out_refs[1][...] = result


def _combine(y_sorted, token_slots, route, h, g_post, tm, split_rows=None):
    m, d = h.shape
    if split_rows is None:
        first_sample_tile = 0
        out_specs = pl.BlockSpec((tm, d), lambda t, sl: (t, 0))
        out_shape = jax.ShapeDtypeStruct((m, d), F32)
    else:
        assert split_rows % tm == 0 and 0 < split_rows < m
        first_sample_tile = split_rows // tm
        out_specs = [pl.BlockSpec((tm, d), lambda t, sl: (jnp.minimum(t, first_sample_tile - 1), 0)),
                     pl.BlockSpec((tm, d), lambda t, sl: (jnp.maximum(t - first_sample_tile, 0), 0))]
        out_shape = [jax.ShapeDtypeStruct((split_rows, d), F32), jax.ShapeDtypeStruct((m - split_rows, d), F32)]
    grid_spec = pltpu.PrefetchScalarGridSpec(
        num_scalar_prefetch=1,
        grid=(m // tm,),
        in_specs=[pl.BlockSpec(memory_space=pl.ANY),
                  pl.BlockSpec((tm, LANES), lambda t, sl: (t, 0)),
                  pl.BlockSpec((tm, d), lambda t, sl: (t, 0)),
                  pl.BlockSpec((1, d), lambda t, sl: (0, 0))],
        out_specs=out_specs,
        scratch_shapes=[pltpu.VMEM((2, TOP_K * tm, d), F32), pltpu.SemaphoreType.DMA((2,))],
    )
    return pl.pallas_call(
        functools.partial(_combine_kernel, tm=tm, first_sample_tile=first_sample_tile), grid_spec=grid_spec,
        out_shape=out_shape,
        compiler_params=_params("arbitrary"),
        name="moe_combine",
    )(token_slots, y_sorted, route, h, g_post.reshape(1, d))


def _routing_tables(route, n_experts, tm):
    m = route.shape[0]
    eid = jnp.concatenate([route[:, 2 + k].astype(I32) for k in range(TOP_K)])
    blk = LANES
    assert (TOP_K * m) % blk == 0 and TOP_K * m < 2 ** 24
    onehot = (eid[:, None] == jnp.arange(n_experts, dtype=I32)[None, :]).astype(F32).reshape(-1, blk, n_experts)
    before = jnp.tril(jnp.ones((blk, blk), F32), -1)
    within = jnp.einsum("ts,bse->bte", before, onehot)
    block_total = jnp.sum(onehot, axis=1)
    block_start = jnp.cumsum(block_total, axis=0) - block_total
    rank = jnp.sum((within + block_start[:, None, :]) * onehot, axis=-1).reshape(-1).astype(I32)
    counts = jnp.sum(block_total, axis=0).astype(I32)
    tiles_e = (counts + tm - 1) // tm
    tile_end = jnp.cumsum(tiles_e)
    start = (tile_end - tiles_e) * tm
    slot = start[eid] + rank
    n_tiles = (TOP_K * m + n_experts * (tm - 1)) // tm
    token = jnp.tile(jnp.arange(m, dtype=I32), TOP_K)
    slot_token = (jnp.arange(n_tiles * tm, dtype=I32) % m).at[slot].set(token)
    n_valid = tile_end[-1].astype(I32)
    tile_id = jnp.arange(n_tiles, dtype=I32)
    tile_src = jnp.minimum(tile_id, n_valid - 1)
    tile_expert = jnp.minimum(jnp.sum((tile_end[None, :] <= tile_src[:, None]).astype(I32), axis=1), n_experts - 1)
    following, cur = [], jnp.int32(-1)
    for e in reversed(range(n_experts)):
        following.append(cur)
        cur = jnp.where(tiles_e[e] > 0, jnp.int32(e), cur)
    following = jnp.stack(following[::-1])
    wraps = following < 0
    following = jnp.where(wraps, cur, following)
    rows_in_last = counts - (tiles_e - 1) * tm
    is_last = tile_src == tile_end[tile_expert] - 1
    half = jnp.logical_and(is_last, rows_in_last[tile_expert] <= tm // 2).astype(I32)
    tiles = (tile_expert, tile_src, n_valid.reshape(1), following[tile_expert], wraps[tile_expert].astype(I32), half)
    return slot_token, slot.reshape(TOP_K, m).astype(I32), tiles


def _pool_prompt_kernel(halo_ref, p_ref, w_ref, s_ref, o_ref, buf_ref, *, tiles_per_seq):
    i = pl.program_id(0) % tiles_per_seq
    t = p_ref.shape[0]
    gw = w_ref.shape[1]
    buf_ref[0:POOL_HALO, :] = jnp.where(i == 0, 0.0, halo_ref[...])
    buf_ref[POOL_HALO:POOL_HALO + t, :] = p_ref[...]
    pos = i * t + lax.broadcasted_iota(I32, (t, 1), 0)
    for g, w in enumerate(POOL_WINDOWS):
        c0 = g * gw
        cur = buf_ref[POOL_HALO:POOL_HALO + t, c0:c0 + gw]
        acc = cur
        for j in range(1, w):
            acc = acc + buf_ref[POOL_HALO - j:POOL_HALO - j + t, c0:c0 + gw]
        cnt = jnp.minimum(pos + 1, w).astype(F32)
        d = acc / cnt - cur
        y = jnp.dot(d.astype(BF16), w_ref[g], preferred_element_type=F32)
        o_ref[:, c0:c0 + gw] = (y * s_ref[:, c0:c0 + gw]).astype(o_ref.dtype)


def _pool_prompt(proj, n_rows, seq, w_pool, scale, t):
    c = scale.shape[0]
    halo_per_tile = t // POOL_HALO
    return pl.pallas_call(
        functools.partial(_pool_prompt_kernel, tiles_per_seq=seq // t),
        grid=(n_rows // t,),
        in_specs=[pl.BlockSpec((POOL_HALO, c), lambda i: (jnp.maximum(i * halo_per_tile - 1, 0), 0)),
                  pl.BlockSpec((t, c), lambda i: (i, 0)),
                  pl.BlockSpec(w_pool.shape, lambda i: (0, 0, 0)),
                  pl.BlockSpec((1, c), lambda i: (0, 0))],
        out_specs=pl.BlockSpec((t, c), lambda i: (i, 0)),
        out_shape=jax.ShapeDtypeStruct((n_rows, c), BF16),
        scratch_shapes=[pltpu.VMEM((POOL_HALO + t, c), F32)],
        compiler_params=_params("parallel"),
        name="pool_prompt",
    )(proj, proj, w_pool, scale.reshape(1, c))


def _pool_sample_kernel(ctx_ref, w_ref, s_ref, o_ref, *, n_ctx, pos0):
    gw = w_ref.shape[1]
    for t in range(o_ref.shape[0]):
        hi = n_ctx + t + 1
        for g, w in enumerate(POOL_WINDOWS):
            c0 = g * gw
            lo = max(hi - w, 0)
            acc = ctx_ref[lo, :, c0:c0 + gw]
            for r in range(lo + 1, hi):
                acc = acc + ctx_ref[r, :, c0:c0 + gw]
            d = acc / float(min(pos0 + t + 1, w)) - ctx_ref[hi - 1, :, c0:c0 + gw]
            y = jnp.dot(d.astype(BF16), w_ref[g], preferred_element_type=F32)
            o_ref[t, :, c0:c0 + gw] = (y * s_ref[:, c0:c0 + gw]).astype(o_ref.dtype)


def _pool_sample(ctx_t, n_ctx, pos0, w_pool, scale):
    total, n, c = ctx_t.shape
    return pl.pallas_call(
        functools.partial(_pool_sample_kernel, n_ctx=n_ctx, pos0=pos0),
        out_shape=jax.ShapeDtypeStruct((total - n_ctx, n, c), BF16),
        compiler_params=pltpu.CompilerParams(vmem_limit_bytes=VMEM_LIMIT_BYTES),
        name="pool_sample",
    )(ctx_t, w_pool, scale.reshape(1, c))


def _gmlp_kernel(u_ref, v_ref, w_ref, b_ref, o_ref):
    t = u_ref.shape[0]
    n_heads = w_ref.shape[0]
    ck = w_ref.shape[1]
    row = lax.broadcasted_iota(I32, (ck, ck), 0)
    col = lax.broadcasted_iota(I32, (ck, ck), 1)
    for h in range(n_heads):
        c0 = h * HEAD_DIM
        wm = jnp.where(col <= row, w_ref[h], 0.0).astype(BF16)
        bias = b_ref[:, h:h + 1]
        for k in range(t // ck):
            r0 = k * ck
            v = v_ref[r0:r0 + ck, c0:c0 + HEAD_DIM].astype(BF16)
            mixed = jnp.dot(wm, v, preferred_element_type=F32) + bias
            o_ref[r0:r0 + ck, c0:c0 + HEAD_DIM] = (u_ref[r0:r0 + ck, c0:c0 + HEAD_DIM] * mixed).astype(o_ref.dtype)


def _gmlp(proj, n_prompt_rows, w_pair, b_pair, t):
    m = proj.shape[0]
    c = proj.shape[1] // 3
    n_heads = w_pair.shape[1]
    first_sample_tile = n_prompt_rows // t
    which = lambda i: jnp.where(i >= first_sample_tile, 1, 0)
    return pl.pallas_call(
        _gmlp_kernel,
        grid=(m // t,),
        in_specs=[pl.BlockSpec((t, c), lambda i: (i, 1)),
                  pl.BlockSpec((t, c), lambda i: (i, 2)),
                  pl.BlockSpec((None, n_heads, GMLP_CHUNK, GMLP_CHUNK), lambda i: (which(i), 0, 0, 0)),
                  pl.BlockSpec((None, GMLP_CHUNK, n_heads), lambda i: (which(i), 0, 0))],
        out_specs=pl.BlockSpec((t, c), lambda i: (i, 0)),
        out_shape=jax.ShapeDtypeStruct((m, c), BF16),
        compiler_params=_params("parallel"),
        name="gmlp",
    )(proj, proj, w_pair, b_pair)


def _conv_prompt_kernel(xc_ref, bg_ref, cg_ref, xch_ref, cgh_ref, w_ref, o_ref, tail_ref, buf_ref, *, tiles_per_seq):
    i = pl.program_id(0) % tiles_per_seq
    t = xc_ref.shape[0]
    z = cg_ref[...] * xc_ref[...]
    buf_ref[0:CONV_HALO, :] = jnp.where(i == 0, 0.0, cgh_ref[...] * xch_ref[...])
    buf_ref[CONV_HALO:CONV_HALO + t, :] = z
    y = w_ref[0:1, :] * buf_ref[CONV_HALO - 2:CONV_HALO - 2 + t, :]
    for j in range(1, CONV_WIDTH):
        y = y + w_ref[j:j + 1, :] * buf_ref[CONV_HALO - 2 + j:CONV_HALO - 2 + j + t, :]
    o_ref[...] = (bg_ref[...] * y).astype(o_ref.dtype)
    tail_ref[...] = z[t - CONV_HALO:, :]


def _conv_prompt(proj, n_rows, seq, conv_w, t):
    c = conv_w.shape[1]
    halo_per_tile = t // CONV_HALO
    halo_row = lambda i: jnp.maximum(i * halo_per_tile - 1, 0)
    return pl.pallas_call(
        functools.partial(_conv_prompt_kernel, tiles_per_seq=seq // t),
        grid=(n_rows // t,),
        in_specs=[pl.BlockSpec((t, c), lambda i: (i, 0)),
                  pl.BlockSpec((t, c), lambda i: (i, 1)),
                  pl.BlockSpec((t, c), lambda i: (i, 2)),
                  pl.BlockSpec((CONV_HALO, c), lambda i: (halo_row(i), 0)),
                  pl.BlockSpec((CONV_HALO, c), lambda i: (halo_row(i), 2)),
                  pl.BlockSpec((CONV_WIDTH, c), lambda i: (0, 0))],
        out_specs=[pl.BlockSpec((t, c), lambda i: (i, 0)),
                   pl.BlockSpec((CONV_HALO, c), lambda i: (i, 0))],
        out_shape=[jax.ShapeDtypeStruct((n_rows, c), BF16),
                   jax.ShapeDtypeStruct((n_rows // t * CONV_HALO, c), F32)],
        scratch_shapes=[pltpu.VMEM((CONV_HALO + t, c), F32)],
        compiler_params=_params("parallel"),
        name="conv_prompt",
    )(proj, proj, proj, proj, proj, conv_w)


def _conv_sample_kernel(ctx_ref, xc_ref, bg_ref, cg_ref, w_ref, o_ref, z_ref):
    n_ctx = ctx_ref.shape[0]
    length = xc_ref.shape[0]
    rows = [ctx_ref[r] for r in range(n_ctx)]
    for t in range(length):
        z = cg_ref[t] * xc_ref[t]
        z_ref[t] = z
        rows.append(z)
    for t in range(length):
        y = w_ref[0:1, :] * rows[t]
        for j in range(1, CONV_WIDTH):
            y = y + w_ref[j:j + 1, :] * rows[t + j]
        o_ref[t] = (bg_ref[t] * y).astype(o_ref.dtype)


def _conv_sample(ctx_t, xc_t, bg_t, cg_t, conv_w):
    length, n, c = xc_t.shape
    return pl.pallas_call(
        _conv_sample_kernel,
        out_shape=[jax.ShapeDtypeStruct((length, n, c), BF16), jax.ShapeDtypeStruct((length, n, c), F32)],
        compiler_params=pltpu.CompilerParams(vmem_limit_bytes=VMEM_LIMIT_BYTES),
        name="conv_sample",
    )(ctx_t, xc_t, bg_t, cg_t, conv_w)


def _log_sigmoid(x):
    return jnp.minimum(x, 0.0) - jnp.log1p(jnp.exp(-jnp.abs(x)))


def _logf_kernel(f_ref, b_ref, lf_ref, c_ref, carry_ref, *, blocks_per_seq):
    i = pl.program_id(0)
    blk = f_ref.shape[0]
    lf = _log_sigmoid(f_ref[...] + b_ref[...])
    lf_ref[...] = lf

    @pl.when(i % blocks_per_seq == 0)
    def _():
        carry_ref[...] = jnp.zeros_like(carry_ref)

    row = lax.broadcasted_iota(I32, (blk, blk), 0)
    col = lax.broadcasted_iota(I32, (blk, blk), 1)
    tri = jnp.where(col <= row, 1.0, 0.0).astype(F32)
    c = jnp.dot(tri, lf, precision=HIGHEST, preferred_element_type=F32) + carry_ref[0:1, :]
    c_ref[...] = c
    carry_ref[...] = jnp.broadcast_to(c[blk - 1:blk, :], carry_ref.shape)


def _logf_cumsum(f_logit, b_forget, seq, blk):
    m = f_logit.shape[0]
    return pl.pallas_call(
        functools.partial(_logf_kernel, blocks_per_seq=seq // blk),
        grid=(m // blk,),
        in_specs=[pl.BlockSpec((blk, LANES), lambda i: (i, 0)), pl.BlockSpec((1, LANES), lambda i: (0, 0))],
        out_specs=[pl.BlockSpec((blk, LANES), lambda i: (i, 0)), pl.BlockSpec((blk, LANES), lambda i: (i, 0))],
        out_shape=[jax.ShapeDtypeStruct((m, LANES), F32), jax.ShapeDtypeStruct((m, LANES), F32)],
        scratch_shapes=[pltpu.VMEM((8, LANES), F32)],
        compiler_params=_params("arbitrary"),
        name="logf_cumsum",
    )(f_logit, b_forget)


FLASH_HEADS = 2


def _flash_kernel(q_ref, k_ref, v_ref, cc_ref, cr_ref, o_ref, k_bf, v_bf, *, tq):
    hp = pl.program_id(1)
    qb = pl.program_id(2)

    @pl.when(qb == 0)
    def _():
        k_bf[...] = k_ref[...].astype(BF16)
        v_bf[...] = v_ref[...].astype(BF16)

    lane = lax.broadcasted_iota(I32, (tq, LANES), 1)
    heads = []
    for hh in range(FLASH_HEADS):
        cols = slice(hh * HEAD_DIM, (hh + 1) * HEAD_DIM)
        q = (q_ref[:, cols] * (1.0 / math.sqrt(HEAD_DIM))).astype(BF16)
        cq = jnp.sum(jnp.where(lane == hp * FLASH_HEADS + hh, cc_ref[...], 0.0), axis=-1, keepdims=True)
        heads.append((cols, q, cq))

    def scores(hh, j):
        cols, q, cq = heads[hh]
        start = pl.multiple_of(j * tq, tq)
        k = k_bf[pl.ds(start, tq), cols]
        v = v_bf[pl.ds(start, tq), cols]
        s = lax.dot_general(q, k, (((1,), (1,)), ((), ())), preferred_element_type=F32)
        ck = cr_ref[hh, :, pl.ds(start, tq)]
        return s + (cq - ck), v

    def update(carry, s, v):
        m, l, acc = carry
        m_new = jnp.maximum(m, jnp.max(s, axis=-1, keepdims=True))
        alpha = jnp.exp(m - m_new)
        p = jnp.exp(s - m_new)
        l = alpha * l + jnp.sum(p, axis=-1, keepdims=True)
        acc = alpha * acc + jnp.dot(p.astype(BF16), v, preferred_element_type=F32)
        return m_new, l, acc

    def body(j, carries):
        return tuple(update(carries[hh], *scores(hh, j)) for hh in range(FLASH_HEADS))

    init = (jnp.full((tq, 1), MASKED, F32), jnp.zeros((tq, 1), F32), jnp.zeros((tq, HEAD_DIM), F32))
    carries = lax.fori_loop(0, qb, body, (init,) * FLASH_HEADS)
    row = lax.broadcasted_iota(I32, (tq, tq), 0)
    col = lax.broadcasted_iota(I32, (tq, tq), 1)
    for hh in range(FLASH_HEADS):
        s, v = scores(hh, qb)
        _, l, acc = update(carries[hh], jnp.where(col <= row, s, MASKED), v)
        o_ref[:, heads[hh][0]] = (acc / l).astype(o_ref.dtype)


def _flash_prompt(proj, c_col, c_row, n_seq, seq, n_heads, tq):
    assert n_heads % FLASH_HEADS == 0
    pairs = n_heads // FLASH_HEADS
    width = FLASH_HEADS * HEAD_DIM
    q_blk, k_blk, v_blk = 3 * pairs, 4 * pairs, 5 * pairs
    nqb = seq // tq
    return pl.pallas_call(
        functools.partial(_flash_kernel, tq=tq),
        grid=(n_seq, pairs, nqb),
        in_specs=[pl.BlockSpec((tq, width), lambda n, h, i: (n * nqb + i, q_blk + h)),
                  pl.BlockSpec((seq, width), lambda n, h, i: (n, k_blk + h)),
                  pl.BlockSpec((seq, width), lambda n, h, i: (n, v_blk + h)),
                  pl.BlockSpec((tq, LANES), lambda n, h, i: (n * nqb + i, 0)),
                  pl.BlockSpec((FLASH_HEADS, 1, seq), lambda n, h, i: (h, 0, n))],
        out_specs=pl.BlockSpec((tq, width), lambda n, h, i: (n * nqb + i, h)),
        out_shape=jax.ShapeDtypeStruct((n_seq * seq, n_heads * HEAD_DIM), BF16),
        scratch_shapes=[pltpu.VMEM((seq, width), BF16), pltpu.VMEM((seq, width), BF16)],
        compiler_params=_params("arbitrary", "arbitrary", "arbitrary"),
        name="fox_prompt",
    )(proj, proj, proj, c_col, c_row)


def _decay_of_later_rows(lf, later_groups, n_heads):
    group, width = lf.shape
    lane = lax.broadcasted_iota(I32, lf.shape, 1)
    incl, total = lf, lf
    shift = n_heads
    while shift < width:
        incl = incl + jnp.where(lane + shift < width, pltpu.roll(incl, width - shift, 1), 0.0)
        total = total + pltpu.roll(total, shift, 1)
        shift *= 2
    later_pages = jnp.zeros_like(lf)
    if group > 1:
        prow = lax.broadcasted_iota(I32, (group, group), 0)
        pcol = lax.broadcasted_iota(I32, (group, group), 1)
        later = jnp.where(pcol > prow, 1.0, 0.0).astype(F32)
        later_pages = jnp.dot(later, total, precision=HIGHEST, preferred_element_type=F32)
    return (incl - lf) + later_pages + later_groups, jnp.sum(total, axis=0, keepdims=True)


def _decode_kernel(pt_ref, q_ref, kn_ref, vn_ref, lfn_ref, *refs, n_heads, n_q, group, n_par):
    pages = 3 * group * n_par
    o_ref, m_ref, l_ref, acc_ref, later_ref = refs[pages:]
    j = pl.program_id(1)
    rows = n_heads * n_q
    width = refs[0].shape[0] * n_heads

    @pl.when(j == 0)
    def _():
        m_ref[...] = jnp.full_like(m_ref, MASKED)
        l_ref[...] = jnp.zeros_like(l_ref)
        acc_ref[...] = jnp.zeros_like(acc_ref)
        later_ref[...] = jnp.zeros_like(later_ref)

    r_i = lax.broadcasted_iota(I32, (rows, LANES), 0)
    c_i = lax.broadcasted_iota(I32, (rows, LANES), 1)
    sel = (c_i % n_heads == r_i // n_q) & (c_i // n_heads <= r_i % n_q) & (c_i < n_q * n_heads)
    r_w = lax.broadcasted_iota(I32, (rows, group * width), 0)
    c_w = lax.broadcasted_iota(I32, (rows, group * width), 1)
    same_head = c_w % n_heads == r_w // n_q

    def update(state, s, v):
        m_old, l, acc = state
        m_new = jnp.maximum(m_old, jnp.max(s, axis=-1, keepdims=True))
        alpha = jnp.exp(m_old - m_new)
        p = jnp.exp(s - m_new)
        l = alpha * l + jnp.sum(p, axis=-1, keepdims=True)
        acc = alpha * acc + jnp.dot(p.astype(BF16), v, preferred_element_type=F32)
        return m_new, l, acc

    def load_state(b):
        return m_ref[b, :, 0:1], l_ref[b, :, 0:1], acc_ref[b]

    def store_state(b, state):
        m, l, acc = state
        m_ref[b] = jnp.broadcast_to(m, m_ref.shape[1:])
        l_ref[b] = jnp.broadcast_to(l, l_ref.shape[1:])
        acc_ref[b] = acc

    states = [load_state(b) for b in range(n_par)]
    later = [later_ref[b] for b in range(n_par)]
    queries = []
    for b in range(n_par):
        base = 3 * group * b
        k_refs, v_refs = refs[base:base + group], refs[base + group:base + 2 * group]
        lf_refs = refs[base + 2 * group:base + 3 * group]
        lf = jnp.concatenate([r[...] for r in lf_refs], axis=0)
        decay, group_total = _decay_of_later_rows(lf, later[b][0:1, :], n_heads)
        later[b] = later[b] + group_total

        q = (q_ref[b] * (1.0 / math.sqrt(HEAD_DIM))).astype(BF16)
        lfn = lfn_ref[b]
        cq = jnp.sum(jnp.where(sel, jnp.broadcast_to(lfn, (rows, LANES)), 0.0), axis=-1, keepdims=True)
        queries.append((q, lfn, cq))

        k = jnp.concatenate([r[...].reshape(width, HEAD_DIM).astype(BF16) for r in k_refs], axis=0)
        v = jnp.concatenate([r[...].reshape(width, HEAD_DIM).astype(BF16) for r in v_refs], axis=0)
        decay_row = jnp.concatenate([decay[p:p + 1, :] for p in range(group)], axis=1)
        s = lax.dot_general(q, k, (((1,), (1,)), ((), ())), preferred_element_type=F32)
        s = jnp.where(same_head, s + decay_row + cq, MASKED)
        states[b] = update(states[b], s, v)
    for b in range(n_par):
        store_state(b, states[b])
        later_ref[b] = later[b]

    @pl.when(j == pl.num_programs(1) - 1)
    def _():
        for b in range(n_par):
            q, lfn, cq = queries[b]
            kn = kn_ref[b].astype(BF16)
            vn = vn_ref[b].astype(BF16)
            sn = lax.dot_general(q, kn, (((1,), (1,)), ((), ())), preferred_element_type=F32)
            lf8 = jnp.broadcast_to(lfn, (8, LANES))
            lane8 = lax.broadcasted_iota(I32, (8, LANES), 1)
            csum = lf8
            shift = n_heads
            while shift < n_q * n_heads:
                csum = csum + jnp.where(lane8 >= shift, pltpu.roll(csum, shift, 1), 0.0)
                shift *= 2
            _, l, acc = update(states[b], jnp.where(sel, sn + (cq - csum[0:1, :]), MASKED), vn)
            o_ref[b] = acc / l


def _decode_attention(q_hq, cache_k, cache_v, logf_pages, layer, page_table, k_new, v_new, lf_new, n_heads, n_q):
    n, n_pages = page_table.shape
    assert n_pages >= 1
    page = cache_k.shape[2]
    rows = n_heads * n_q
    width = page * n_heads
    group = max(g for g in (1, 2, 4, 8) if n_pages % g == 0)
    n_groups = n_pages // group
    n_par = 2 if n % 2 == 0 else 1

    def physical_page(i, j, pt, b, p):
        return pt[(i * n_par + b) * n_pages + (n_groups - 1 - j) * group + p]

    def page_spec(b, p):
        return pl.BlockSpec((None, None, page, n_heads, HEAD_DIM),
                            lambda i, j, pt: (layer, physical_page(i, j, pt, b, p), 0, 0, 0))

    def logf_spec(b, p):
        return pl.BlockSpec((None, 1, width), lambda i, j, pt: (physical_page(i, j, pt, b, p), 0, 0))

    per_seq = lambda shape: pl.BlockSpec((n_par,) + shape, lambda i, j, pt: (i, 0, 0))
    page_specs, page_args = [], []
    for b in range(n_par):
        page_specs += [page_spec(b, p) for p in range(group)] * 2 + [logf_spec(b, p) for p in range(group)]
        page_args += [cache_k] * group + [cache_v] * group + [logf_pages] * group
    grid_spec = pltpu.PrefetchScalarGridSpec(
        num_scalar_prefetch=1,
        grid=(n // n_par, n_groups),
        in_specs=[per_seq((rows, HEAD_DIM)), per_seq((LANES, HEAD_DIM)), per_seq((LANES, HEAD_DIM)),
                  per_seq((1, LANES))] + page_specs,
        out_specs=per_seq((rows, HEAD_DIM)),
        scratch_shapes=[pltpu.VMEM((n_par, rows, LANES), F32), pltpu.VMEM((n_par, rows, LANES), F32),
                        pltpu.VMEM((n_par, rows, HEAD_DIM), F32), pltpu.VMEM((n_par, 8, width), F32)],
    )
    return pl.pallas_call(
        functools.partial(_decode_kernel, n_heads=n_heads, n_q=n_q, group=group, n_par=n_par),
        grid_spec=grid_spec,
        out_shape=jax.ShapeDtypeStruct((n, rows, HEAD_DIM), F32),
        compiler_params=_params("parallel", "arbitrary"),
        name="fox_decode",
    )(page_table.reshape(-1), q_hq, k_new, v_new, lf_new, *page_args)


def kernel(x_prompt, x_sample, state_pool, state_conv, cache_k, cache_v, cache_logf, page_table, e_norm_mix_pre, e_norm_mix_post, e_norm_ffn_pre, e_norm_ffn_post, e_w_in, e_w_pool, e_pool_scale, e_w_spatial, e_b_spatial, e_w_out, e_ffn_gate, e_ffn_up, e_ffn_down, o_norm_mix_pre, o_norm_mix_post, o_norm_ffn_pre, o_norm_ffn_post, o_w_in, o_conv_w, o_b_forget, o_w_out, o_w_router, o_exp_gate, o_exp_up, o_exp_down):
    n_p, seq, d = x_prompt.shape
    n_s, dec, _ = x_sample.shape
    mp, ms = n_p * seq, n_s * dec
    m = mp + ms
    c = d // 2
    n_heads = c // HEAD_DIM
    n_pages = page_table.shape[1]
    page = cache_k.shape[2]
    past_len = n_pages * page
    n_layers = e_w_in.shape[0] + o_w_in.shape[0]
    n_experts = o_w_router.shape[2]
    assert ms % GMLP_CHUNK == 0 and seq % GMLP_CHUNK == 0 and past_len % GMLP_CHUNK == 0
    assert GMLP_CHUNK % min(dec, GMLP_CHUNK) == 0 and dec % min(dec, GMLP_CHUNK) == 0
    assert dec * n_heads <= LANES and dec >= CONV_WIDTH - 1

    tm = _tile(math.gcd(mp, ms), 512, LANES)
    t_seq = _tile(math.gcd(seq, tm), 512, LANES)
    tq = _tile(seq, 512, LANES)
    tm_moe = 512
    dense_tiles = _dense_tiles(m // tm)

    h = (x_prompt.reshape(mp, d), x_sample.reshape(ms, d))
    xn = _norm_cast(h, e_norm_mix_pre[0], tm)
    out_rows = None

    pool_p, pool_s, gv_s, conv_p, conv_s = [], [], [], [], []
    kp_l, vp_l, lp_l, ks_l, vs_l, ls_l = [], [], [], [], [], []

    def time_major(x2d):
        return jnp.transpose(x2d.reshape(n_s, dec, -1), (1, 0, 2))

    def row_major(x3d):
        return jnp.transpose(x3d, (1, 0, 2)).reshape(ms, -1)

    for layer in range(n_layers):
        i = layer // 2
        if layer % 2 == 0:
            g_next = e_norm_ffn_pre[i]
            proj = _mm(xn, e_w_in, 3 * c, tm, _tile(3 * c, 1024, LANES), w_layer=i)
            p_s = proj[mp:, :c].reshape(n_s, dec, c)
            ctx = jnp.concatenate([state_pool[i].astype(F32), p_s], axis=1)
            n_ctx = state_pool.shape[2]
            w_pool = e_w_pool[i].astype(BF16)
            a_p = _pool_prompt(proj, mp, seq, w_pool, e_pool_scale[i], t_seq)
            a_s = row_major(_pool_sample(jnp.transpose(ctx, (1, 0, 2)), n_ctx, past_len, w_pool, e_pool_scale[i]))
            cl = min(dec, GMLP_CHUNK)
            reps = GMLP_CHUNK // cl
            w_samp = jnp.einsum("ab,hts->hatbs", jnp.eye(reps, dtype=F32),
                                e_w_spatial[i][:, :cl, :cl]).reshape(n_heads, GMLP_CHUNK, GMLP_CHUNK)
            b_samp = jnp.tile(e_b_spatial[i][:, :cl], (1, reps))
            w_pair = jnp.stack([e_w_spatial[i], w_samp])
            b_pair = jnp.stack([e_b_spatial[i].T, b_samp.T])
            b = _gmlp(proj, mp, w_pair, b_pair, tm)
            h, xn = _proj_out((a_p, a_s), (b,), e_w_out[i].astype(BF16), h, e_norm_mix_post[i], g_next, tm, mp)
            g_after = o_norm_mix_pre[i] if layer + 1 < n_layers else jnp.ones((d,), F32)
            mix = _swiglu_grouped(xn, e_ffn_gate[i][None], e_ffn_up[i][None], e_ffn_down[i][None], dense_tiles, tm)
            h, xn = _norm_residual(mix, h, e_norm_ffn_post[i], g_after, tm)
            h = (h,)
            pool_p.append(jnp.stack([proj[(s + 1) * seq - n_ctx:(s + 1) * seq, :c] for s in range(n_p)]))
            pool_s.append(ctx[:, ctx.shape[1] - n_ctx:])
            gv_s.append(proj[mp:, 2 * c:].reshape(n_s, dec, c))
        else:
            w_in = o_w_in[i]
            proj, k_p, v_p = _mm(xn, o_w_in, 6 * c, tm, c, copy_cols=(4, 5), copy_rows=mp, w_layer=i)
            w_f = jnp.zeros((d, LANES), F32).at[:, :n_heads].set(w_in[:, 6 * c:])
            f_logit = _mm(xn, w_f, LANES, tm, LANES)
            b_f = jnp.zeros((1, LANES), F32).at[0, :n_heads].set(o_b_forget[i])
            logf, csum = _logf_cumsum(f_logit, b_f, seq, _tile(math.gcd(seq, m), 512, LANES))
            c_p, z_tail = _conv_prompt(proj, mp, seq, o_conv_w[i], t_seq)
            z_tail = z_tail.reshape(n_p, seq // t_seq, CONV_HALO, c)[:, -1, CONV_HALO - (CONV_WIDTH - 1):]
            xs = proj[mp:]
            conv_ctx = jnp.transpose(state_conv[i].astype(F32), (1, 0, 2))
            c_s, z_s = _conv_sample(conv_ctx, time_major(xs[:, :c]), time_major(xs[:, c:2 * c]),
                                    time_major(xs[:, 2 * c:3 * c]), o_conv_w[i])
            z_all = jnp.concatenate([conv_ctx, z_s], axis=0)
            c_row = csum[:mp, :n_heads].T.reshape(n_heads, 1, mp)
            att_p = _flash_prompt(proj, csum, c_row, n_p, seq, n_heads, tq)
            k_s = xs[:, 4 * c:5 * c]
            v_s = xs[:, 5 * c:6 * c]
            q_hq = jnp.transpose(xs[:, 3 * c:4 * c].reshape(n_s, dec, n_heads, HEAD_DIM), (0, 2, 1, 3))
            q_hq = q_hq.reshape(n_s, n_heads * dec, HEAD_DIM)
            pad_rows = LANES - dec * n_heads
            k_new = jnp.pad(k_s.reshape(n_s, dec * n_heads, HEAD_DIM), ((0, 0), (0, pad_rows), (0, 0)))
            v_new = jnp.pad(v_s.reshape(n_s, dec * n_heads, HEAD_DIM), ((0, 0), (0, pad_rows), (0, 0)))
            lf_new = jnp.pad(logf[mp:, :n_heads].reshape(n_s, 1, dec * n_heads), ((0, 0), (0, 0), (0, pad_rows)))
            logf_pages = cache_logf[i].astype(F32).reshape(cache_logf.shape[1], 1, page * n_heads)
            att_s = _decode_attention(q_hq, cache_k, cache_v, logf_pages, i, page_table, k_new, v_new, lf_new,
                                      n_heads, dec)
            att_s = jnp.transpose(att_s.reshape(n_s, n_heads, dec, HEAD_DIM), (0, 2, 1, 3)).reshape(ms, c)
            h, _, route = _proj_out((c_p, row_major(c_s)), (att_p, att_s.astype(BF16)), o_w_out[i].astype(BF16), h,
                                    o_norm_mix_post[i], o_norm_ffn_pre[i], tm, mp, w_router=o_w_router[i])
            slot_token, token_slots, expert_tiles = _routing_tables(route, n_experts, tm_moe)
            x_sorted = _dispatch(h, o_norm_ffn_pre[i], slot_token, tm_moe)
            y_sorted = _swiglu_grouped(x_sorted, o_exp_gate[i], o_exp_up[i], o_exp_down[i], expert_tiles, tm_moe)
            t_comb = _tile(math.gcd(mp, ms), 256)
            if layer + 1 < n_layers:
                h = (_combine(y_sorted, token_slots, route, h, o_norm_ffn_post[i], t_comb),)
                xn = _norm_cast(h, e_norm_mix_pre[i + 1], tm)
            else:
                out_rows = _combine(y_sorted, token_slots, route, h, o_norm_ffn_post[i], t_comb, split_rows=mp)
            conv_p.append(z_tail)
            conv_s.append(jnp.transpose(z_all[z_all.shape[0] - (CONV_WIDTH - 1):], (1, 0, 2)))
            kp_l.append(k_p.reshape(n_p, seq, n_heads, HEAD_DIM))
            vp_l.append(v_p.reshape(n_p, seq, n_heads, HEAD_DIM))
            lp_l.append(logf[:mp, :n_heads].reshape(n_p, seq, n_heads))
            ks_l.append(k_s.reshape(n_s, dec, n_heads, HEAD_DIM))
            vs_l.append(v_s.reshape(n_s, dec, n_heads, HEAD_DIM))
            ls_l.append(logf[mp:, :n_heads].reshape(n_s, dec, n_heads))

    if out_rows is None:
        out_rows = (h[0][:mp], h[0][mp:])
    return (out_rows[0].reshape(n_p, seq, d), out_rows[1].reshape(n_s, dec, d),
            jnp.stack(pool_p), jnp.stack(pool_s), jnp.stack(gv_s), jnp.stack(conv_p), jnp.stack(conv_s),
            jnp.stack(kp_l), jnp.stack(vp_l), jnp.stack(lp_l), jnp.stack(ks_l), jnp.stack(vs_l), jnp.stack(ls_l))
```

```python
import functools
import math

import jax
import jax.numpy as jnp
from jax import lax
from jax.experimental import pallas as pl
from jax.experimental.pallas import tpu as pltpu

F32, BF16, I32 = jnp.float32, jnp.bfloat16, jnp.int32
RMS_EPS = 1e-6
POOL_WINDOWS = (2, 4, 8, 16)
POOL_HALO = 16
CONV_WIDTH = 3
CONV_HALO = 8
HEAD_DIM = 128
GMLP_CHUNK = 128
TOP_K = 2
LANES = 128
MASKED = -1e30
VMEM_LIMIT_BYTES = 56 * 1024 * 1024
HIGHEST = lax.Precision.HIGHEST


def _params(*semantics):
    return pltpu.CompilerParams(dimension_semantics=semantics, vmem_limit_bytes=VMEM_LIMIT_BYTES)


def _tile(n, pref, mult=8):
    t = min(n, pref)
    t -= t % mult
    while t > mult and n % t:
        t -= mult
    assert t > 0 and n % t == 0, (n, pref, mult)
    return t


def _rms(x, g):
    return x * lax.rsqrt(jnp.mean(x * x, axis=-1, keepdims=True) + RMS_EPS) * g


def _row_specs(parts, tm):
    if len(parts) == 1:
        return [pl.BlockSpec((tm, parts[0].shape[1]), lambda i, *_: (i, 0))]
    first_sample_tile = parts[0].shape[0] // tm
    return [pl.BlockSpec((tm, parts[0].shape[1]), lambda i, *_: (jnp.minimum(i, first_sample_tile - 1), 0)),
            pl.BlockSpec((tm, parts[1].shape[1]), lambda i, *_: (jnp.maximum(i - first_sample_tile, 0), 0))]


def _read_rows(refs, first_sample_tile, tile):
    if len(refs) == 1:
        return refs[0][...]
    return jnp.where(tile < first_sample_tile, refs[0][...], refs[1][...])


def _norm_cast_kernel(*refs, first_sample_tile):
    g_ref, o_ref = refs[-2:]
    x = _read_rows(refs[:-2], first_sample_tile, pl.program_id(0))
    o_ref[...] = _rms(x, g_ref[...]).astype(o_ref.dtype)


def _norm_cast(x_parts, g, tm):
    m = sum(p.shape[0] for p in x_parts)
    d = x_parts[0].shape[1]
    return pl.pallas_call(
        functools.partial(_norm_cast_kernel, first_sample_tile=x_parts[0].shape[0] // tm),
        grid=(m // tm,),
        in_specs=_row_specs(x_parts, tm) + [pl.BlockSpec((1, d), lambda i: (0, 0))],
        out_specs=pl.BlockSpec((tm, d), lambda i: (i, 0)),
        out_shape=jax.ShapeDtypeStruct((m, d), BF16),
        compiler_params=_params("parallel"),
        name="norm_cast",
    )(*x_parts, g.reshape(1, d))


def _mm_kernel(x_ref, w_ref, o_ref, *rest, copy_cols, copy_tiles):
    copy_refs, w_bf = rest[:-1], rest[-1]

    @pl.when(pl.program_id(1) == 0)
    def _():
        w_bf[...] = w_ref[...].astype(BF16)

    y = jnp.dot(x_ref[...], w_bf[...], preferred_element_type=F32)
    o_ref[...] = y
    for ref, col in zip(copy_refs, copy_cols):
        @pl.when(jnp.logical_and(pl.program_id(0) == col, pl.program_id(1) < copy_tiles))
        def _(ref=ref):
            ref[...] = y


def _mm(x, w, n_cols, tm, tn, copy_cols=(), copy_rows=0, w_layer=0):
    m, k = x.shape
    n = n_cols
    copy_tiles = copy_rows // tm
    assert copy_rows % tm == 0 and n % tn == 0
    if w.ndim == 3:
        assert n <= w.shape[2]
        w_spec = pl.BlockSpec((None, k, tn), lambda j, i: (w_layer, 0, j))
    else:
        assert n <= w.shape[1]
        w_spec = pl.BlockSpec((k, tn), lambda j, i: (0, j))

    def copy_spec(col):
        def index(j, i):
            row = jnp.where(j < col, 0, jnp.where(j > col, copy_tiles - 1, jnp.minimum(i, copy_tiles - 1)))
            return (row, 0)
        return pl.BlockSpec((tm, tn), index)

    out = pl.pallas_call(
        functools.partial(_mm_kernel, copy_cols=tuple(copy_cols), copy_tiles=copy_tiles),
        grid=(n // tn, m // tm),
        in_specs=[pl.BlockSpec((tm, k), lambda j, i: (i, 0)), w_spec],
        out_specs=[pl.BlockSpec((tm, tn), lambda j, i: (i, j))] + [copy_spec(col) for col in copy_cols],
        out_shape=[jax.ShapeDtypeStruct((m, n), F32)] + [jax.ShapeDtypeStruct((copy_rows, tn), F32)] * len(copy_cols),
        scratch_shapes=[pltpu.VMEM((k, tn), BF16)],
        compiler_params=_params("arbitrary", "arbitrary"),
        name="proj_in",
    )(x, w)
    return out if copy_cols else out[0]


def _top2(logits, n_experts):
    lane = lax.broadcasted_iota(I32, logits.shape, 1)
    neg_inf = jnp.float32(-jnp.inf)
    l1 = jnp.where(lane < n_experts, logits, neg_inf)
    m1 = jnp.max(l1, axis=-1, keepdims=True)
    i1 = jnp.min(jnp.where(l1 == m1, lane, LANES), axis=-1, keepdims=True)
    l2 = jnp.where(lane == i1, neg_inf, l1)
    m2 = jnp.max(l2, axis=-1, keepdims=True)
    i2 = jnp.min(jnp.where(l2 == m2, lane, LANES), axis=-1, keepdims=True)
    e2 = jnp.exp(m2 - m1)
    g1 = 1.0 / (1.0 + e2)
    g2 = e2 / (1.0 + e2)
    return jnp.where(lane == 0, g1,
                     jnp.where(lane == 1, g2,
                               jnp.where(lane == 2, i1.astype(F32),
                                         jnp.where(lane == 3, i2.astype(F32), 0.0))))


def _proj_out_kernel(*refs, n_experts, layout, first_sample_tile):
    na, nb, nh = layout
    tile = pl.program_id(0)
    a = _read_rows(refs[:na], first_sample_tile, tile)
    b = _read_rows(refs[na:na + nb], first_sample_tile, tile)
    h = _read_rows(refs[na + nb:na + nb + nh], first_sample_tile, tile)
    w_ref, gpost_ref, gnext_ref, *rest = refs[na + nb + nh:]
    half = a.shape[1]
    m = (jnp.dot(a, w_ref[:half, :], preferred_element_type=F32)
         + jnp.dot(b, w_ref[half:, :], preferred_element_type=F32))
    hn = h + _rms(m, gpost_ref[...])
    xn = _rms(hn, gnext_ref[...])
    xn_hi = xn.astype(BF16)
    if n_experts:
        wr_ref, hn_ref, xn_ref, route_ref = rest
        xn_lo = (xn - xn_hi.astype(F32)).astype(BF16)
        both = jnp.dot(xn_hi, wr_ref[...], preferred_element_type=F32)
        logits = (both[:, :LANES] + jnp.dot(xn_lo, wr_ref[:, :LANES], preferred_element_type=F32)
                  + both[:, LANES:])
        route_ref[...] = _top2(logits, n_experts)
    else:
        hn_ref, xn_ref = rest
    hn_ref[...] = hn
    xn_ref[...] = xn_hi


def _proj_out(a_parts, b_parts, w, h_parts, g_post, g_next, tm, n_prompt_rows, w_router=None):
    m = sum(p.shape[0] for p in h_parts)
    d = h_parts[0].shape[1]
    half = a_parts[0].shape[1]
    n_experts = 0 if w_router is None else w_router.shape[1]
    first_sample_tile = n_prompt_rows // tm
    assert all(len(p) == 1 or p[0].shape[0] == n_prompt_rows for p in (a_parts, b_parts, h_parts))
    row = lambda i: (i, 0)
    fixed = lambda i: (0, 0)
    in_specs = (_row_specs(a_parts, tm) + _row_specs(b_parts, tm) + _row_specs(h_parts, tm)
                + [pl.BlockSpec((2 * half, d), fixed), pl.BlockSpec((1, d), fixed), pl.BlockSpec((1, d), fixed)])
    out_specs = [pl.BlockSpec((tm, d), row), pl.BlockSpec((tm, d), row)]
    out_shape = [jax.ShapeDtypeStruct((m, d), F32), jax.ShapeDtypeStruct((m, d), BF16)]
    args = [*a_parts, *b_parts, *h_parts, w, g_post.reshape(1, d), g_next.reshape(1, d)]
    if n_experts:
        wr = jnp.zeros((d, LANES), F32).at[:, :n_experts].set(w_router.astype(F32))
        wr_hi = wr.astype(BF16)
        wr_lo = (wr - wr_hi.astype(F32)).astype(BF16)
        in_specs.append(pl.BlockSpec((d, 2 * LANES), fixed))
        out_specs.append(pl.BlockSpec((tm, LANES), row))
        out_shape.append(jax.ShapeDtypeStruct((m, LANES), F32))
        args.append(jnp.concatenate([wr_hi, wr_lo], axis=1))
    return pl.pallas_call(
        functools.partial(_proj_out_kernel, n_experts=n_experts, first_sample_tile=first_sample_tile,
                          layout=(len(a_parts), len(b_parts), len(h_parts))),
        grid=(m // tm,),
        in_specs=in_specs, out_specs=out_specs, out_shape=out_shape,
        compiler_params=_params("parallel"),
        name="proj_out",
    )(*args)


def _norm_residual_kernel(m_ref, h_ref, gpost_ref, gnext_ref, hn_ref, xn_ref):
    hn = h_ref[...] + _rms(m_ref[...], gpost_ref[...])
    hn_ref[...] = hn
    xn_ref[...] = _rms(hn, gnext_ref[...]).astype(xn_ref.dtype)


def _norm_residual(mix, h, g_post, g_next, tm):
    m, d = h.shape
    row = lambda i: (i, 0)
    fixed = lambda i: (0, 0)
    return pl.pallas_call(
        _norm_residual_kernel,
        grid=(m // tm,),
        in_specs=[pl.BlockSpec((tm, d), row), pl.BlockSpec((tm, d), row),
                  pl.BlockSpec((1, d), fixed), pl.BlockSpec((1, d), fixed)],
        out_specs=[pl.BlockSpec((tm, d), row), pl.BlockSpec((tm, d), row)],
        out_shape=[jax.ShapeDtypeStruct((m, d), F32), jax.ShapeDtypeStruct((m, d), BF16)],
        compiler_params=_params("parallel"),
        name="norm_residual",
    )(mix, h, g_post.reshape(1, d), g_next.reshape(1, d))


def _ring_step(te_ref, nv_ref, nxt_ref, wrap_ref, cnt_ref, copies, consume):
    j = pl.program_id(0)
    t = pl.program_id(1)

    @pl.when(jnp.logical_and(j == 0, t == 0))
    def _():
        cnt_ref[0] = 0
        for c in copies(te_ref[0], 0, 0):
            c.start()

    group_start = jnp.logical_or(t == 0, te_ref[t] != te_ref[jnp.maximum(t - 1, 0)])

    @pl.when(jnp.logical_and(group_start, t < nv_ref[0]))
    def _():
        slot = cnt_ref[0] % 2
        for c in copies(te_ref[t], j, slot):
            c.wait()
        consume(slot)
        j_next = j + wrap_ref[t]

        @pl.when(j_next < pl.num_programs(0))
        def _():
            for c in copies(nxt_ref[t], j_next, 1 - slot):
                c.start()
        cnt_ref[0] = cnt_ref[0] + 1


def _ffn_up_kernel(te_ref, src_ref, nv_ref, nxt_ref, wrap_ref, half_ref, x_ref, wg_hbm, wu_hbm, o_ref,
                   wbuf, wg_bf, wu_bf, sem, cnt_ref):
    t = pl.program_id(1)
    tf = wg_bf.shape[1]

    def copies(e, j, slot):
        cols = pl.ds(pl.multiple_of(j * tf, tf), tf)
        return (pltpu.make_async_copy(wg_hbm.at[e, :, cols], wbuf.at[slot, 0], sem.at[slot, 0]),
                pltpu.make_async_copy(wu_hbm.at[e, :, cols], wbuf.at[slot, 1], sem.at[slot, 1]))

    def consume(slot):
        wg_bf[...] = wbuf[slot, 0].astype(BF16)
        wu_bf[...] = wbuf[slot, 1].astype(BF16)

    _ring_step(te_ref, nv_ref, nxt_ref, wrap_ref, cnt_ref, copies, consume)

    def rows(r):
        x = x_ref[r, :]
        g = jnp.dot(x, wg_bf[...], preferred_element_type=F32)
        u = jnp.dot(x, wu_bf[...], preferred_element_type=F32)
        o_ref[r, :] = (g * jax.nn.sigmoid(g) * u).astype(o_ref.dtype)

    _tile_rows(t, nv_ref, half_ref, o_ref, rows)


def _tile_rows(t, nv_ref, half_ref, o_ref, rows):
    tm = o_ref.shape[0]
    valid = t < nv_ref[0]
    half = half_ref[t] == 1

    @pl.when(jnp.logical_and(valid, jnp.logical_not(half)))
    def _():
        rows(slice(None))

    @pl.when(jnp.logical_and(valid, half))
    def _():
        rows(slice(0, tm // 2))
        o_ref[tm // 2:, :] = jnp.zeros((tm - tm // 2, o_ref.shape[1]), o_ref.dtype)

    @pl.when(jnp.logical_not(valid))
    def _():
        o_ref[...] = jnp.zeros_like(o_ref)


def _ffn_up(x, w_gate, w_up, tiles, tm, tf):
    s, k = x.shape
    f = w_gate.shape[2]
    n_tiles = s // tm
    grid_spec = pltpu.PrefetchScalarGridSpec(
        num_scalar_prefetch=len(tiles),
        grid=(f // tf, n_tiles),
        in_specs=[pl.BlockSpec((tm, k), lambda j, t, te, src, *_: (src[t], 0)),
                  pl.BlockSpec(memory_space=pl.ANY), pl.BlockSpec(memory_space=pl.ANY)],
        out_specs=pl.BlockSpec((tm, tf), lambda j, t, *_: (t, j)),
        scratch_shapes=[pltpu.VMEM((2, 2, k, tf), F32), pltpu.VMEM((k, tf), BF16), pltpu.VMEM((k, tf), BF16),
                        pltpu.SemaphoreType.DMA((2, 2)), pltpu.SMEM((1,), I32)],
    )
    return pl.pallas_call(
        _ffn_up_kernel, grid_spec=grid_spec,
        out_shape=jax.ShapeDtypeStruct((s, f), BF16),
        compiler_params=_params("arbitrary", "arbitrary"),
        name="ffn_up",
    )(*tiles, x, w_gate, w_up)


def _ffn_down_kernel(te_ref, src_ref, nv_ref, nxt_ref, wrap_ref, half_ref, a_ref, wd_hbm, o_ref,
                     wbuf, wd_bf, sem, cnt_ref):
    t = pl.program_id(1)
    tn = wd_bf.shape[1]

    def copies(e, j, slot):
        cols = pl.ds(pl.multiple_of(j * tn, tn), tn)
        return (pltpu.make_async_copy(wd_hbm.at[e, :, cols], wbuf.at[slot], sem.at[slot]),)

    def consume(slot):
        wd_bf[...] = wbuf[slot].astype(BF16)

    _ring_step(te_ref, nv_ref, nxt_ref, wrap_ref, cnt_ref, copies, consume)

    def rows(r):
        o_ref[r, :] = jnp.dot(a_ref[r, :], wd_bf[...], preferred_element_type=F32)

    _tile_rows(t, nv_ref, half_ref, o_ref, rows)


def _ffn_down(a, w_down, tiles, tm, tn):
    s, f = a.shape
    d = w_down.shape[2]
    n_tiles = s // tm
    grid_spec = pltpu.PrefetchScalarGridSpec(
        num_scalar_prefetch=len(tiles),
        grid=(d // tn, n_tiles),
        in_specs=[pl.BlockSpec((tm, f), lambda j, t, te, src, *_: (src[t], 0)),
                  pl.BlockSpec(memory_space=pl.ANY)],
        out_specs=pl.BlockSpec((tm, tn), lambda j, t, *_: (t, j)),
        scratch_shapes=[pltpu.VMEM((2, f, tn), F32), pltpu.VMEM((f, tn), BF16),
                        pltpu.SemaphoreType.DMA((2,)), pltpu.SMEM((1,), I32)],
    )
    return pl.pallas_call(
        _ffn_down_kernel, grid_spec=grid_spec,
        out_shape=jax.ShapeDtypeStruct((s, d), F32),
        compiler_params=_params("arbitrary", "arbitrary"),
        name="ffn_down",
    )(*tiles, a, w_down)


def _swiglu_grouped(x, w_gate, w_up, w_down, tiles, tm):
    tf = _tile(w_gate.shape[2], 512, LANES)
    tn = _tile(w_down.shape[2], 512, LANES)
    act = _ffn_up(x, w_gate, w_up, tiles, tm, tf)
    return _ffn_down(act, w_down, tiles, tm, tn)


def _dense_tiles(n_tiles):
    zeros = jnp.zeros((n_tiles,), I32)
    return (zeros, jnp.arange(n_tiles, dtype=I32), jnp.full((1,), n_tiles, I32), zeros, jnp.ones((n_tiles,), I32),
            zeros)


GATHER_UNROLL = 8


def _row_copy(src_hbm, row, buf, slot, r, sem):
    return pltpu.make_async_copy(src_hbm.at[pl.ds(row, 1), :], buf.at[slot, pl.ds(r, 1), :], sem.at[slot])


def _wait_slot(src_hbm, buf, slot, sem):
    pltpu.make_async_copy(src_hbm.at[pl.ds(0, buf.shape[1]), :], buf.at[slot], sem.at[slot]).wait()


def _dispatch_kernel(tok_ref, h_hbm, g_ref, o_ref, buf, sem, *, tm):
    t = pl.program_id(0)
    n_t = pl.num_programs(0)

    def issue(tile, slot):
        def body(blk, c):
            for u in range(GATHER_UNROLL):
                r = blk * GATHER_UNROLL + u
                _row_copy(h_hbm, tok_ref[tile * tm + r], buf, slot, r, sem).start(priority=u % 2)
            return c
        lax.fori_loop(0, tm // GATHER_UNROLL, body, 0)

    @pl.when(t == 0)
    def _():
        issue(0, 0)

    @pl.when(t + 1 < n_t)
    def _():
        issue(t + 1, (t + 1) % 2)

    slot = t % 2
    _wait_slot(h_hbm, buf, slot, sem)
    o_ref[...] = _rms(buf[slot], g_ref[...]).astype(o_ref.dtype)


def _dispatch(h, g, slot_token, tm):
    s = slot_token.shape[0]
    d = h.shape[1]
    grid_spec = pltpu.PrefetchScalarGridSpec(
        num_scalar_prefetch=1,
        grid=(s // tm,),
        in_specs=[pl.BlockSpec(memory_space=pl.ANY), pl.BlockSpec((1, d), lambda t, tok: (0, 0))],
        out_specs=pl.BlockSpec((tm, d), lambda t, tok: (t, 0)),
        scratch_shapes=[pltpu.VMEM((2, tm, d), F32), pltpu.SemaphoreType.DMA((2,))],
    )
    return pl.pallas_call(
        functools.partial(_dispatch_kernel, tm=tm), grid_spec=grid_spec,
        out_shape=jax.ShapeDtypeStruct((s, d), BF16),
        compiler_params=_params("arbitrary"),
        name="moe_dispatch",
    )(slot_token, h, g.reshape(1, d))


def _combine_kernel(slot_ref, y_hbm, route_ref, h_ref, gpost_ref, *rest, tm, first_sample_tile):
    out_refs, (buf, sem) = rest[:-2], rest[-2:]
    t = pl.program_id(0)
    n_t = pl.num_programs(0)

    def issue(tile, slot):
        def body(blk, c):
            for u in range(GATHER_UNROLL // TOP_K):
                r = blk * (GATHER_UNROLL // TOP_K) + u
                for k in range(TOP_K):
                    _row_copy(y_hbm, slot_ref[k, tile * tm + r], buf, slot, k * tm + r, sem).start(priority=k % 2)
            return c
        lax.fori_loop(0, tm // (GATHER_UNROLL // TOP_K), body, 0)

    @pl.when(t == 0)
    def _():
        issue(0, 0)

    @pl.when(t + 1 < n_t)
    def _():
        issue(t + 1, (t + 1) % 2)

    slot = t % 2
    _wait_slot(y_hbm, buf, slot, sem)
    route = route_ref[...]
    mix = route[:, 0:1] * buf[slot, 0:tm, :]
    for k in range(1, TOP_K):
        mix = mix + route[:, k:k + 1] * buf[slot, k * tm:(k + 1) * tm, :]
    result = h_ref[...] + _rms(mix, gpost_ref[...])
    if len(out_refs) == 1:
        out_refs[0][...] = result
    else:
        @pl.when(t < first_sample_tile)
        def _():
            out_refs[0][...] = result

        @pl.when(t >= first_sample_tile)
        def _():
            out_refs[1][...] = result


def _combine(y_sorted, token_slots, route, h, g_post, tm, split_rows=None):
    m, d = h.shape
    if split_rows is None:
        first_sample_tile = 0
        out_specs = pl.BlockSpec((tm, d), lambda t, sl: (t, 0))
        out_shape = jax.ShapeDtypeStruct((m, d), F32)
    else:
        assert split_rows % tm == 0 and 0 < split_rows < m
        first_sample_tile = split_rows // tm
        out_specs = [pl.BlockSpec((tm, d), lambda t, sl: (jnp.minimum(t, first_sample_tile - 1), 0)),
                     pl.BlockSpec((tm, d), lambda t, sl: (jnp.maximum(t - first_sample_tile, 0), 0))]
        out_shape = [jax.ShapeDtypeStruct((split_rows, d), F32), jax.ShapeDtypeStruct((m - split_rows, d), F32)]
    grid_spec = pltpu.PrefetchScalarGridSpec(
        num_scalar_prefetch=1,
        grid=(m // tm,),
        in_specs=[pl.BlockSpec(memory_space=pl.ANY),
                  pl.BlockSpec((tm, LANES), lambda t, sl: (t, 0)),
                  pl.BlockSpec((tm, d), lambda t, sl: (t, 0)),
                  pl.BlockSpec((1, d), lambda t, sl: (0, 0))],
        out_specs=out_specs,
        scratch_shapes=[pltpu.VMEM((2, TOP_K * tm, d), F32), pltpu.SemaphoreType.DMA((2,))],
    )
    return pl.pallas_call(
        functools.partial(_combine_kernel, tm=tm, first_sample_tile=first_sample_tile), grid_spec=grid_spec,
        out_shape=out_shape,
        compiler_params=_params("arbitrary"),
        name="moe_combine",
    )(token_slots, y_sorted, route, h, g_post.reshape(1, d))


def _routing_tables(route, n_experts, tm):
    m = route.shape[0]
    eid = jnp.concatenate([route[:, 2 + k].astype(I32) for k in range(TOP_K)])
    blk = LANES
    assert (TOP_K * m) % blk == 0 and TOP_K * m < 2 ** 24
    onehot = (eid[:, None] == jnp.arange(n_experts, dtype=I32)[None, :]).astype(F32).reshape(-1, blk, n_experts)
    before = jnp.tril(jnp.ones((blk, blk), F32), -1)
    within = jnp.einsum("ts,bse->bte", before, onehot)
    block_total = jnp.sum(onehot, axis=1)
    block_start = jnp.cumsum(block_total, axis=0) - block_total
    rank = jnp.sum((within + block_start[:, None, :]) * onehot, axis=-1).reshape(-1).astype(I32)
    counts = jnp.sum(block_total, axis=0).astype(I32)
    tiles_e = (counts + tm - 1) // tm
    tile_end = jnp.cumsum(tiles_e)
    start = (tile_end - tiles_e) * tm
    slot = start[eid] + rank
    n_tiles = (TOP_K * m + n_experts * (tm - 1)) // tm
    token = jnp.tile(jnp.arange(m, dtype=I32), TOP_K)
    slot_token = (jnp.arange(n_tiles * tm, dtype=I32) % m).at[slot].set(token)
    n_valid = tile_end[-1].astype(I32)
    tile_id = jnp.arange(n_tiles, dtype=I32)
    tile_src = jnp.minimum(tile_id, n_valid - 1)
    tile_expert = jnp.minimum(jnp.sum((tile_end[None, :] <= tile_src[:, None]).astype(I32), axis=1), n_experts - 1)
    following, cur = [], jnp.int32(-1)
    for e in reversed(range(n_experts)):
        following.append(cur)
        cur = jnp.where(tiles_e[e] > 0, jnp.int32(e), cur)
    following = jnp.stack(following[::-1])
    wraps = following < 0
    following = jnp.where(wraps, cur, following)
    rows_in_last = counts - (tiles_e - 1) * tm
    is_last = tile_src == tile_end[tile_expert] - 1
    half = jnp.logical_and(is_last, rows_in_last[tile_expert] <= tm // 2).astype(I32)
    tiles = (tile_expert, tile_src, n_valid.reshape(1), following[tile_expert], wraps[tile_expert].astype(I32), half)
    return slot_token, slot.reshape(TOP_K, m).astype(I32), tiles


def _pool_prompt_kernel(halo_ref, p_ref, w_ref, s_ref, o_ref, buf_ref, *, tiles_per_seq):
    i = pl.program_id(0) % tiles_per_seq
    t = p_ref.shape[0]
    gw = w_ref.shape[1]
    buf_ref[0:POOL_HALO, :] = jnp.where(i == 0, 0.0, halo_ref[...])
    buf_ref[POOL_HALO:POOL_HALO + t, :] = p_ref[...]
    pos = i * t + lax.broadcasted_iota(I32, (t, 1), 0)
    for g, w in enumerate(POOL_WINDOWS):
        c0 = g * gw
        cur = buf_ref[POOL_HALO:POOL_HALO + t, c0:c0 + gw]
        acc = cur
        for j in range(1, w):
            acc = acc + buf_ref[POOL_HALO - j:POOL_HALO - j + t, c0:c0 + gw]
        cnt = jnp.minimum(pos + 1, w).astype(F32)
        d = acc / cnt - cur
        y = jnp.dot(d.astype(BF16), w_ref[g], preferred_element_type=F32)
        o_ref[:, c0:c0 + gw] = (y * s_ref[:, c0:c0 + gw]).astype(o_ref.dtype)


def _pool_prompt(proj, n_rows, seq, w_pool, scale, t):
    c = scale.shape[0]
    halo_per_tile = t // POOL_HALO
    return pl.pallas_call(
        functools.partial(_pool_prompt_kernel, tiles_per_seq=seq // t),
        grid=(n_rows // t,),
        in_specs=[pl.BlockSpec((POOL_HALO, c), lambda i: (jnp.maximum(i * halo_per_tile - 1, 0), 0)),
                  pl.BlockSpec((t, c), lambda i: (i, 0)),
                  pl.BlockSpec(w_pool.shape, lambda i: (0, 0, 0)),
                  pl.BlockSpec((1, c), lambda i: (0, 0))],
        out_specs=pl.BlockSpec((t, c), lambda i: (i, 0)),
        out_shape=jax.ShapeDtypeStruct((n_rows, c), BF16),
        scratch_shapes=[pltpu.VMEM((POOL_HALO + t, c), F32)],
        compiler_params=_params("parallel"),
        name="pool_prompt",
    )(proj, proj, w_pool, scale.reshape(1, c))


def _pool_sample_kernel(ctx_ref, w_ref, s_ref, o_ref, *, n_ctx, pos0):
    gw = w_ref.shape[1]
    for t in range(o_ref.shape[0]):
        hi = n_ctx + t + 1
        for g, w in enumerate(POOL_WINDOWS):
            c0 = g * gw
            lo = max(hi - w, 0)
            acc = ctx_ref[lo, :, c0:c0 + gw]
            for r in range(lo + 1, hi):
                acc = acc + ctx_ref[r, :, c0:c0 + gw]
            d = acc / float(min(pos0 + t + 1, w)) - ctx_ref[hi - 1, :, c0:c0 + gw]
            y = jnp.dot(d.astype(BF16), w_ref[g], preferred_element_type=F32)
            o_ref[t, :, c0:c0 + gw] = (y * s_ref[:, c0:c0 + gw]).astype(o_ref.dtype)


def _pool_sample(ctx_t, n_ctx, pos0, w_pool, scale):
    total, n, c = ctx_t.shape
    return pl.pallas_call(
        functools.partial(_pool_sample_kernel, n_ctx=n_ctx, pos0=pos0),
        out_shape=jax.ShapeDtypeStruct((total - n_ctx, n, c), BF16),
        compiler_params=pltpu.CompilerParams(vmem_limit_bytes=VMEM_LIMIT_BYTES),
        name="pool_sample",
    )(ctx_t, w_pool, scale.reshape(1, c))


def _gmlp_kernel(u_ref, v_ref, w_ref, b_ref, o_ref):
    t = u_ref.shape[0]
    n_heads = w_ref.shape[0]
    ck = w_ref.shape[1]
    row = lax.broadcasted_iota(I32, (ck, ck), 0)
    col = lax.broadcasted_iota(I32, (ck, ck), 1)
    for h in range(n_heads):
        c0 = h * HEAD_DIM
        wm = jnp.where(col <= row, w_ref[h], 0.0).astype(BF16)
        bias = b_ref[:, h:h + 1]
        for k in range(t // ck):
            r0 = k * ck
            v = v_ref[r0:r0 + ck, c0:c0 + HEAD_DIM].astype(BF16)
            mixed = jnp.dot(wm, v, preferred_element_type=F32) + bias
            o_ref[r0:r0 + ck, c0:c0 + HEAD_DIM] = (u_ref[r0:r0 + ck, c0:c0 + HEAD_DIM] * mixed).astype(o_ref.dtype)


def _gmlp(proj, n_prompt_rows, w_pair, b_pair, t):
    m = proj.shape[0]
    c = proj.shape[1] // 3
    n_heads = w_pair.shape[1]
    first_sample_tile = n_prompt_rows // t
    which = lambda i: jnp.where(i >= first_sample_tile, 1, 0)
    return pl.pallas_call(
        _gmlp_kernel,
        grid=(m // t,),
        in_specs=[pl.BlockSpec((t, c), lambda i: (i, 1)),
                  pl.BlockSpec((t, c), lambda i: (i, 2)),
                  pl.BlockSpec((None, n_heads, GMLP_CHUNK, GMLP_CHUNK), lambda i: (which(i), 0, 0, 0)),
                  pl.BlockSpec((None, GMLP_CHUNK, n_heads), lambda i: (which(i), 0, 0))],
        out_specs=pl.BlockSpec((t, c), lambda i: (i, 0)),
        out_shape=jax.ShapeDtypeStruct((m, c), BF16),
        compiler_params=_params("parallel"),
        name="gmlp",
    )(proj, proj, w_pair, b_pair)


def _conv_prompt_kernel(xc_ref, bg_ref, cg_ref, xch_ref, cgh_ref, w_ref, o_ref, tail_ref, buf_ref, *, tiles_per_seq):
    i = pl.program_id(0) % tiles_per_seq
    t = xc_ref.shape[0]
    z = cg_ref[...] * xc_ref[...]
    buf_ref[0:CONV_HALO, :] = jnp.where(i == 0, 0.0, cgh_ref[...] * xch_ref[...])
    buf_ref[CONV_HALO:CONV_HALO + t, :] = z
    y = w_ref[0:1, :] * buf_ref[CONV_HALO - 2:CONV_HALO - 2 + t, :]
    for j in range(1, CONV_WIDTH):
        y = y + w_ref[j:j + 1, :] * buf_ref[CONV_HALO - 2 + j:CONV_HALO - 2 + j + t, :]
    o_ref[...] = (bg_ref[...] * y).astype(o_ref.dtype)
    tail_ref[...] = z[t - CONV_HALO:, :]


def _conv_prompt(proj, n_rows, seq, conv_w, t):
    c = conv_w.shape[1]
    halo_per_tile = t // CONV_HALO
    halo_row = lambda i: jnp.maximum(i * halo_per_tile - 1, 0)
    return pl.pallas_call(
        functools.partial(_conv_prompt_kernel, tiles_per_seq=seq // t),
        grid=(n_rows // t,),
        in_specs=[pl.BlockSpec((t, c), lambda i: (i, 0)),
                  pl.BlockSpec((t, c), lambda i: (i, 1)),
                  pl.BlockSpec((t, c), lambda i: (i, 2)),
                  pl.BlockSpec((CONV_HALO, c), lambda i: (halo_row(i), 0)),
                  pl.BlockSpec((CONV_HALO, c), lambda i: (halo_row(i), 2)),
                  pl.BlockSpec((CONV_WIDTH, c), lambda i: (0, 0))],
        out_specs=[pl.BlockSpec((t, c), lambda i: (i, 0)),
                   pl.BlockSpec((CONV_HALO, c), lambda i: (i, 0))],
        out_shape=[jax.ShapeDtypeStruct((n_rows, c), BF16),
                   jax.ShapeDtypeStruct((n_rows // t * CONV_HALO, c), F32)],
        scratch_shapes=[pltpu.VMEM((CONV_HALO + t, c), F32)],
        compiler_params=_params("parallel"),
        name="conv_prompt",
    )(proj, proj, proj, proj, proj, conv_w)


def _conv_sample_kernel(ctx_ref, xc_ref, bg_ref, cg_ref, w_ref, o_ref, z_ref):
    n_ctx = ctx_ref.shape[0]
    length = xc_ref.shape[0]
    rows = [ctx_ref[r] for r in range(n_ctx)]
    for t in range(length):
        z = cg_ref[t] * xc_ref[t]
        z_ref[t] = z
        rows.append(z)
    for t in range(length):
        y = w_ref[0:1, :] * rows[t]
        for j in range(1, CONV_WIDTH):
            y = y + w_ref[j:j + 1, :] * rows[t + j]
        o_ref[t] = (bg_ref[t] * y).astype(o_ref.dtype)


def _conv_sample(ctx_t, xc_t, bg_t, cg_t, conv_w):
    length, n, c = xc_t.shape
    return pl.pallas_call(
        _conv_sample_kernel,
        out_shape=[jax.ShapeDtypeStruct((length, n, c), BF16), jax.ShapeDtypeStruct((length, n, c), F32)],
        compiler_params=pltpu.CompilerParams(vmem_limit_bytes=VMEM_LIMIT_BYTES),
        name="conv_sample",
    )(ctx_t, xc_t, bg_t, cg_t, conv_w)


def _log_sigmoid(x):
    return jnp.minimum(x, 0.0) - jnp.log1p(jnp.exp(-jnp.abs(x)))


def _logf_kernel(f_ref, b_ref, lf_ref, c_ref, carry_ref, *, blocks_per_seq):
    i = pl.program_id(0)
    blk = f_ref.shape[0]
    lf = _log_sigmoid(f_ref[...] + b_ref[...])
    lf_ref[...] = lf

    @pl.when(i % blocks_per_seq == 0)
    def _():
        carry_ref[...] = jnp.zeros_like(carry_ref)

    row = lax.broadcasted_iota(I32, (blk, blk), 0)
    col = lax.broadcasted_iota(I32, (blk, blk), 1)
    tri = jnp.where(col <= row, 1.0, 0.0).astype(F32)
    c = jnp.dot(tri, lf, precision=HIGHEST, preferred_element_type=F32) + carry_ref[0:1, :]
    c_ref[...] = c
    carry_ref[...] = jnp.broadcast_to(c[blk - 1:blk, :], carry_ref.shape)


def _logf_cumsum(f_logit, b_forget, seq, blk):
    m = f_logit.shape[0]
    return pl.pallas_call(
        functools.partial(_logf_kernel, blocks_per_seq=seq // blk),
        grid=(m // blk,),
        in_specs=[pl.BlockSpec((blk, LANES), lambda i: (i, 0)), pl.BlockSpec((1, LANES), lambda i: (0, 0))],
        out_specs=[pl.BlockSpec((blk, LANES), lambda i: (i, 0)), pl.BlockSpec((blk, LANES), lambda i: (i, 0))],
        out_shape=[jax.ShapeDtypeStruct((m, LANES), F32), jax.ShapeDtypeStruct((m, LANES), F32)],
        scratch_shapes=[pltpu.VMEM((8, LANES), F32)],
        compiler_params=_params("arbitrary"),
        name="logf_cumsum",
    )(f_logit, b_forget)


FLASH_HEADS = 2


def _flash_kernel(q_ref, k_ref, v_ref, cc_ref, cr_ref, o_ref, k_bf, v_bf, *, tq):
    hp = pl.program_id(1)
    qb = pl.program_id(2)

    @pl.when(qb == 0)
    def _():
        k_bf[...] = k_ref[...].astype(BF16)
        v_bf[...] = v_ref[...].astype(BF16)

    lane = lax.broadcasted_iota(I32, (tq, LANES), 1)
    heads = []
    for hh in range(FLASH_HEADS):
        cols = slice(hh * HEAD_DIM, (hh + 1) * HEAD_DIM)
        q = (q_ref[:, cols] * (1.0 / math.sqrt(HEAD_DIM))).astype(BF16)
        cq = jnp.sum(jnp.where(lane == hp * FLASH_HEADS + hh, cc_ref[...], 0.0), axis=-1, keepdims=True)
        heads.append((cols, q, cq))

    def scores(hh, j):
        cols, q, cq = heads[hh]
        start = pl.multiple_of(j * tq, tq)
        k = k_bf[pl.ds(start, tq), cols]
        v = v_bf[pl.ds(start, tq), cols]
        s = lax.dot_general(q, k, (((1,), (1,)), ((), ())), preferred_element_type=F32)
        ck = cr_ref[hh, :, pl.ds(start, tq)]
        return s + (cq - ck), v

    def update(carry, s, v):
        m, l, acc = carry
        m_new = jnp.maximum(m, jnp.max(s, axis=-1, keepdims=True))
        alpha = jnp.exp(m - m_new)
        p = jnp.exp(s - m_new)
        l = alpha * l + jnp.sum(p, axis=-1, keepdims=True)
        acc = alpha * acc + jnp.dot(p.astype(BF16), v, preferred_element_type=F32)
        return m_new, l, acc

    def body(j, carries):
        return tuple(update(carries[hh], *scores(hh, j)) for hh in range(FLASH_HEADS))

    init = (jnp.full((tq, 1), MASKED, F32), jnp.zeros((tq, 1), F32), jnp.zeros((tq, HEAD_DIM), F32))
    carries = lax.fori_loop(0, qb, body, (init,) * FLASH_HEADS)
    row = lax.broadcasted_iota(I32, (tq, tq), 0)
    col = lax.broadcasted_iota(I32, (tq, tq), 1)
    for hh in range(FLASH_HEADS):
        s, v = scores(hh, qb)
        _, l, acc = update(carries[hh], jnp.where(col <= row, s, MASKED), v)
        o_ref[:, heads[hh][0]] = (acc / l).astype(o_ref.dtype)


def _flash_prompt(proj, c_col, c_row, n_seq, seq, n_heads, tq):
    assert n_heads % FLASH_HEADS == 0
    pairs = n_heads // FLASH_HEADS
    width = FLASH_HEADS * HEAD_DIM
    q_blk, k_blk, v_blk = 3 * pairs, 4 * pairs, 5 * pairs
    nqb = seq // tq
    return pl.pallas_call(
        functools.partial(_flash_kernel, tq=tq),
        grid=(n_seq, pairs, nqb),
        in_specs=[pl.BlockSpec((tq, width), lambda n, h, i: (n * nqb + i, q_blk + h)),
                  pl.BlockSpec((seq, width), lambda n, h, i: (n, k_blk + h)),
                  pl.BlockSpec((seq, width), lambda n, h, i: (n, v_blk + h)),
                  pl.BlockSpec((tq, LANES), lambda n, h, i: (n * nqb + i, 0)),
                  pl.BlockSpec((FLASH_HEADS, 1, seq), lambda n, h, i: (h, 0, n))],
        out_specs=pl.BlockSpec((tq, width), lambda n, h, i: (n * nqb + i, h)),
        out_shape=jax.ShapeDtypeStruct((n_seq * seq, n_heads * HEAD_DIM), BF16),
        scratch_shapes=[pltpu.VMEM((seq, width), BF16), pltpu.VMEM((seq, width), BF16)],
        compiler_params=_params("arbitrary", "arbitrary", "arbitrary"),
        name="fox_prompt",
    )(proj, proj, proj, c_col, c_row)


def _decay_of_later_rows(lf, later_groups, n_heads):
    group, width = lf.shape
    lane = lax.broadcasted_iota(I32, lf.shape, 1)
    incl, total = lf, lf
    shift = n_heads
    while shift < width:
        incl = incl + jnp.where(lane + shift < width, pltpu.roll(incl, width - shift, 1), 0.0)
        total = total + pltpu.roll(total, shift, 1)
        shift *= 2
    later_pages = jnp.zeros_like(lf)
    if group > 1:
        prow = lax.broadcasted_iota(I32, (group, group), 0)
        pcol = lax.broadcasted_iota(I32, (group, group), 1)
        later = jnp.where(pcol > prow, 1.0, 0.0).astype(F32)
        later_pages = jnp.dot(later, total, precision=HIGHEST, preferred_element_type=F32)
    return (incl - lf) + later_pages + later_groups, jnp.sum(total, axis=0, keepdims=True)


def _decode_kernel(pt_ref, q_ref, kn_ref, vn_ref, lfn_ref, k_hbm, v_hbm, lf_hbm, o_ref,
                   k_buf, v_buf, lf_buf, sem, m_ref, l_ref, acc_ref, later_ref,
                   *, n_heads, n_q, group, n_par, layer, n_pages):
    i = pl.program_id(0)
    j = pl.program_id(1)
    n_groups = pl.num_programs(1)
    rows = n_heads * n_q
    width = k_buf.shape[2] * n_heads
    slot = (i * n_groups + j) % 2

    def page_copies(step_i, step_j, into):
        copies = []
        for b in range(n_par):
            for p in range(group):
                page = pt_ref[(step_i * n_par + b) * n_pages + (n_groups - 1 - step_j) * group + p]
                idx = b * group + p
                copies += [pltpu.make_async_copy(k_hbm.at[layer, page], k_buf.at[into, idx], sem.at[into, 0]),
                           pltpu.make_async_copy(v_hbm.at[layer, page], v_buf.at[into, idx], sem.at[into, 1]),
                           pltpu.make_async_copy(lf_hbm.at[page], lf_buf.at[into, idx], sem.at[into, 2])]
        return copies

    @pl.when(jnp.logical_and(i == 0, j == 0))
    def _():
        for c in page_copies(0, 0, 0):
            c.start()

    n_slot_pages = n_par * group
    pltpu.make_async_copy(k_hbm.at[layer, pl.ds(0, n_slot_pages)], k_buf.at[slot], sem.at[slot, 0]).wait()
    pltpu.make_async_copy(v_hbm.at[layer, pl.ds(0, n_slot_pages)], v_buf.at[slot], sem.at[slot, 1]).wait()
    pltpu.make_async_copy(lf_hbm.at[pl.ds(0, n_slot_pages)], lf_buf.at[slot], sem.at[slot, 2]).wait()

    last_group = j == n_groups - 1
    next_i = jnp.where(last_group, i + 1, i)
    next_j = jnp.where(last_group, 0, j + 1)

    @pl.when(next_i < pl.num_programs(0))
    def _():
        for c in page_copies(next_i, next_j, 1 - slot):
            c.start()

    @pl.when(j == 0)
    def _():
        m_ref[...] = jnp.full_like(m_ref, MASKED)
        l_ref[...] = jnp.zeros_like(l_ref)
        acc_ref[...] = jnp.zeros_like(acc_ref)
        later_ref[...] = jnp.zeros_like(later_ref)

    r_i = lax.broadcasted_iota(I32, (rows, LANES), 0)
    c_i = lax.broadcasted_iota(I32, (rows, LANES), 1)
    sel = (c_i % n_heads == r_i // n_q) & (c_i // n_heads <= r_i % n_q) & (c_i < n_q * n_heads)
    r_w = lax.broadcasted_iota(I32, (rows, group * width), 0)
    c_w = lax.broadcasted_iota(I32, (rows, group * width), 1)
    same_head = c_w % n_heads == r_w // n_q

    def update(state, s, v):
        m_old, l, acc = state
        m_new = jnp.maximum(m_old, jnp.max(s, axis=-1, keepdims=True))
        alpha = jnp.exp(m_old - m_new)
        p = jnp.exp(s - m_new)
        l = alpha * l + jnp.sum(p, axis=-1, keepdims=True)
        acc = alpha * acc + jnp.dot(p.astype(BF16), v, preferred_element_type=F32)
        return m_new, l, acc

    def load_state(b):
        return m_ref[b, :, 0:1], l_ref[b, :, 0:1], acc_ref[b]

    def store_state(b, state):
        m, l, acc = state
        m_ref[b] = jnp.broadcast_to(m, m_ref.shape[1:])
        l_ref[b] = jnp.broadcast_to(l, l_ref.shape[1:])
        acc_ref[b] = acc

    states = [load_state(b) for b in range(n_par)]
    later = [later_ref[b] for b in range(n_par)]
    queries = []
    for b in range(n_par):
        pages = range(b * group, (b + 1) * group)
        lf = jnp.concatenate([lf_buf[slot, idx] for idx in pages], axis=0)
        decay, group_total = _decay_of_later_rows(lf, later[b][0:1, :], n_heads)
        later[b] = later[b] + group_total

        q = (q_ref[b] * (1.0 / math.sqrt(HEAD_DIM))).astype(BF16)
        lfn = lfn_ref[b]
        cq = jnp.sum(jnp.where(sel, jnp.broadcast_to(lfn, (rows, LANES)), 0.0), axis=-1, keepdims=True)
        queries.append((q, lfn, cq))

        k = jnp.concatenate([k_buf[slot, idx].reshape(width, HEAD_DIM).astype(BF16) for idx in pages], axis=0)
        v = jnp.concatenate([v_buf[slot, idx].reshape(width, HEAD_DIM).astype(BF16) for idx in pages], axis=0)
        decay_row = jnp.concatenate([decay[p:p + 1, :] for p in range(group)], axis=1)
        s = lax.dot_general(q, k, (((1,), (1,)), ((), ())), preferred_element_type=F32)
        s = jnp.where(same_head, s + decay_row + cq, MASKED)
        states[b] = update(states[b], s, v)
    for b in range(n_par):
        store_state(b, states[b])
        later_ref[b] = later[b]

    @pl.when(j == pl.num_programs(1) - 1)
    def _():
        for b in range(n_par):
            q, lfn, cq = queries[b]
            kn = kn_ref[b].astype(BF16)
            vn = vn_ref[b].astype(BF16)
            sn = lax.dot_general(q, kn, (((1,), (1,)), ((), ())), preferred_element_type=F32)
            lf8 = jnp.broadcast_to(lfn, (8, LANES))
            lane8 = lax.broadcasted_iota(I32, (8, LANES), 1)
            csum = lf8
            shift = n_heads
            while shift < n_q * n_heads:
                csum = csum + jnp.where(lane8 >= shift, pltpu.roll(csum, shift, 1), 0.0)
                shift *= 2
            _, l, acc = update(states[b], jnp.where(sel, sn + (cq - csum[0:1, :]), MASKED), vn)
            o_ref[b] = acc / l


def _decode_attention(q_hq, cache_k, cache_v, logf_pages, layer, page_table, k_new, v_new, lf_new, n_heads, n_q):
    n, n_pages = page_table.shape
    assert n_pages >= 1
    page = cache_k.shape[2]
    rows = n_heads * n_q
    width = page * n_heads
    group = max(g for g in (1, 2, 4, 8) if n_pages % g == 0)
    n_groups = n_pages // group
    n_par = 2 if n % 2 == 0 else 1

    n_slot_pages = n_par * group
    assert cache_k.shape[1] >= n_slot_pages
    per_seq = lambda shape: pl.BlockSpec((n_par,) + shape, lambda i, j, pt: (i, 0, 0))
    hbm = pl.BlockSpec(memory_space=pl.ANY)
    grid_spec = pltpu.PrefetchScalarGridSpec(
        num_scalar_prefetch=1,
        grid=(n // n_par, n_groups),
        in_specs=[per_seq((rows, HEAD_DIM)), per_seq((LANES, HEAD_DIM)), per_seq((LANES, HEAD_DIM)),
                  per_seq((1, LANES)), hbm, hbm, hbm],
        out_specs=per_seq((rows, HEAD_DIM)),
        scratch_shapes=[pltpu.VMEM((2, n_slot_pages, page, n_heads, HEAD_DIM), F32),
                        pltpu.VMEM((2, n_slot_pages, page, n_heads, HEAD_DIM), F32),
                        pltpu.VMEM((2, n_slot_pages, 1, width), F32),
                        pltpu.SemaphoreType.DMA((2, 3)),
                        pltpu.VMEM((n_par, rows, LANES), F32), pltpu.VMEM((n_par, rows, LANES), F32),
                        pltpu.VMEM((n_par, rows, HEAD_DIM), F32), pltpu.VMEM((n_par, 8, width), F32)],
    )
    return pl.pallas_call(
        functools.partial(_decode_kernel, n_heads=n_heads, n_q=n_q, group=group, n_par=n_par, layer=layer,
                          n_pages=n_pages),
        grid_spec=grid_spec,
        out_shape=jax.ShapeDtypeStruct((n, rows, HEAD_DIM), F32),
        compiler_params=_params("arbitrary", "arbitrary"),
        name="fox_decode",
    )(page_table.reshape(-1), q_hq, k_new, v_new, lf_new, cache_k, cache_v, logf_pages)


def kernel(x_prompt, x_sample, state_pool, state_conv, cache_k, cache_v, cache_logf, page_table, e_norm_mix_pre, e_norm_mix_post, e_norm_ffn_pre, e_norm_ffn_post, e_w_in, e_w_pool, e_pool_scale, e_w_spatial, e_b_spatial, e_w_out, e_ffn_gate, e_ffn_up, e_ffn_down, o_norm_mix_pre, o_norm_mix_post, o_norm_ffn_pre, o_norm_ffn_post, o_w_in, o_conv_w, o_b_forget, o_w_out, o_w_router, o_exp_gate, o_exp_up, o_exp_down):
    n_p, seq, d = x_prompt.shape
    n_s, dec, _ = x_sample.shape
    mp, ms = n_p * seq, n_s * dec
    m = mp + ms
    c = d // 2
    n_heads = c // HEAD_DIM
    n_pages = page_table.shape[1]
    page = cache_k.shape[2]
    past_len = n_pages * page
    n_layers = e_w_in.shape[0] + o_w_in.shape[0]
    n_experts = o_w_router.shape[2]
    assert ms % GMLP_CHUNK == 0 and seq % GMLP_CHUNK == 0 and past_len % GMLP_CHUNK == 0
    assert GMLP_CHUNK % min(dec, GMLP_CHUNK) == 0 and dec % min(dec, GMLP_CHUNK) == 0
    assert dec * n_heads <= LANES and dec >= CONV_WIDTH - 1

    tm = _tile(math.gcd(mp, ms), 512, LANES)
    t_seq = _tile(math.gcd(seq, tm), 512, LANES)
    tq = _tile(seq, 512, LANES)
    tm_moe = 512
    dense_tiles = _dense_tiles(m // tm)

    h = (x_prompt.reshape(mp, d), x_sample.reshape(ms, d))
    xn = _norm_cast(h, e_norm_mix_pre[0], tm)
    out_rows = None

    pool_p, pool_s, gv_s, conv_p, conv_s = [], [], [], [], []
    kp_l, vp_l, lp_l, ks_l, vs_l, ls_l = [], [], [], [], [], []

    def time_major(x2d):
        return jnp.transpose(x2d.reshape(n_s, dec, -1), (1, 0, 2))

    def row_major(x3d):
        return jnp.transpose(x3d, (1, 0, 2)).reshape(ms, -1)

    for layer in range(n_layers):
        i = layer // 2
        if layer % 2 == 0:
            g_next = e_norm_ffn_pre[i]
            proj = _mm(xn, e_w_in, 3 * c, tm, _tile(3 * c, 1024, LANES), w_layer=i)
            p_s = proj[mp:, :c].reshape(n_s, dec, c)
            ctx = jnp.concatenate([state_pool[i].astype(F32), p_s], axis=1)
            n_ctx = state_pool.shape[2]
            w_pool = e_w_pool[i].astype(BF16)
            a_p = _pool_prompt(proj, mp, seq, w_pool, e_pool_scale[i], t_seq)
            a_s = row_major(_pool_sample(jnp.transpose(ctx, (1, 0, 2)), n_ctx, past_len, w_pool, e_pool_scale[i]))
            cl = min(dec, GMLP_CHUNK)
            reps = GMLP_CHUNK // cl
            w_samp = jnp.einsum("ab,hts->hatbs", jnp.eye(reps, dtype=F32),
                                e_w_spatial[i][:, :cl, :cl]).reshape(n_heads, GMLP_CHUNK, GMLP_CHUNK)
            b_samp = jnp.tile(e_b_spatial[i][:, :cl], (1, reps))
            w_pair = jnp.stack([e_w_spatial[i], w_samp])
            b_pair = jnp.stack([e_b_spatial[i].T, b_samp.T])
            b = _gmlp(proj, mp, w_pair, b_pair, tm)
            h, xn = _proj_out((a_p, a_s), (b,), e_w_out[i].astype(BF16), h, e_norm_mix_post[i], g_next, tm, mp)
            g_after = o_norm_mix_pre[i] if layer + 1 < n_layers else jnp.ones((d,), F32)
            mix = _swiglu_grouped(xn, e_ffn_gate[i][None], e_ffn_up[i][None], e_ffn_down[i][None], dense_tiles, tm)
            h, xn = _norm_residual(mix, h, e_norm_ffn_post[i], g_after, tm)
            h = (h,)
            pool_p.append(jnp.stack([proj[(s + 1) * seq - n_ctx:(s + 1) * seq, :c] for s in range(n_p)]))
            pool_s.append(ctx[:, ctx.shape[1] - n_ctx:])
            gv_s.append(proj[mp:, 2 * c:].reshape(n_s, dec, c))
        else:
            w_in = o_w_in[i]
            proj, k_p, v_p = _mm(xn, o_w_in, 6 * c, tm, c, copy_cols=(4, 5), copy_rows=mp, w_layer=i)
            w_f = jnp.zeros((d, LANES), F32).at[:, :n_heads].set(w_in[:, 6 * c:])
            f_logit = _mm(xn, w_f, LANES, tm, LANES)
            b_f = jnp.zeros((1, LANES), F32).at[0, :n_heads].set(o_b_forget[i])
            logf, csum = _logf_cumsum(f_logit, b_f, seq, _tile(math.gcd(seq, m), 512, LANES))
            c_p, z_tail = _conv_prompt(proj, mp, seq, o_conv_w[i], t_seq)
            z_tail = z_tail.reshape(n_p, seq // t_seq, CONV_HALO, c)[:, -1, CONV_HALO - (CONV_WIDTH - 1):]
            xs = proj[mp:]
            conv_ctx = jnp.transpose(state_conv[i].astype(F32), (1, 0, 2))
            c_s, z_s = _conv_sample(conv_ctx, time_major(xs[:, :c]), time_major(xs[:, c:2 * c]),
                                    time_major(xs[:, 2 * c:3 * c]), o_conv_w[i])
            z_all = jnp.concatenate([conv_ctx, z_s], axis=0)
            c_row = csum[:mp, :n_heads].T.reshape(n_heads, 1, mp)
            att_p = _flash_prompt(proj, csum, c_row, n_p, seq, n_heads, tq)
            k_s = xs[:, 4 * c:5 * c]
            v_s = xs[:, 5 * c:6 * c]
            q_hq = jnp.transpose(xs[:, 3 * c:4 * c].reshape(n_s, dec, n_heads, HEAD_DIM), (0, 2, 1, 3))
            q_hq = q_hq.reshape(n_s, n_heads * dec, HEAD_DIM)
            pad_rows = LANES - dec * n_heads
            k_new = jnp.pad(k_s.reshape(n_s, dec * n_heads, HEAD_DIM), ((0, 0), (0, pad_rows), (0, 0)))
            v_new = jnp.pad(v_s.reshape(n_s, dec * n_heads, HEAD_DIM), ((0, 0), (0, pad_rows), (0, 0)))
            lf_new = jnp.pad(logf[mp:, :n_heads].reshape(n_s, 1, dec * n_heads), ((0, 0), (0, 0), (0, pad_rows)))
            logf_pages = cache_logf[i].astype(F32).reshape(cache_logf.shape[1], 1, page * n_heads)
            att_s = _decode_attention(q_hq, cache_k, cache_v, logf_pages, i, page_table, k_new, v_new, lf_new,
                                      n_heads, dec)
            att_s = jnp.transpose(att_s.reshape(n_s, n_heads, dec, HEAD_DIM), (0, 2, 1, 3)).reshape(ms, c)
            h, _, route = _proj_out((c_p, row_major(c_s)), (att_p, att_s.astype(BF16)), o_w_out[i].astype(BF16), h,
                                    o_norm_mix_post[i], o_norm_ffn_pre[i], tm, mp, w_router=o_w_router[i])
            slot_token, token_slots, expert_tiles = _routing_tables(route, n_experts, tm_moe)
            x_sorted = _dispatch(h, o_norm_ffn_pre[i], slot_token, tm_moe)
            y_sorted = _swiglu_grouped(x_sorted, o_exp_gate[i], o_exp_up[i], o_exp_down[i], expert_tiles, tm_moe)
            t_comb = _tile(math.gcd(mp, ms), 256)
            if layer + 1 < n_layers:
                h = (_combine(y_sorted, token_slots, route, h, o_norm_ffn_post[i], t_comb),)
                xn = _norm_cast(h, e_norm_mix_pre[i + 1], tm)
            else:
                out_rows = _combine(y_sorted, token_slots, route, h, o_norm_ffn_post[i], t_comb, split_rows=mp)
            conv_p.append(z_tail)
            conv_s.append(jnp.transpose(z_all[z_all.shape[0] - (CONV_WIDTH - 1):], (1, 0, 2)))
            kp_l.append(k_p.reshape(n_p, seq, n_heads, HEAD_DIM))
            vp_l.append(v_p.reshape(n_p, seq, n_heads, HEAD_DIM))
            lp_l.append(logf[:mp, :n_heads].reshape(n_p, seq, n_heads))
            ks_l.append(k_s.reshape(n_s, dec, n_heads, HEAD_DIM))
            vs_l.append(v_s.reshape(n_s, dec, n_heads, HEAD_DIM))
            ls_l.append(logf[mp:, :n_heads].reshape(n_s, dec, n_heads))

    if out_rows is None:
        out_rows = (h[0][:mp], h[0][mp:])
    return (out_rows[0].reshape(n_p, seq, d), out_rows[1].reshape(n_s, dec, d),
            jnp.stack(pool_p), jnp.stack(pool_s), jnp.stack(gv_s), jnp.stack(conv_p), jnp.stack(conv_s),
            jnp.stack(kp_l), jnp.stack(vp_l), jnp.stack(lp_l), jnp.stack(ks_l), jnp.stack(vs_l), jnp.stack(ls_l))
```

```python
import functools
import math

import jax
import jax.numpy as jnp
from jax import lax
from jax.experimental import pallas as pl
from jax.experimental.pallas import tpu as pltpu

F32, BF16, I32 = jnp.float32, jnp.bfloat16, jnp.int32
RMS_EPS = 1e-6
POOL_WINDOWS = (2, 4, 8, 16)
POOL_HALO = 16
CONV_WIDTH = 3
CONV_HALO = 8
HEAD_DIM = 128
GMLP_CHUNK = 128
TOP_K = 2
LANES = 128
MASKED = -1e30
VMEM_LIMIT_BYTES = 56 * 1024 * 1024
HIGHEST = lax.Precision.HIGHEST


def _params(*semantics):
    return pltpu.CompilerParams(dimension_semantics=semantics, vmem_limit_bytes=VMEM_LIMIT_BYTES)


def _tile(n, pref, mult=8):
    t = min(n, pref)
    t -= t % mult
    while t > mult and n % t:
        t -= mult
    assert t > 0 and n % t == 0, (n, pref, mult)
    return t


def _rms(x, g):
    return x * lax.rsqrt(jnp.mean(x * x, axis=-1, keepdims=True) + RMS_EPS) * g


def _row_specs(parts, tm):
    if len(parts) == 1:
        return [pl.BlockSpec((tm, parts[0].shape[1]), lambda i, *_: (i, 0))]
    first_sample_tile = parts[0].shape[0] // tm
    return [pl.BlockSpec((tm, parts[0].shape[1]), lambda i, *_: (jnp.minimum(i, first_sample_tile - 1), 0)),
            pl.BlockSpec((tm, parts[1].shape[1]), lambda i, *_: (jnp.maximum(i - first_sample_tile, 0), 0))]


def _read_rows(refs, first_sample_tile, tile):
    if len(refs) == 1:
        return refs[0][...]
    return jnp.where(tile < first_sample_tile, refs[0][...], refs[1][...])


def _norm_cast_kernel(*refs, first_sample_tile):
    g_ref, o_ref = refs[-2:]
    x = _read_rows(refs[:-2], first_sample_tile, pl.program_id(0))
    o_ref[...] = _rms(x, g_ref[...]).astype(o_ref.dtype)


def _norm_cast(x_parts, g, tm):
    m = sum(p.shape[0] for p in x_parts)
    d = x_parts[0].shape[1]
    return pl.pallas_call(
        functools.partial(_norm_cast_kernel, first_sample_tile=x_parts[0].shape[0] // tm),
        grid=(m // tm,),
        in_specs=_row_specs(x_parts, tm) + [pl.BlockSpec((1, d), lambda i: (0, 0))],
        out_specs=pl.BlockSpec((tm, d), lambda i: (i, 0)),
        out_shape=jax.ShapeDtypeStruct((m, d), BF16),
        compiler_params=_params("parallel"),
        name="norm_cast",
    )(*x_parts, g.reshape(1, d))


def _mm_kernel(x_ref, w_ref, o_ref, *rest, copy_cols, copy_tiles):
    copy_refs, w_bf = rest[:-1], rest[-1]

    @pl.when(pl.program_id(1) == 0)
    def _():
        w_bf[...] = w_ref[...].astype(BF16)

    y = jnp.dot(x_ref[...], w_bf[...], preferred_element_type=F32)
    o_ref[...] = y
    for ref, col in zip(copy_refs, copy_cols):
        @pl.when(jnp.logical_and(pl.program_id(0) == col, pl.program_id(1) < copy_tiles))
        def _(ref=ref):
            ref[...] = y


def _mm(x, w, n_cols, tm, tn, copy_cols=(), copy_rows=0, w_layer=0):
    m, k = x.shape
    n = n_cols
    copy_tiles = copy_rows // tm
    assert copy_rows % tm == 0 and n % tn == 0
    if w.ndim == 3:
        assert n <= w.shape[2]
        w_spec = pl.BlockSpec((None, k, tn), lambda j, i: (w_layer, 0, j))
    else:
        assert n <= w.shape[1]
        w_spec = pl.BlockSpec((k, tn), lambda j, i: (0, j))

    def copy_spec(col):
        def index(j, i):
            row = jnp.where(j < col, 0, jnp.where(j > col, copy_tiles - 1, jnp.minimum(i, copy_tiles - 1)))
            return (row, 0)
        return pl.BlockSpec((tm, tn), index)

    out = pl.pallas_call(
        functools.partial(_mm_kernel, copy_cols=tuple(copy_cols), copy_tiles=copy_tiles),
        grid=(n // tn, m // tm),
        in_specs=[pl.BlockSpec((tm, k), lambda j, i: (i, 0)), w_spec],
        out_specs=[pl.BlockSpec((tm, tn), lambda j, i: (i, j))] + [copy_spec(col) for col in copy_cols],
        out_shape=[jax.ShapeDtypeStruct((m, n), F32)] + [jax.ShapeDtypeStruct((copy_rows, tn), F32)] * len(copy_cols),
        scratch_shapes=[pltpu.VMEM((k, tn), BF16)],
        compiler_params=_params("arbitrary", "arbitrary"),
        name="proj_in",
    )(x, w)
    return out if copy_cols else out[0]


def _top2(logits, n_experts):
    lane = lax.broadcasted_iota(I32, logits.shape, 1)
    neg_inf = jnp.float32(-jnp.inf)
    l1 = jnp.where(lane < n_experts, logits, neg_inf)
    m1 = jnp.max(l1, axis=-1, keepdims=True)
    i1 = jnp.min(jnp.where(l1 == m1, lane, LANES), axis=-1, keepdims=True)
    l2 = jnp.where(lane == i1, neg_inf, l1)
    m2 = jnp.max(l2, axis=-1, keepdims=True)
    i2 = jnp.min(jnp.where(l2 == m2, lane, LANES), axis=-1, keepdims=True)
    e2 = jnp.exp(m2 - m1)
    g1 = 1.0 / (1.0 + e2)
    g2 = e2 / (1.0 + e2)
    return jnp.where(lane == 0, g1,
                     jnp.where(lane == 1, g2,
                               jnp.where(lane == 2, i1.astype(F32),
                                         jnp.where(lane == 3, i2.astype(F32), 0.0))))


def _proj_out_kernel(*refs, n_experts, layout, first_sample_tile):
    na, nb, nh = layout
    tile = pl.program_id(0)
    a = _read_rows(refs[:na], first_sample_tile, tile)
    b = _read_rows(refs[na:na + nb], first_sample_tile, tile)
    h = _read_rows(refs[na + nb:na + nb + nh], first_sample_tile, tile)
    w_ref, gpost_ref, gnext_ref, *rest = refs[na + nb + nh:]
    half = a.shape[1]
    m = (jnp.dot(a, w_ref[:half, :], preferred_element_type=F32)
         + jnp.dot(b, w_ref[half:, :], preferred_element_type=F32))
    hn = h + _rms(m, gpost_ref[...])
    xn = _rms(hn, gnext_ref[...])
    xn_hi = xn.astype(BF16)
    if n_experts:
        wr_ref, hn_ref, xn_ref, route_ref = rest
        xn_lo = (xn - xn_hi.astype(F32)).astype(BF16)
        both = jnp.dot(xn_hi, wr_ref[...], preferred_element_type=F32)
        logits = (both[:, :LANES] + jnp.dot(xn_lo, wr_ref[:, :LANES], preferred_element_type=F32)
                  + both[:, LANES:])
        route_ref[...] = _top2(logits, n_experts)
    else:
        hn_ref, xn_ref = rest
    hn_ref[...] = hn
    xn_ref[...] = xn_hi


def _proj_out(a_parts, b_parts, w, h_parts, g_post, g_next, tm, n_prompt_rows, w_router=None):
    m = sum(p.shape[0] for p in h_parts)
    d = h_parts[0].shape[1]
    half = a_parts[0].shape[1]
    n_experts = 0 if w_router is None else w_router.shape[1]
    first_sample_tile = n_prompt_rows // tm
    assert all(len(p) == 1 or p[0].shape[0] == n_prompt_rows for p in (a_parts, b_parts, h_parts))
    row = lambda i: (i, 0)
    fixed = lambda i: (0, 0)
    in_specs = (_row_specs(a_parts, tm) + _row_specs(b_parts, tm) + _row_specs(h_parts, tm)
                + [pl.BlockSpec((2 * half, d), fixed), pl.BlockSpec((1, d), fixed), pl.BlockSpec((1, d), fixed)])
    out_specs = [pl.BlockSpec((tm, d), row), pl.BlockSpec((tm, d), row)]
    out_shape = [jax.ShapeDtypeStruct((m, d), F32), jax.ShapeDtypeStruct((m, d), BF16)]
    args = [*a_parts, *b_parts, *h_parts, w, g_post.reshape(1, d), g_next.reshape(1, d)]
    if n_experts:
        wr = jnp.zeros((d, LANES), F32).at[:, :n_experts].set(w_router.astype(F32))
        wr_hi = wr.astype(BF16)
        wr_lo = (wr - wr_hi.astype(F32)).astype(BF16)
        in_specs.append(pl.BlockSpec((d, 2 * LANES), fixed))
        out_specs.append(pl.BlockSpec((tm, LANES), row))
        out_shape.append(jax.ShapeDtypeStruct((m, LANES), F32))
        args.append(jnp.concatenate([wr_hi, wr_lo], axis=1))
    return pl.pallas_call(
        functools.partial(_proj_out_kernel, n_experts=n_experts, first_sample_tile=first_sample_tile,
                          layout=(len(a_parts), len(b_parts), len(h_parts))),
        grid=(m // tm,),
        in_specs=in_specs, out_specs=out_specs, out_shape=out_shape,
        compiler_params=_params("parallel"),
        name="proj_out",
    )(*args)


def _norm_residual_kernel(m_ref, h_ref, gpost_ref, gnext_ref, hn_ref, xn_ref):
    hn = h_ref[...] + _rms(m_ref[...], gpost_ref[...])
    hn_ref[...] = hn
    xn_ref[...] = _rms(hn, gnext_ref[...]).astype(xn_ref.dtype)


def _norm_residual(mix, h, g_post, g_next, tm):
    m, d = h.shape
    row = lambda i: (i, 0)
    fixed = lambda i: (0, 0)
    return pl.pallas_call(
        _norm_residual_kernel,
        grid=(m // tm,),
        in_specs=[pl.BlockSpec((tm, d), row), pl.BlockSpec((tm, d), row),
                  pl.BlockSpec((1, d), fixed), pl.BlockSpec((1, d), fixed)],
        out_specs=[pl.BlockSpec((tm, d), row), pl.BlockSpec((tm, d), row)],
        out_shape=[jax.ShapeDtypeStruct((m, d), F32), jax.ShapeDtypeStruct((m, d), BF16)],
        compiler_params=_params("parallel"),
        name="norm_residual",
    )(mix, h, g_post.reshape(1, d), g_next.reshape(1, d))


def _ring_step(te_ref, nv_ref, nxt_ref, wrap_ref, cnt_ref, copies, consume):
    j = pl.program_id(0)
    t = pl.program_id(1)

    @pl.when(jnp.logical_and(j == 0, t == 0))
    def _():
        cnt_ref[0] = 0
        for c in copies(te_ref[0], 0, 0):
            c.start()

    group_start = jnp.logical_or(t == 0, te_ref[t] != te_ref[jnp.maximum(t - 1, 0)])

    @pl.when(jnp.logical_and(group_start, t < nv_ref[0]))
    def _():
        slot = cnt_ref[0] % 2
        for c in copies(te_ref[t], j, slot):
            c.wait()
        consume(slot)
        j_next = j + wrap_ref[t]

        @pl.when(j_next < pl.num_programs(0))
        def _():
            for c in copies(nxt_ref[t], j_next, 1 - slot):
                c.start()
        cnt_ref[0] = cnt_ref[0] + 1


def _ffn_up_kernel(te_ref, src_ref, nv_ref, nxt_ref, wrap_ref, half_ref, x_ref, wg_hbm, wu_hbm, o_ref,
                   wbuf, wg_bf, wu_bf, sem, cnt_ref):
    t = pl.program_id(1)
    tf = wg_bf.shape[1]

    def copies(e, j, slot):
        cols = pl.ds(pl.multiple_of(j * tf, tf), tf)
        return (pltpu.make_async_copy(wg_hbm.at[e, :, cols], wbuf.at[slot, 0], sem.at[slot, 0]),
                pltpu.make_async_copy(wu_hbm.at[e, :, cols], wbuf.at[slot, 1], sem.at[slot, 1]))

    def consume(slot):
        wg_bf[...] = wbuf[slot, 0].astype(BF16)
        wu_bf[...] = wbuf[slot, 1].astype(BF16)

    _ring_step(te_ref, nv_ref, nxt_ref, wrap_ref, cnt_ref, copies, consume)

    def rows(r):
        x = x_ref[r, :]
        g = jnp.dot(x, wg_bf[...], preferred_element_type=F32)
        u = jnp.dot(x, wu_bf[...], preferred_element_type=F32)
        o_ref[r, :] = (g * jax.nn.sigmoid(g) * u).astype(o_ref.dtype)

    _tile_rows(t, nv_ref, half_ref, o_ref, rows)


def _tile_rows(t, nv_ref, half_ref, o_ref, rows):
    tm = o_ref.shape[0]
    valid = t < nv_ref[0]
    half = half_ref[t] == 1

    @pl.when(jnp.logical_and(valid, jnp.logical_not(half)))
    def _():
        rows(slice(None))

    @pl.when(jnp.logical_and(valid, half))
    def _():
        rows(slice(0, tm // 2))
        o_ref[tm // 2:, :] = jnp.zeros((tm - tm // 2, o_ref.shape[1]), o_ref.dtype)

    @pl.when(jnp.logical_not(valid))
    def _():
        o_ref[...] = jnp.zeros_like(o_ref)


def _ffn_up(x, w_gate, w_up, tiles, tm, tf):
    s, k = x.shape
    f = w_gate.shape[2]
    n_tiles = s // tm
    grid_spec = pltpu.PrefetchScalarGridSpec(
        num_scalar_prefetch=len(tiles),
        grid=(f // tf, n_tiles),
        in_specs=[pl.BlockSpec((tm, k), lambda j, t, te, src, *_: (src[t], 0)),
                  pl.BlockSpec(memory_space=pl.ANY), pl.BlockSpec(memory_space=pl.ANY)],
        out_specs=pl.BlockSpec((tm, tf), lambda j, t, *_: (t, j)),
        scratch_shapes=[pltpu.VMEM((2, 2, k, tf), F32), pltpu.VMEM((k, tf), BF16), pltpu.VMEM((k, tf), BF16),
                        pltpu.SemaphoreType.DMA((2, 2)), pltpu.SMEM((1,), I32)],
    )
    return pl.pallas_call(
        _ffn_up_kernel, grid_spec=grid_spec,
        out_shape=jax.ShapeDtypeStruct((s, f), BF16),
        compiler_params=_params("arbitrary", "arbitrary"),
        name="ffn_up",
    )(*tiles, x, w_gate, w_up)


def _ffn_down_kernel(te_ref, src_ref, nv_ref, nxt_ref, wrap_ref, half_ref, a_ref, wd_hbm, o_ref,
                     wbuf, wd_bf, sem, cnt_ref):
    t = pl.program_id(1)
    tn = wd_bf.shape[1]

    def copies(e, j, slot):
        cols = pl.ds(pl.multiple_of(j * tn, tn), tn)
        return (pltpu.make_async_copy(wd_hbm.at[e, :, cols], wbuf.at[slot], sem.at[slot]),)

    def consume(slot):
        wd_bf[...] = wbuf[slot].astype(BF16)

    _ring_step(te_ref, nv_ref, nxt_ref, wrap_ref, cnt_ref, copies, consume)

    def rows(r):
        o_ref[r, :] = jnp.dot(a_ref[r, :], wd_bf[...], preferred_element_type=F32)

    _tile_rows(t, nv_ref, half_ref, o_ref, rows)


def _ffn_down(a, w_down, tiles, tm, tn):
    s, f = a.shape
    d = w_down.shape[2]
    n_tiles = s // tm
    grid_spec = pltpu.PrefetchScalarGridSpec(
        num_scalar_prefetch=len(tiles),
        grid=(d // tn, n_tiles),
        in_specs=[pl.BlockSpec((tm, f), lambda j, t, te, src, *_: (src[t], 0)),
                  pl.BlockSpec(memory_space=pl.ANY)],
        out_specs=pl.BlockSpec((tm, tn), lambda j, t, *_: (t, j)),
        scratch_shapes=[pltpu.VMEM((2, f, tn), F32), pltpu.VMEM((f, tn), BF16),
                        pltpu.SemaphoreType.DMA((2,)), pltpu.SMEM((1,), I32)],
    )
    return pl.pallas_call(
        _ffn_down_kernel, grid_spec=grid_spec,
        out_shape=jax.ShapeDtypeStruct((s, d), F32),
        compiler_params=_params("arbitrary", "arbitrary"),
        name="ffn_down",
    )(*tiles, a, w_down)


def _swiglu_grouped(x, w_gate, w_up, w_down, tiles, tm):
    tf = _tile(w_gate.shape[2], 512, LANES)
    tn = _tile(w_down.shape[2], 512, LANES)
    act = _ffn_up(x, w_gate, w_up, tiles, tm, tf)
    return _ffn_down(act, w_down, tiles, tm, tn)


def _dense_tiles(n_tiles):
    zeros = jnp.zeros((n_tiles,), I32)
    return (zeros, jnp.arange(n_tiles, dtype=I32), jnp.full((1,), n_tiles, I32), zeros, jnp.ones((n_tiles,), I32),
            zeros)


GATHER_UNROLL = 8


def _row_copy(src_hbm, row, buf, slot, r, sem):
    return pltpu.make_async_copy(src_hbm.at[pl.ds(row, 1), :], buf.at[slot, pl.ds(r, 1), :], sem.at[slot])


def _wait_slot(src_hbm, buf, slot, sem):
    pltpu.make_async_copy(src_hbm.at[pl.ds(0, buf.shape[1]), :], buf.at[slot], sem.at[slot]).wait()


def _dispatch_kernel(tok_ref, h_hbm, g_ref, o_ref, buf, sem, *, tm):
    t = pl.program_id(0)
    n_t = pl.num_programs(0)

    def issue(tile, slot):
        def body(blk, c):
            for u in range(GATHER_UNROLL):
                r = blk * GATHER_UNROLL + u
                _row_copy(h_hbm, tok_ref[tile * tm + r], buf, slot, r, sem).start(priority=u % 2)
            return c
        lax.fori_loop(0, tm // GATHER_UNROLL, body, 0)

    @pl.when(t == 0)
    def _():
        issue(0, 0)

    @pl.when(t + 1 < n_t)
    def _():
        issue(t + 1, (t + 1) % 2)

    slot = t % 2
    _wait_slot(h_hbm, buf, slot, sem)
    o_ref[...] = _rms(buf[slot], g_ref[...]).astype(o_ref.dtype)


def _dispatch(h, g, slot_token, tm):
    s = slot_token.shape[0]
    d = h.shape[1]
    grid_spec = pltpu.PrefetchScalarGridSpec(
        num_scalar_prefetch=1,
        grid=(s // tm,),
        in_specs=[pl.BlockSpec(memory_space=pl.ANY), pl.BlockSpec((1, d), lambda t, tok: (0, 0))],
        out_specs=pl.BlockSpec((tm, d), lambda t, tok: (t, 0)),
        scratch_shapes=[pltpu.VMEM((2, tm, d), F32), pltpu.SemaphoreType.DMA((2,))],
    )
    return pl.pallas_call(
        functools.partial(_dispatch_kernel, tm=tm), grid_spec=grid_spec,
        out_shape=jax.ShapeDtypeStruct((s, d), BF16),
        compiler_params=_params("arbitrary"),
        name="moe_dispatch",
    )(slot_token, h, g.reshape(1, d))


def _combine_kernel(slot_ref, y_hbm, route_ref, h_ref, gpost_ref, *rest, tm, first_sample_tile):
    out_refs, (buf, sem) = rest[:-2], rest[-2:]
    t = pl.program_id(0)
    n_t = pl.num_programs(0)

    def issue(tile, slot):
        def body(blk, c):
            for u in range(GATHER_UNROLL // TOP_K):
                r = blk * (GATHER_UNROLL // TOP_K) + u
                for k in range(TOP_K):
                    _row_copy(y_hbm, slot_ref[k, tile * tm + r], buf, slot, k * tm + r, sem).start(priority=k % 2)
            return c
        lax.fori_loop(0, tm // (GATHER_UNROLL // TOP_K), body, 0)

    @pl.when(t == 0)
    def _():
        issue(0, 0)

    @pl.when(t + 1 < n_t)
    def _():
        issue(t + 1, (t + 1) % 2)

    slot = t % 2
    _wait_slot(y_hbm, buf, slot, sem)
    route = route_ref[...]
    mix = route[:, 0:1] * buf[slot, 0:tm, :]
    for k in range(1, TOP_K):
        mix = mix + route[:, k:k + 1] * buf[slot, k * tm:(k + 1) * tm, :]
    result = h_ref[...] + _rms(mix, gpost_ref[...])
    if len(out_refs) == 1:
        out_refs[0][...] = result
    else:
        @pl.when(t < first_sample_tile)
        def _():
            out_refs[0][...] = result

        @pl.when(t >= first_sample_tile)
        def _():
            out_refs[1][...] = result


def _combine(y_sorted, token_slots, route, h, g_post, tm, split_rows=None):
    m, d = h.shape
    if split_rows is None:
        first_sample_tile = 0
        out_specs = pl.BlockSpec((tm, d), lambda t, sl: (t, 0))
        out_shape = jax.ShapeDtypeStruct((m, d), F32)
    else:
        assert split_rows % tm == 0 and 0 < split_rows < m
        first_sample_tile = split_rows // tm
        out_specs = [pl.BlockSpec((tm, d), lambda t, sl: (jnp.minimum(t, first_sample_tile - 1), 0)),
                     pl.BlockSpec((tm, d), lambda t, sl: (jnp.maximum(t - first_sample_tile, 0), 0))]
        out_shape = [jax.ShapeDtypeStruct((split_rows, d), F32), jax.ShapeDtypeStruct((m - split_rows, d), F32)]
    grid_spec = pltpu.PrefetchScalarGridSpec(
        num_scalar_prefetch=1,
        grid=(m // tm,),
        in_specs=[pl.BlockSpec(memory_space=pl.ANY),
                  pl.BlockSpec((tm, LANES), lambda t, sl: (t, 0)),
                  pl.BlockSpec((tm, d), lambda t, sl: (t, 0)),
                  pl.BlockSpec((1, d), lambda t, sl: (0, 0))],
        out_specs=out_specs,
        scratch_shapes=[pltpu.VMEM((2, TOP_K * tm, d), F32), pltpu.SemaphoreType.DMA((2,))],
    )
    return pl.pallas_call(
        functools.partial(_combine_kernel, tm=tm, first_sample_tile=first_sample_tile), grid_spec=grid_spec,
        out_shape=out_shape,
        compiler_params=_params("arbitrary"),
        name="moe_combine",
    )(token_slots, y_sorted, route, h, g_post.reshape(1, d))


def _routing_tables(route, n_experts, tm):
    m = route.shape[0]
    eid = jnp.concatenate([route[:, 2 + k].astype(I32) for k in range(TOP_K)])
    blk = LANES
    assert (TOP_K * m) % blk == 0 and TOP_K * m < 2 ** 24
    onehot = (eid[:, None] == jnp.arange(n_experts, dtype=I32)[None, :]).astype(F32).reshape(-1, blk, n_experts)
    before = jnp.tril(jnp.ones((blk, blk), F32), -1)
    within = jnp.einsum("ts,bse->bte", before, onehot)
    block_total = jnp.sum(onehot, axis=1)
    block_start = jnp.cumsum(block_total, axis=0) - block_total
    rank = jnp.sum((within + block_start[:, None, :]) * onehot, axis=-1).reshape(-1).astype(I32)
    counts = jnp.sum(block_total, axis=0).astype(I32)
    tiles_e = (counts + tm - 1) // tm
    tile_end = jnp.cumsum(tiles_e)
    start = (tile_end - tiles_e) * tm
    slot = start[eid] + rank
    n_tiles = (TOP_K * m + n_experts * (tm - 1)) // tm
    token = jnp.tile(jnp.arange(m, dtype=I32), TOP_K)
    slot_token = (jnp.arange(n_tiles * tm, dtype=I32) % m).at[slot].set(token)
    n_valid = tile_end[-1].astype(I32)
    tile_id = jnp.arange(n_tiles, dtype=I32)
    tile_src = jnp.minimum(tile_id, n_valid - 1)
    tile_expert = jnp.minimum(jnp.sum((tile_end[None, :] <= tile_src[:, None]).astype(I32), axis=1), n_experts - 1)
    following, cur = [], jnp.int32(-1)
    for e in reversed(range(n_experts)):
        following.append(cur)
        cur = jnp.where(tiles_e[e] > 0, jnp.int32(e), cur)
    following = jnp.stack(following[::-1])
    wraps = following < 0
    following = jnp.where(wraps, cur, following)
    rows_in_last = counts - (tiles_e - 1) * tm
    is_last = tile_src == tile_end[tile_expert] - 1
    half = jnp.logical_and(is_last, rows_in_last[tile_expert] <= tm // 2).astype(I32)
    tiles = (tile_expert, tile_src, n_valid.reshape(1), following[tile_expert], wraps[tile_expert].astype(I32), half)
    return slot_token, slot.reshape(TOP_K, m).astype(I32), tiles


def _pool_prompt_kernel(halo_ref, p_ref, w_ref, s_ref, o_ref, buf_ref, *, tiles_per_seq):
    i = pl.program_id(0) % tiles_per_seq
    t = p_ref.shape[0]
    gw = w_ref.shape[1]
    buf_ref[0:POOL_HALO, :] = jnp.where(i == 0, 0.0, halo_ref[...])
    buf_ref[POOL_HALO:POOL_HALO + t, :] = p_ref[...]
    pos = i * t + lax.broadcasted_iota(I32, (t, 1), 0)
    for g, w in enumerate(POOL_WINDOWS):
        c0 = g * gw
        cur = buf_ref[POOL_HALO:POOL_HALO + t, c0:c0 + gw]
        acc = cur
        for j in range(1, w):
            acc = acc + buf_ref[POOL_HALO - j:POOL_HALO - j + t, c0:c0 + gw]
        cnt = jnp.minimum(pos + 1, w).astype(F32)
        d = acc / cnt - cur
        y = jnp.dot(d.astype(BF16), w_ref[g], preferred_element_type=F32)
        o_ref[:, c0:c0 + gw] = (y * s_ref[:, c0:c0 + gw]).astype(o_ref.dtype)


def _pool_prompt(proj, n_rows, seq, w_pool, scale, t):
    c = scale.shape[0]
    halo_per_tile = t // POOL_HALO
    return pl.pallas_call(
        functools.partial(_pool_prompt_kernel, tiles_per_seq=seq // t),
        grid=(n_rows // t,),
        in_specs=[pl.BlockSpec((POOL_HALO, c), lambda i: (jnp.maximum(i * halo_per_tile - 1, 0), 0)),
                  pl.BlockSpec((t, c), lambda i: (i, 0)),
                  pl.BlockSpec(w_pool.shape, lambda i: (0, 0, 0)),
                  pl.BlockSpec((1, c), lambda i: (0, 0))],
        out_specs=pl.BlockSpec((t, c), lambda i: (i, 0)),
        out_shape=jax.ShapeDtypeStruct((n_rows, c), BF16),
        scratch_shapes=[pltpu.VMEM((POOL_HALO + t, c), F32)],
        compiler_params=_params("parallel"),
        name="pool_prompt",
    )(proj, proj, w_pool, scale.reshape(1, c))


def _pool_sample_kernel(ctx_ref, w_ref, s_ref, o_ref, *, n_ctx, pos0):
    gw = w_ref.shape[1]
    for t in range(o_ref.shape[0]):
        hi = n_ctx + t + 1
        for g, w in enumerate(POOL_WINDOWS):
            c0 = g * gw
            lo = max(hi - w, 0)
            acc = ctx_ref[lo, :, c0:c0 + gw]
            for r in range(lo + 1, hi):
                acc = acc + ctx_ref[r, :, c0:c0 + gw]
            d = acc / float(min(pos0 + t + 1, w)) - ctx_ref[hi - 1, :, c0:c0 + gw]
            y = jnp.dot(d.astype(BF16), w_ref[g], preferred_element_type=F32)
            o_ref[t, :, c0:c0 + gw] = (y * s_ref[:, c0:c0 + gw]).astype(o_ref.dtype)


def _pool_sample(ctx_t, n_ctx, pos0, w_pool, scale):
    total, n, c = ctx_t.shape
    return pl.pallas_call(
        functools.partial(_pool_sample_kernel, n_ctx=n_ctx, pos0=pos0),
        out_shape=jax.ShapeDtypeStruct((total - n_ctx, n, c), BF16),
        compiler_params=pltpu.CompilerParams(vmem_limit_bytes=VMEM_LIMIT_BYTES),
        name="pool_sample",
    )(ctx_t, w_pool, scale.reshape(1, c))


def _gmlp_kernel(u_ref, v_ref, w_ref, b_ref, o_ref):
    t = u_ref.shape[0]
    n_heads = w_ref.shape[0]
    ck = w_ref.shape[1]
    row = lax.broadcasted_iota(I32, (ck, ck), 0)
    col = lax.broadcasted_iota(I32, (ck, ck), 1)
    for h in range(n_heads):
        c0 = h * HEAD_DIM
        wm = jnp.where(col <= row, w_ref[h], 0.0).astype(BF16)
        bias = b_ref[:, h:h + 1]
        for k in range(t // ck):
            r0 = k * ck
            v = v_ref[r0:r0 + ck, c0:c0 + HEAD_DIM].astype(BF16)
            mixed = jnp.dot(wm, v, preferred_element_type=F32) + bias
            o_ref[r0:r0 + ck, c0:c0 + HEAD_DIM] = (u_ref[r0:r0 + ck, c0:c0 + HEAD_DIM] * mixed).astype(o_ref.dtype)


def _gmlp(proj, n_prompt_rows, w_pair, b_pair, t):
    m = proj.shape[0]
    c = proj.shape[1] // 3
    n_heads = w_pair.shape[1]
    first_sample_tile = n_prompt_rows // t
    which = lambda i: jnp.where(i >= first_sample_tile, 1, 0)
    return pl.pallas_call(
        _gmlp_kernel,
        grid=(m // t,),
        in_specs=[pl.BlockSpec((t, c), lambda i: (i, 1)),
                  pl.BlockSpec((t, c), lambda i: (i, 2)),
                  pl.BlockSpec((None, n_heads, GMLP_CHUNK, GMLP_CHUNK), lambda i: (which(i), 0, 0, 0)),
                  pl.BlockSpec((None, GMLP_CHUNK, n_heads), lambda i: (which(i), 0, 0))],
        out_specs=pl.BlockSpec((t, c), lambda i: (i, 0)),
        out_shape=jax.ShapeDtypeStruct((m, c), BF16),
        compiler_params=_params("parallel"),
        name="gmlp",
    )(proj, proj, w_pair, b_pair)


def _conv_prompt_kernel(xc_ref, bg_ref, cg_ref, xch_ref, cgh_ref, w_ref, o_ref, tail_ref, buf_ref, *, tiles_per_seq):
    i = pl.program_id(0) % tiles_per_seq
    t = xc_ref.shape[0]
    z = cg_ref[...] * xc_ref[...]
    buf_ref[0:CONV_HALO, :] = jnp.where(i == 0, 0.0, cgh_ref[...] * xch_ref[...])
    buf_ref[CONV_HALO:CONV_HALO + t, :] = z
    y = w_ref[0:1, :] * buf_ref[CONV_HALO - 2:CONV_HALO - 2 + t, :]
    for j in range(1, CONV_WIDTH):
        y = y + w_ref[j:j + 1, :] * buf_ref[CONV_HALO - 2 + j:CONV_HALO - 2 + j + t, :]
    o_ref[...] = (bg_ref[...] * y).astype(o_ref.dtype)
    tail_ref[...] = z[t - CONV_HALO:, :]


def _conv_prompt(proj, n_rows, seq, conv_w, t):
    c = conv_w.shape[1]
    halo_per_tile = t // CONV_HALO
    halo_row = lambda i: jnp.maximum(i * halo_per_tile - 1, 0)
    return pl.pallas_call(
        functools.partial(_conv_prompt_kernel, tiles_per_seq=seq // t),
        grid=(n_rows // t,),
        in_specs=[pl.BlockSpec((t, c), lambda i: (i, 0)),
                  pl.BlockSpec((t, c), lambda i: (i, 1)),
                  pl.BlockSpec((t, c), lambda i: (i, 2)),
                  pl.BlockSpec((CONV_HALO, c), lambda i: (halo_row(i), 0)),
                  pl.BlockSpec((CONV_HALO, c), lambda i: (halo_row(i), 2)),
                  pl.BlockSpec((CONV_WIDTH, c), lambda i: (0, 0))],
        out_specs=[pl.BlockSpec((t, c), lambda i: (i, 0)),
                   pl.BlockSpec((CONV_HALO, c), lambda i: (i, 0))],
        out_shape=[jax.ShapeDtypeStruct((n_rows, c), BF16),
                   jax.ShapeDtypeStruct((n_rows // t * CONV_HALO, c), F32)],
        scratch_shapes=[pltpu.VMEM((CONV_HALO + t, c), F32)],
        compiler_params=_params("parallel"),
        name="conv_prompt",
    )(proj, proj, proj, proj, proj, conv_w)


def _conv_sample_kernel(ctx_ref, xc_ref, bg_ref, cg_ref, w_ref, o_ref, z_ref):
    n_ctx = ctx_ref.shape[0]
    length = xc_ref.shape[0]
    rows = [ctx_ref[r] for r in range(n_ctx)]
    for t in range(length):
        z = cg_ref[t] * xc_ref[t]
        z_ref[t] = z
        rows.append(z)
    for t in range(length):
        y = w_ref[0:1, :] * rows[t]
        for j in range(1, CONV_WIDTH):
            y = y + w_ref[j:j + 1, :] * rows[t + j]
        o_ref[t] = (bg_ref[t] * y).astype(o_ref.dtype)


def _conv_sample(ctx_t, xc_t, bg_t, cg_t, conv_w):
    length, n, c = xc_t.shape
    return pl.pallas_call(
        _conv_sample_kernel,
        out_shape=[jax.ShapeDtypeStruct((length, n, c), BF16), jax.ShapeDtypeStruct((length, n, c), F32)],
        compiler_params=pltpu.CompilerParams(vmem_limit_bytes=VMEM_LIMIT_BYTES),
        name="conv_sample",
    )(ctx_t, xc_t, bg_t, cg_t, conv_w)


def _log_sigmoid(x):
    return jnp.minimum(x, 0.0) - jnp.log1p(jnp.exp(-jnp.abs(x)))


def _logf_kernel(f_ref, b_ref, lf_ref, c_ref, carry_ref, *, blocks_per_seq):
    i = pl.program_id(0)
    blk = f_ref.shape[0]
    lf = _log_sigmoid(f_ref[...] + b_ref[...])
    lf_ref[...] = lf

    @pl.when(i % blocks_per_seq == 0)
    def _():
        carry_ref[...] = jnp.zeros_like(carry_ref)

    row = lax.broadcasted_iota(I32, (blk, blk), 0)
    col = lax.broadcasted_iota(I32, (blk, blk), 1)
    tri = jnp.where(col <= row, 1.0, 0.0).astype(F32)
    c = jnp.dot(tri, lf, precision=HIGHEST, preferred_element_type=F32) + carry_ref[0:1, :]
    c_ref[...] = c
    carry_ref[...] = jnp.broadcast_to(c[blk - 1:blk, :], carry_ref.shape)


def _logf_cumsum(f_logit, b_forget, seq, blk):
    m = f_logit.shape[0]
    return pl.pallas_call(
        functools.partial(_logf_kernel, blocks_per_seq=seq // blk),
        grid=(m // blk,),
        in_specs=[pl.BlockSpec((blk, LANES), lambda i: (i, 0)), pl.BlockSpec((1, LANES), lambda i: (0, 0))],
        out_specs=[pl.BlockSpec((blk, LANES), lambda i: (i, 0)), pl.BlockSpec((blk, LANES), lambda i: (i, 0))],
        out_shape=[jax.ShapeDtypeStruct((m, LANES), F32), jax.ShapeDtypeStruct((m, LANES), F32)],
        scratch_shapes=[pltpu.VMEM((8, LANES), F32)],
        compiler_params=_params("arbitrary"),
        name="logf_cumsum",
    )(f_logit, b_forget)


FLASH_HEADS = 2


def _flash_kernel(q_ref, k_ref, v_ref, cc_ref, cr_ref, o_ref, k_bf, v_bf, *, tq):
    hp = pl.program_id(1)
    qb = pl.program_id(2)

    @pl.when(qb == 0)
    def _():
        k_bf[...] = k_ref[...].astype(BF16)
        v_bf[...] = v_ref[...].astype(BF16)

    lane = lax.broadcasted_iota(I32, (tq, LANES), 1)
    heads = []
    for hh in range(FLASH_HEADS):
        cols = slice(hh * HEAD_DIM, (hh + 1) * HEAD_DIM)
        q = (q_ref[:, cols] * (1.0 / math.sqrt(HEAD_DIM))).astype(BF16)
        cq = jnp.sum(jnp.where(lane == hp * FLASH_HEADS + hh, cc_ref[...], 0.0), axis=-1, keepdims=True)
        heads.append((cols, q, cq))

    def scores(hh, j):
        cols, q, cq = heads[hh]
        start = pl.multiple_of(j * tq, tq)
        k = k_bf[pl.ds(start, tq), cols]
        v = v_bf[pl.ds(start, tq), cols]
        s = lax.dot_general(q, k, (((1,), (1,)), ((), ())), preferred_element_type=F32)
        ck = cr_ref[hh, :, pl.ds(start, tq)]
        return s + (cq - ck), v

    def update(carry, s, v):
        m, l, acc = carry
        m_new = jnp.maximum(m, jnp.max(s, axis=-1, keepdims=True))
        alpha = jnp.exp(m - m_new)
        p = jnp.exp(s - m_new)
        l = alpha * l + jnp.sum(p, axis=-1, keepdims=True)
        acc = alpha * acc + jnp.dot(p.astype(BF16), v, preferred_element_type=F32)
        return m_new, l, acc

    def body(j, carries):
        return tuple(update(carries[hh], *scores(hh, j)) for hh in range(FLASH_HEADS))

    init = (jnp.full((tq, 1), MASKED, F32), jnp.zeros((tq, 1), F32), jnp.zeros((tq, HEAD_DIM), F32))
    carries = lax.fori_loop(0, qb, body, (init,) * FLASH_HEADS)
    row = lax.broadcasted_iota(I32, (tq, tq), 0)
    col = lax.broadcasted_iota(I32, (tq, tq), 1)
    for hh in range(FLASH_HEADS):
        s, v = scores(hh, qb)
        _, l, acc = update(carries[hh], jnp.where(col <= row, s, MASKED), v)
        o_ref[:, heads[hh][0]] = (acc / l).astype(o_ref.dtype)


def _flash_prompt(proj, c_col, c_row, n_seq, seq, n_heads, tq):
    assert n_heads % FLASH_HEADS == 0
    pairs = n_heads // FLASH_HEADS
    width = FLASH_HEADS * HEAD_DIM
    q_blk, k_blk, v_blk = 3 * pairs, 4 * pairs, 5 * pairs
    nqb = seq // tq
    return pl.pallas_call(
        functools.partial(_flash_kernel, tq=tq),
        grid=(n_seq, pairs, nqb),
        in_specs=[pl.BlockSpec((tq, width), lambda n, h, i: (n * nqb + i, q_blk + h)),
                  pl.BlockSpec((seq, width), lambda n, h, i: (n, k_blk + h)),
                  pl.BlockSpec((seq, width), lambda n, h, i: (n, v_blk + h)),
                  pl.BlockSpec((tq, LANES), lambda n, h, i: (n * nqb + i, 0)),
                  pl.BlockSpec((FLASH_HEADS, 1, seq), lambda n, h, i: (h, 0, n))],
        out_specs=pl.BlockSpec((tq, width), lambda n, h, i: (n * nqb + i, h)),
        out_shape=jax.ShapeDtypeStruct((n_seq * seq, n_heads * HEAD_DIM), BF16),
        scratch_shapes=[pltpu.VMEM((seq, width), BF16), pltpu.VMEM((seq, width), BF16)],
        compiler_params=_params("arbitrary", "arbitrary", "arbitrary"),
        name="fox_prompt",
    )(proj, proj, proj, c_col, c_row)


def _decay_of_later_rows(lf, later_groups, n_heads):
    group, width = lf.shape
    lane = lax.broadcasted_iota(I32, lf.shape, 1)
    incl, total = lf, lf
    shift = n_heads
    while shift < width:
        incl = incl + jnp.where(lane + shift < width, pltpu.roll(incl, width - shift, 1), 0.0)
        total = total + pltpu.roll(total, shift, 1)
        shift *= 2
    later_pages = jnp.zeros_like(lf)
    if group > 1:
        prow = lax.broadcasted_iota(I32, (group, group), 0)
        pcol = lax.broadcasted_iota(I32, (group, group), 1)
        later = jnp.where(pcol > prow, 1.0, 0.0).astype(F32)
        later_pages = jnp.dot(later, total, precision=HIGHEST, preferred_element_type=F32)
    return (incl - lf) + later_pages + later_groups, jnp.sum(total, axis=0, keepdims=True)


def _decode_kernel(pt_ref, q_ref, kn_ref, vn_ref, lfn_ref, *refs, n_heads, n_q, group, n_par):
    pages = 3 * group * n_par
    o_ref, m_ref, l_ref, acc_ref, later_ref = refs[pages:]
    j = pl.program_id(1)
    rows = n_heads * n_q
    width = refs[0].shape[0] * n_heads

    @pl.when(j == 0)
    def _():
        m_ref[...] = jnp.full_like(m_ref, MASKED)
        l_ref[...] = jnp.zeros_like(l_ref)
        acc_ref[...] = jnp.zeros_like(acc_ref)
        later_ref[...] = jnp.zeros_like(later_ref)

    r_i = lax.broadcasted_iota(I32, (rows, LANES), 0)
    c_i = lax.broadcasted_iota(I32, (rows, LANES), 1)
    sel = (c_i % n_heads == r_i // n_q) & (c_i // n_heads <= r_i % n_q) & (c_i < n_q * n_heads)
    r_w = lax.broadcasted_iota(I32, (rows, group * width), 0)
    c_w = lax.broadcasted_iota(I32, (rows, group * width), 1)
    same_head = c_w % n_heads == r_w // n_q

    def update(state, s, v):
        m_old, l, acc = state
        m_new = jnp.maximum(m_old, jnp.max(s, axis=-1, keepdims=True))
        alpha = jnp.exp(m_old - m_new)
        p = jnp.exp(s - m_new)
        l = alpha * l + jnp.sum(p, axis=-1, keepdims=True)
        acc = alpha * acc + jnp.dot(p.astype(BF16), v, preferred_element_type=F32)
        return m_new, l, acc

    def load_state(b):
        return m_ref[b, :, 0:1], l_ref[b, :, 0:1], acc_ref[b]

    def store_state(b, state):
        m, l, acc = state
        m_ref[b] = jnp.broadcast_to(m, m_ref.shape[1:])
        l_ref[b] = jnp.broadcast_to(l, l_ref.shape[1:])
        acc_ref[b] = acc

    states = [load_state(b) for b in range(n_par)]
    later = [later_ref[b] for b in range(n_par)]
    queries = []
    for b in range(n_par):
        base = 3 * group * b
        k_refs, v_refs = refs[base:base + group], refs[base + group:base + 2 * group]
        lf_refs = refs[base + 2 * group:base + 3 * group]
        lf = jnp.concatenate([r[...] for r in lf_refs], axis=0)
        decay, group_total = _decay_of_later_rows(lf, later[b][0:1, :], n_heads)
        later[b] = later[b] + group_total

        q = (q_ref[b] * (1.0 / math.sqrt(HEAD_DIM))).astype(BF16)
        lfn = lfn_ref[b]
        cq = jnp.sum(jnp.where(sel, jnp.broadcast_to(lfn, (rows, LANES)), 0.0), axis=-1, keepdims=True)
        queries.append((q, lfn, cq))

        k = jnp.concatenate([r[...].reshape(width, HEAD_DIM).astype(BF16) for r in k_refs], axis=0)
        v = jnp.concatenate([r[...].reshape(width, HEAD_DIM).astype(BF16) for r in v_refs], axis=0)
        decay_row = jnp.concatenate([decay[p:p + 1, :] for p in range(group)], axis=1)
        s = lax.dot_general(q, k, (((1,), (1,)), ((), ())), preferred_element_type=F32)
        s = jnp.where(same_head, s + decay_row + cq, MASKED)
        states[b] = update(states[b], s, v)
    for b in range(n_par):
        store_state(b, states[b])
        later_ref[b] = later[b]

    @pl.when(j == pl.num_programs(1) - 1)
    def _():
        for b in range(n_par):
            q, lfn, cq = queries[b]
            kn = kn_ref[b].astype(BF16)
            vn = vn_ref[b].astype(BF16)
            sn = lax.dot_general(q, kn, (((1,), (1,)), ((), ())), preferred_element_type=F32)
            lf8 = jnp.broadcast_to(lfn, (8, LANES))
            lane8 = lax.broadcasted_iota(I32, (8, LANES), 1)
            csum = lf8
            shift = n_heads
            while shift < n_q * n_heads:
                csum = csum + jnp.where(lane8 >= shift, pltpu.roll(csum, shift, 1), 0.0)
                shift *= 2
            _, l, acc = update(states[b], jnp.where(sel, sn + (cq - csum[0:1, :]), MASKED), vn)
            o_ref[b] = acc / l


def _decode_attention(q_hq, cache_k, cache_v, logf_pages, layer, page_table, k_new, v_new, lf_new, n_heads, n_q):
    n, n_pages = page_table.shape
    assert n_pages >= 1
    page = cache_k.shape[2]
    rows = n_heads * n_q
    width = page * n_heads
    group = max(g for g in (1, 2, 4, 8) if n_pages % g == 0)
    n_groups = n_pages // group
    n_par = 2 if n % 2 == 0 else 1

    def physical_page(i, j, pt, b, p):
        return pt[(i * n_par + b) * n_pages + (n_groups - 1 - j) * group + p]

    def page_spec(b, p):
        return pl.BlockSpec((None, None, page, n_heads, HEAD_DIM),
                            lambda i, j, pt: (layer, physical_page(i, j, pt, b, p), 0, 0, 0))

    def logf_spec(b, p):
        return pl.BlockSpec((None, 1, width), lambda i, j, pt: (physical_page(i, j, pt, b, p), 0, 0))

    per_seq = lambda shape: pl.BlockSpec((n_par,) + shape, lambda i, j, pt: (i, 0, 0))
    page_specs, page_args = [], []
    for b in range(n_par):
        page_specs += [page_spec(b, p) for p in range(group)] * 2 + [logf_spec(b, p) for p in range(group)]
        page_args += [cache_k] * group + [cache_v] * group + [logf_pages] * group
    grid_spec = pltpu.PrefetchScalarGridSpec(
        num_scalar_prefetch=1,
        grid=(n // n_par, n_groups),
        in_specs=[per_seq((rows, HEAD_DIM)), per_seq((LANES, HEAD_DIM)), per_seq((LANES, HEAD_DIM)),
                  per_seq((1, LANES))] + page_specs,
        out_specs=per_seq((rows, HEAD_DIM)),
        scratch_shapes=[pltpu.VMEM((n_par, rows, LANES), F32), pltpu.VMEM((n_par, rows, LANES), F32),
                        pltpu.VMEM((n_par, rows, HEAD_DIM), F32), pltpu.VMEM((n_par, 8, width), F32)],
    )
    return pl.pallas_call(
        functools.partial(_decode_kernel, n_heads=n_heads, n_q=n_q, group=group, n_par=n_par),
        grid_spec=grid_spec,
        out_shape=jax.ShapeDtypeStruct((n, rows, HEAD_DIM), F32),
        compiler_params=_params("parallel", "arbitrary"),
        name="fox_decode",
    )(page_table.reshape(-1), q_hq, k_new, v_new, lf_new, *page_args)


def kernel(x_prompt, x_sample, state_pool, state_conv, cache_k, cache_v, cache_logf, page_table, e_norm_mix_pre, e_norm_mix_post, e_norm_ffn_pre, e_norm_ffn_post, e_w_in, e_w_pool, e_pool_scale, e_w_spatial, e_b_spatial, e_w_out, e_ffn_gate, e_ffn_up, e_ffn_down, o_norm_mix_pre, o_norm_mix_post, o_norm_ffn_pre, o_norm_ffn_post, o_w_in, o_conv_w, o_b_forget, o_w_out, o_w_router, o_exp_gate, o_exp_up, o_exp_down):
    n_p, seq, d = x_prompt.shape
    n_s, dec, _ = x_sample.shape
    mp, ms = n_p * seq, n_s * dec
    m = mp + ms
    c = d // 2
    n_heads = c // HEAD_DIM
    n_pages = page_table.shape[1]
    page = cache_k.shape[2]
    past_len = n_pages * page
    n_layers = e_w_in.shape[0] + o_w_in.shape[0]
    n_experts = o_w_router.shape[2]
    assert ms % GMLP_CHUNK == 0 and seq % GMLP_CHUNK == 0 and past_len % GMLP_CHUNK == 0
    assert GMLP_CHUNK % min(dec, GMLP_CHUNK) == 0 and dec % min(dec, GMLP_CHUNK) == 0
    assert dec * n_heads <= LANES and dec >= CONV_WIDTH - 1

    tm = _tile(math.gcd(mp, ms), 512, LANES)
    t_seq = _tile(math.gcd(seq, tm), 512, LANES)
    tq = _tile(seq, 512, LANES)
    tm_moe = 512
    dense_tiles = _dense_tiles(m // tm)

    h = (x_prompt.reshape(mp, d), x_sample.reshape(ms, d))
    xn = _norm_cast(h, e_norm_mix_pre[0], tm)
    out_rows = None

    pool_p, pool_s, gv_s, conv_p, conv_s = [], [], [], [], []
    kp_l, vp_l, lp_l, ks_l, vs_l, ls_l = [], [], [], [], [], []

    def time_major(x2d):
        return jnp.transpose(x2d.reshape(n_s, dec, -1), (1, 0, 2))

    def row_major(x3d):
        return jnp.transpose(x3d, (1, 0, 2)).reshape(ms, -1)

    for layer in range(n_layers):
        i = layer // 2
        if layer % 2 == 0:
            g_next = e_norm_ffn_pre[i]
            proj = _mm(xn, e_w_in, 3 * c, tm, _tile(3 * c, 1024, LANES), w_layer=i)
            p_s = proj[mp:, :c].reshape(n_s, dec, c)
            ctx = jnp.concatenate([state_pool[i].astype(F32), p_s], axis=1)
            n_ctx = state_pool.shape[2]
            w_pool = e_w_pool[i].astype(BF16)
            a_p = _pool_prompt(proj, mp, seq, w_pool, e_pool_scale[i], t_seq)
            a_s = row_major(_pool_sample(jnp.transpose(ctx, (1, 0, 2)), n_ctx, past_len, w_pool, e_pool_scale[i]))
            cl = min(dec, GMLP_CHUNK)
            reps = GMLP_CHUNK // cl
            w_samp = jnp.einsum("ab,hts->hatbs", jnp.eye(reps, dtype=F32),
                                e_w_spatial[i][:, :cl, :cl]).reshape(n_heads, GMLP_CHUNK, GMLP_CHUNK)
            b_samp = jnp.tile(e_b_spatial[i][:, :cl], (1, reps))
            w_pair = jnp.stack([e_w_spatial[i], w_samp])
            b_pair = jnp.stack([e_b_spatial[i].T, b_samp.T])
            b = _gmlp(proj, mp, w_pair, b_pair, tm)
            h, xn = _proj_out((a_p, a_s), (b,), e_w_out[i].astype(BF16), h, e_norm_mix_post[i], g_next, tm, mp)
            g_after = o_norm_mix_pre[i] if layer + 1 < n_layers else jnp.ones((d,), F32)
            mix = _swiglu_grouped(xn, e_ffn_gate[i][None], e_ffn_up[i][None], e_ffn_down[i][None], dense_tiles, tm)
            h, xn = _norm_residual(mix, h, e_norm_ffn_post[i], g_after, tm)
            h = (h,)
            pool_p.append(jnp.stack([proj[(s + 1) * seq - n_ctx:(s + 1) * seq, :c] for s in range(n_p)]))
            pool_s.append(ctx[:, ctx.shape[1] - n_ctx:])
            gv_s.append(proj[mp:, 2 * c:].reshape(n_s, dec, c))
        else:
            w_in = o_w_in[i]
            proj, k_p, v_p = _mm(xn, o_w_in, 6 * c, tm, c, copy_cols=(4, 5), copy_rows=mp, w_layer=i)
            w_f = jnp.zeros((d, LANES), F32).at[:, :n_heads].set(w_in[:, 6 * c:])
            f_logit = _mm(xn, w_f, LANES, tm, LANES)
            b_f = jnp.zeros((1, LANES), F32).at[0, :n_heads].set(o_b_forget[i])
            logf, csum = _logf_cumsum(f_logit, b_f, seq, _tile(math.gcd(seq, m), 512, LANES))
            c_p, z_tail = _conv_prompt(proj, mp, seq, o_conv_w[i], t_seq)
            z_tail = z_tail.reshape(n_p, seq // t_seq, CONV_HALO, c)[:, -1, CONV_HALO - (CONV_WIDTH - 1):]
            xs = proj[mp:]
            conv_ctx = jnp.transpose(state_conv[i].astype(F32), (1, 0, 2))
            c_s, z_s = _conv_sample(conv_ctx, time_major(xs[:, :c]), time_major(xs[:, c:2 * c]),
                                    time_major(xs[:, 2 * c:3 * c]), o_conv_w[i])
            z_all = jnp.concatenate([conv_ctx, z_s], axis=0)
            c_row = csum[:mp, :n_heads].T.reshape(n_heads, 1, mp)
            att_p = _flash_prompt(proj, csum, c_row, n_p, seq, n_heads, tq)
            k_s = xs[:, 4 * c:5 * c]
            v_s = xs[:, 5 * c:6 * c]
            q_hq = jnp.transpose(xs[:, 3 * c:4 * c].reshape(n_s, dec, n_heads, HEAD_DIM), (0, 2, 1, 3))
            q_hq = q_hq.reshape(n_s, n_heads * dec, HEAD_DIM)
            pad_rows = LANES - dec * n_heads
            k_new = jnp.pad(k_s.reshape(n_s, dec * n_heads, HEAD_DIM), ((0, 0), (0, pad_rows), (0, 0)))
            v_new = jnp.pad(v_s.reshape(n_s, dec * n_heads, HEAD_DIM), ((0, 0), (0, pad_rows), (0, 0)))
            lf_new = jnp.pad(logf[mp:, :n_heads].reshape(n_s, 1, dec * n_heads), ((0, 0), (0, 0), (0, pad_rows)))
            logf_pages = cache_logf[i].astype(F32).reshape(cache_logf.shape[1], 1, page * n_heads)
            att_s = _decode_attention(q_hq, cache_k, cache_v, logf_pages, i, page_table, k_new, v_new, lf_new,
                                      n_heads, dec)
            att_s = jnp.transpose(att_s.reshape(n_s, n_heads, dec, HEAD_DIM), (0, 2, 1, 3)).reshape(ms, c)
            h, _, route = _proj_out((c_p, row_major(c_s)), (att_p, att_s.astype(BF16)), o_w_out[i].astype(BF16), h,
                                    o_norm_mix_post[i], o_norm_ffn_pre[i], tm, mp, w_router=o_w_router[i])
            slot_token, token_slots, expert_tiles = _routing_tables(route, n_experts, tm_moe)
            x_sorted = _dispatch(h, o_norm_ffn_pre[i], slot_token, tm_moe)
            y_sorted = _swiglu_grouped(x_sorted, o_exp_gate[i], o_exp_up[i], o_exp_down[i], expert_tiles, tm_moe)
            t_comb = _tile(math.gcd(mp, ms), 256)
            if layer + 1 < n_layers:
                h = (_combine(y_sorted, token_slots, route, h, o_norm_ffn_post[i], t_comb),)
                xn = _norm_cast(h, e_norm_mix_pre[i + 1], tm)
            else:
                out_rows = _combine(y_sorted, token_slots, route, h, o_norm_ffn_post[i], t_comb, split_rows=mp)
            conv_p.append(z_tail)
            conv_s.append(jnp.transpose(z_all[z_all.shape[0] - (CONV_WIDTH - 1):], (1, 0, 2)))
            kp_l.append(k_p.reshape(n_p, seq, n_heads, HEAD_DIM))
            vp_l.append(v_p.reshape(n_p, seq, n_heads, HEAD_DIM))
            lp_l.append(logf[:mp, :n_heads].reshape(n_p, seq, n_heads))
            ks_l.append(k_s.reshape(n_s, dec, n_heads, HEAD_DIM))
            vs_l.append(v_s.reshape(n_s, dec, n_heads, HEAD_DIM))
            ls_l.append(logf[mp:, :n_heads].reshape(n_s, dec, n_heads))

    if out_rows is None:
        out_rows = (h[0][:mp], h[0][mp:])
    return (out_rows[0].reshape(n_p, seq, d), out_rows[1].reshape(n_s, dec, d),
            jnp.stack(pool_p), jnp.stack(pool_s), jnp.stack(gv_s), jnp.stack(conv_p), jnp.stack(conv_s),
            jnp.stack(kp_l), jnp.stack(vp_l), jnp.stack(lp_l), jnp.stack(ks_l), jnp.stack(vs_l), jnp.stack(ls_l))
```

```python
import functools
import math

import jax
import jax.numpy as jnp
from jax import lax
from jax.experimental import pallas as pl
from jax.experimental.pallas import tpu as pltpu

F32, BF16, I32 = jnp.float32, jnp.bfloat16, jnp.int32
RMS_EPS = 1e-6
POOL_WINDOWS = (2, 4, 8, 16)
POOL_HALO = 16
CONV_WIDTH = 3
CONV_HALO = 8
HEAD_DIM = 128
GMLP_CHUNK = 128
TOP_K = 2
LANES = 128
MASKED = -1e30
VMEM_LIMIT_BYTES = 56 * 1024 * 1024
HIGHEST = lax.Precision.HIGHEST


def _params(*semantics):
    return pltpu.CompilerParams(dimension_semantics=semantics, vmem_limit_bytes=VMEM_LIMIT_BYTES)


def _tile(n, pref, mult=8):
    t = min(n, pref)
    t -= t % mult
    while t > mult and n % t:
        t -= mult
    assert t > 0 and n % t == 0, (n, pref, mult)
    return t


def _rms(x, g):
    return x * lax.rsqrt(jnp.mean(x * x, axis=-1, keepdims=True) + RMS_EPS) * g


def _row_specs(parts, tm):
    if len(parts) == 1:
        return [pl.BlockSpec((tm, parts[0].shape[1]), lambda i, *_: (i, 0))]
    first_sample_tile = parts[0].shape[0] // tm
    return [pl.BlockSpec((tm, parts[0].shape[1]), lambda i, *_: (jnp.minimum(i, first_sample_tile - 1), 0)),
            pl.BlockSpec((tm, parts[1].shape[1]), lambda i, *_: (jnp.maximum(i - first_sample_tile, 0), 0))]


def _read_rows(refs, first_sample_tile, tile):
    if len(refs) == 1:
        return refs[0][...]
    return jnp.where(tile < first_sample_tile, refs[0][...], refs[1][...])


def _norm_cast_kernel(*refs, first_sample_tile):
    g_ref, o_ref = refs[-2:]
    x = _read_rows(refs[:-2], first_sample_tile, pl.program_id(0))
    o_ref[...] = _rms(x, g_ref[...]).astype(o_ref.dtype)


def _norm_cast(x_parts, g, tm):
    m = sum(p.shape[0] for p in x_parts)
    d = x_parts[0].shape[1]
    return pl.pallas_call(
        functools.partial(_norm_cast_kernel, first_sample_tile=x_parts[0].shape[0] // tm),
        grid=(m // tm,),
        in_specs=_row_specs(x_parts, tm) + [pl.BlockSpec((1, d), lambda i: (0, 0))],
        out_specs=pl.BlockSpec((tm, d), lambda i: (i, 0)),
        out_shape=jax.ShapeDtypeStruct((m, d), BF16),
        compiler_params=_params("parallel"),
        name="norm_cast",
    )(*x_parts, g.reshape(1, d))


def _mm_kernel(x_ref, w_ref, o_ref, *rest, copy_cols, copy_tiles):
    copy_refs, w_bf = rest[:-1], rest[-1]

    @pl.when(pl.program_id(1) == 0)
    def _():
        w_bf[...] = w_ref[...].astype(BF16)

    y = jnp.dot(x_ref[...], w_bf[...], preferred_element_type=F32)
    o_ref[...] = y
    for ref, col in zip(copy_refs, copy_cols):
        @pl.when(jnp.logical_and(pl.program_id(0) == col, pl.program_id(1) < copy_tiles))
        def _(ref=ref):
            ref[...] = y


def _mm(x, w, n_cols, tm, tn, copy_cols=(), copy_rows=0, w_layer=0):
    m, k = x.shape
    n = n_cols
    copy_tiles = copy_rows // tm
    assert copy_rows % tm == 0 and n % tn == 0
    if w.ndim == 3:
        assert n <= w.shape[2]
        w_spec = pl.BlockSpec((None, k, tn), lambda j, i: (w_layer, 0, j))
    else:
        assert n <= w.shape[1]
        w_spec = pl.BlockSpec((k, tn), lambda j, i: (0, j))

    def copy_spec(col):
        def index(j, i):
            row = jnp.where(j < col, 0, jnp.where(j > col, copy_tiles - 1, jnp.minimum(i, copy_tiles - 1)))
            return (row, 0)
        return pl.BlockSpec((tm, tn), index)

    out = pl.pallas_call(
        functools.partial(_mm_kernel, copy_cols=tuple(copy_cols), copy_tiles=copy_tiles),
        grid=(n // tn, m // tm),
        in_specs=[pl.BlockSpec((tm, k), lambda j, i: (i, 0)), w_spec],
        out_specs=[pl.BlockSpec((tm, tn), lambda j, i: (i, j))] + [copy_spec(col) for col in copy_cols],
        out_shape=[jax.ShapeDtypeStruct((m, n), F32)] + [jax.ShapeDtypeStruct((copy_rows, tn), F32)] * len(copy_cols),
        scratch_shapes=[pltpu.VMEM((k, tn), BF16)],
        compiler_params=_params("arbitrary", "arbitrary"),
        name="proj_in",
    )(x, w)
    return out if copy_cols else out[0]


def _top2(logits, n_experts):
    lane = lax.broadcasted_iota(I32, logits.shape, 1)
    neg_inf = jnp.float32(-jnp.inf)
    l1 = jnp.where(lane < n_experts, logits, neg_inf)
    m1 = jnp.max(l1, axis=-1, keepdims=True)
    i1 = jnp.min(jnp.where(l1 == m1, lane, LANES), axis=-1, keepdims=True)
    l2 = jnp.where(lane == i1, neg_inf, l1)
    m2 = jnp.max(l2, axis=-1, keepdims=True)
    i2 = jnp.min(jnp.where(l2 == m2, lane, LANES), axis=-1, keepdims=True)
    e2 = jnp.exp(m2 - m1)
    g1 = 1.0 / (1.0 + e2)
    g2 = e2 / (1.0 + e2)
    return jnp.where(lane == 0, g1,
                     jnp.where(lane == 1, g2,
                               jnp.where(lane == 2, i1.astype(F32),
                                         jnp.where(lane == 3, i2.astype(F32), 0.0))))


def _proj_out_kernel(*refs, n_experts, layout, first_sample_tile):
    na, nb, nh = layout
    tile = pl.program_id(0)
    a = _read_rows(refs[:na], first_sample_tile, tile)
    b = _read_rows(refs[na:na + nb], first_sample_tile, tile)
    h = _read_rows(refs[na + nb:na + nb + nh], first_sample_tile, tile)
    w_ref, gpost_ref, gnext_ref, *rest = refs[na + nb + nh:]
    half = a.shape[1]
    m = (jnp.dot(a, w_ref[:half, :], preferred_element_type=F32)
         + jnp.dot(b, w_ref[half:, :], preferred_element_type=F32))
    hn = h + _rms(m, gpost_ref[...])
    xn = _rms(hn, gnext_ref[...])
    xn_hi = xn.astype(BF16)
    if n_experts:
        wr_ref, hn_ref, xn_ref, route_ref = rest
        xn_lo = (xn - xn_hi.astype(F32)).astype(BF16)
        both = jnp.dot(xn_hi, wr_ref[...], preferred_element_type=F32)
        logits = (both[:, :LANES] + jnp.dot(xn_lo, wr_ref[:, :LANES], preferred_element_type=F32)
                  + both[:, LANES:])
        route_ref[...] = _top2(logits, n_experts)
    else:
        hn_ref, xn_ref = rest
    hn_ref[...] = hn
    xn_ref[...] = xn_hi


def _proj_out(a_parts, b_parts, w, h_parts, g_post, g_next, tm, n_prompt_rows, w_router=None):
    m = sum(p.shape[0] for p in h_parts)
    d = h_parts[0].shape[1]
    half = a_parts[0].shape[1]
    n_experts = 0 if w_router is None else w_router.shape[1]
    first_sample_tile = n_prompt_rows // tm
    assert all(len(p) == 1 or p[0].shape[0] == n_prompt_rows for p in (a_parts, b_parts, h_parts))
    row = lambda i: (i, 0)
    fixed = lambda i: (0, 0)
    in_specs = (_row_specs(a_parts, tm) + _row_specs(b_parts, tm) + _row_specs(h_parts, tm)
                + [pl.BlockSpec((2 * half, d), fixed), pl.BlockSpec((1, d), fixed), pl.BlockSpec((1, d), fixed)])
    out_specs = [pl.BlockSpec((tm, d), row), pl.BlockSpec((tm, d), row)]
    out_shape = [jax.ShapeDtypeStruct((m, d), F32), jax.ShapeDtypeStruct((m, d), BF16)]
    args = [*a_parts, *b_parts, *h_parts, w, g_post.reshape(1, d), g_next.reshape(1, d)]
    if n_experts:
        wr = jnp.zeros((d, LANES), F32).at[:, :n_experts].set(w_router.astype(F32))
        wr_hi = wr.astype(BF16)
        wr_lo = (wr - wr_hi.astype(F32)).astype(BF16)
        in_specs.append(pl.BlockSpec((d, 2 * LANES), fixed))
        out_specs.append(pl.BlockSpec((tm, LANES), row))
        out_shape.append(jax.ShapeDtypeStruct((m, LANES), F32))
        args.append(jnp.concatenate([wr_hi, wr_lo], axis=1))
    return pl.pallas_call(
        functools.partial(_proj_out_kernel, n_experts=n_experts, first_sample_tile=first_sample_tile,
                          layout=(len(a_parts), len(b_parts), len(h_parts))),
        grid=(m // tm,),
        in_specs=in_specs, out_specs=out_specs, out_shape=out_shape,
        compiler_params=_params("parallel"),
        name="proj_out",
    )(*args)


def _norm_residual_kernel(m_ref, h_ref, gpost_ref, gnext_ref, hn_ref, xn_ref):
    hn = h_ref[...] + _rms(m_ref[...], gpost_ref[...])
    hn_ref[...] = hn
    xn_ref[...] = _rms(hn, gnext_ref[...]).astype(xn_ref.dtype)


def _norm_residual(mix, h, g_post, g_next, tm):
    m, d = h.shape
    row = lambda i: (i, 0)
    fixed = lambda i: (0, 0)
    return pl.pallas_call(
        _norm_residual_kernel,
        grid=(m // tm,),
        in_specs=[pl.BlockSpec((tm, d), row), pl.BlockSpec((tm, d), row),
                  pl.BlockSpec((1, d), fixed), pl.BlockSpec((1, d), fixed)],
        out_specs=[pl.BlockSpec((tm, d), row), pl.BlockSpec((tm, d), row)],
        out_shape=[jax.ShapeDtypeStruct((m, d), F32), jax.ShapeDtypeStruct((m, d), BF16)],
        compiler_params=_params("parallel"),
        name="norm_residual",
    )(mix, h, g_post.reshape(1, d), g_next.reshape(1, d))


def _ring_step(te_ref, nv_ref, nxt_ref, wrap_ref, cnt_ref, copies, consume):
    j = pl.program_id(0)
    t = pl.program_id(1)

    @pl.when(jnp.logical_and(j == 0, t == 0))
    def _():
        cnt_ref[0] = 0
        for c in copies(te_ref[0], 0, 0):
            c.start()

    group_start = jnp.logical_or(t == 0, te_ref[t] != te_ref[jnp.maximum(t - 1, 0)])

    @pl.when(jnp.logical_and(group_start, t < nv_ref[0]))
    def _():
        slot = cnt_ref[0] % 2
        for c in copies(te_ref[t], j, slot):
            c.wait()
        consume(slot)
        j_next = j + wrap_ref[t]

        @pl.when(j_next < pl.num_programs(0))
        def _():
            for c in copies(nxt_ref[t], j_next, 1 - slot):
                c.start()
        cnt_ref[0] = cnt_ref[0] + 1


def _ffn_up_kernel(te_ref, src_ref, nv_ref, nxt_ref, wrap_ref, parts_ref, x_ref, wg_hbm, wu_hbm, o_ref,
                   wbuf, wg_bf, wu_bf, sem, cnt_ref):
    t = pl.program_id(1)
    tf = wg_bf.shape[1]

    def copies(e, j, slot):
        cols = pl.ds(pl.multiple_of(j * tf, tf), tf)
        return (pltpu.make_async_copy(wg_hbm.at[e, :, cols], wbuf.at[slot, 0], sem.at[slot, 0]),
                pltpu.make_async_copy(wu_hbm.at[e, :, cols], wbuf.at[slot, 1], sem.at[slot, 1]))

    def consume(slot):
        wg_bf[...] = wbuf[slot, 0].astype(BF16)
        wu_bf[...] = wbuf[slot, 1].astype(BF16)

    _ring_step(te_ref, nv_ref, nxt_ref, wrap_ref, cnt_ref, copies, consume)

    def rows(r):
        x = x_ref[r, :]
        g = jnp.dot(x, wg_bf[...], preferred_element_type=F32)
        u = jnp.dot(x, wu_bf[...], preferred_element_type=F32)
        o_ref[r, :] = (g * jax.nn.sigmoid(g) * u).astype(o_ref.dtype)

    _tile_rows(t, nv_ref, parts_ref, o_ref, rows)


TILE_PARTS = 4


def _tile_rows(t, nv_ref, parts_ref, o_ref, rows):
    tm = o_ref.shape[0]
    part = tm // TILE_PARTS
    valid = t < nv_ref[0]

    for n_parts in range(1, TILE_PARTS + 1):
        @pl.when(jnp.logical_and(valid, parts_ref[t] == n_parts))
        def _(n_parts=n_parts):
            rows(slice(0, n_parts * part))
            if n_parts < TILE_PARTS:
                o_ref[n_parts * part:, :] = jnp.zeros((tm - n_parts * part, o_ref.shape[1]), o_ref.dtype)

    @pl.when(jnp.logical_not(valid))
    def _():
        o_ref[...] = jnp.zeros_like(o_ref)


def _ffn_up(x, w_gate, w_up, tiles, tm, tf):
    s, k = x.shape
    f = w_gate.shape[2]
    n_tiles = s // tm
    grid_spec = pltpu.PrefetchScalarGridSpec(
        num_scalar_prefetch=len(tiles),
        grid=(f // tf, n_tiles),
        in_specs=[pl.BlockSpec((tm, k), lambda j, t, te, src, *_: (src[t], 0)),
                  pl.BlockSpec(memory_space=pl.ANY), pl.BlockSpec(memory_space=pl.ANY)],
        out_specs=pl.BlockSpec((tm, tf), lambda j, t, *_: (t, j)),
        scratch_shapes=[pltpu.VMEM((2, 2, k, tf), F32), pltpu.VMEM((k, tf), BF16), pltpu.VMEM((k, tf), BF16),
                        pltpu.SemaphoreType.DMA((2, 2)), pltpu.SMEM((1,), I32)],
    )
    return pl.pallas_call(
        _ffn_up_kernel, grid_spec=grid_spec,
        out_shape=jax.ShapeDtypeStruct((s, f), BF16),
        compiler_params=_params("arbitrary", "arbitrary"),
        name="ffn_up",
    )(*tiles, x, w_gate, w_up)


def _ffn_down_kernel(te_ref, src_ref, nv_ref, nxt_ref, wrap_ref, parts_ref, a_ref, wd_hbm, o_ref,
                     wbuf, wd_bf, sem, cnt_ref):
    t = pl.program_id(1)
    tn = wd_bf.shape[1]

    def copies(e, j, slot):
        cols = pl.ds(pl.multiple_of(j * tn, tn), tn)
        return (pltpu.make_async_copy(wd_hbm.at[e, :, cols], wbuf.at[slot], sem.at[slot]),)

    def consume(slot):
        wd_bf[...] = wbuf[slot].astype(BF16)

    _ring_step(te_ref, nv_ref, nxt_ref, wrap_ref, cnt_ref, copies, consume)

    def rows(r):
        o_ref[r, :] = jnp.dot(a_ref[r, :], wd_bf[...], preferred_element_type=F32)

    _tile_rows(t, nv_ref, parts_ref, o_ref, rows)


def _ffn_down(a, w_down, tiles, tm, tn):
    s, f = a.shape
    d = w_down.shape[2]
    n_tiles = s // tm
    grid_spec = pltpu.PrefetchScalarGridSpec(
        num_scalar_prefetch=len(tiles),
        grid=(d // tn, n_tiles),
        in_specs=[pl.BlockSpec((tm, f), lambda j, t, te, src, *_: (src[t], 0)),
                  pl.BlockSpec(memory_space=pl.ANY)],
        out_specs=pl.BlockSpec((tm, tn), lambda j, t, *_: (t, j)),
        scratch_shapes=[pltpu.VMEM((2, f, tn), F32), pltpu.VMEM((f, tn), BF16),
                        pltpu.SemaphoreType.DMA((2,)), pltpu.SMEM((1,), I32)],
    )
    return pl.pallas_call(
        _ffn_down_kernel, grid_spec=grid_spec,
        out_shape=jax.ShapeDtypeStruct((s, d), F32),
        compiler_params=_params("arbitrary", "arbitrary"),
        name="ffn_down",
    )(*tiles, a, w_down)


def _swiglu_grouped(x, w_gate, w_up, w_down, tiles, tm):
    tf = _tile(w_gate.shape[2], 512, LANES)
    tn = _tile(w_down.shape[2], 512, LANES)
    act = _ffn_up(x, w_gate, w_up, tiles, tm, tf)
    return _ffn_down(act, w_down, tiles, tm, tn)


def _dense_tiles(n_tiles):
    zeros = jnp.zeros((n_tiles,), I32)
    return (zeros, jnp.arange(n_tiles, dtype=I32), jnp.full((1,), n_tiles, I32), zeros, jnp.ones((n_tiles,), I32),
            jnp.full((n_tiles,), TILE_PARTS, I32))


GATHER_UNROLL = 8


def _row_copy(src_hbm, row, buf, slot, r, sem):
    return pltpu.make_async_copy(src_hbm.at[pl.ds(row, 1), :], buf.at[slot, pl.ds(r, 1), :], sem.at[slot])


def _wait_slot(src_hbm, buf, slot, sem):
    pltpu.make_async_copy(src_hbm.at[pl.ds(0, buf.shape[1]), :], buf.at[slot], sem.at[slot]).wait()


def _dispatch_kernel(tok_ref, h_hbm, g_ref, o_ref, buf, sem, *, tm):
    t = pl.program_id(0)
    n_t = pl.num_programs(0)

    def issue(tile, slot):
        def body(blk, c):
            for u in range(GATHER_UNROLL):
                r = blk * GATHER_UNROLL + u
                _row_copy(h_hbm, tok_ref[tile * tm + r], buf, slot, r, sem).start(priority=u % 2)
            return c
        lax.fori_loop(0, tm // GATHER_UNROLL, body, 0)

    @pl.when(t == 0)
    def _():
        issue(0, 0)

    @pl.when(t + 1 < n_t)
    def _():
        issue(t + 1, (t + 1) % 2)

    slot = t % 2
    _wait_slot(h_hbm, buf, slot, sem)
    o_ref[...] = _rms(buf[slot], g_ref[...]).astype(o_ref.dtype)


def _dispatch(h, g, slot_token, tm):
    s = slot_token.shape[0]
    d = h.shape[1]
    grid_spec = pltpu.PrefetchScalarGridSpec(
        num_scalar_prefetch=1,
        grid=(s // tm,),
        in_specs=[pl.BlockSpec(memory_space=pl.ANY), pl.BlockSpec((1, d), lambda t, tok: (0, 0))],
        out_specs=pl.BlockSpec((tm, d), lambda t, tok: (t, 0)),
        scratch_shapes=[pltpu.VMEM((2, tm, d), F32), pltpu.SemaphoreType.DMA((2,))],
    )
    return pl.pallas_call(
        functools.partial(_dispatch_kernel, tm=tm), grid_spec=grid_spec,
        out_shape=jax.ShapeDtypeStruct((s, d), BF16),
        compiler_params=_params("arbitrary"),
        name="moe_dispatch",
    )(slot_token, h, g.reshape(1, d))


def _combine_kernel(slot_ref, y_hbm, route_ref, h_ref, gpost_ref, *rest, tm, first_sample_tile):
    out_refs, (buf, sem) = rest[:-2], rest[-2:]
    t = pl.program_id(0)
    n_t = pl.num_programs(0)

    def issue(tile, slot):
        def body(blk, c):
            for u in range(GATHER_UNROLL // TOP_K):
                r = blk * (GATHER_UNROLL // TOP_K) + u
                for k in range(TOP_K):
                    _row_copy(y_hbm, slot_ref[k, tile * tm + r], buf, slot, k * tm + r, sem).start(priority=k % 2)
            return c
        lax.fori_loop(0, tm // (GATHER_UNROLL // TOP_K), body, 0)

    @pl.when(t == 0)
    def _():
        issue(0, 0)

    @pl.when(t + 1 < n_t)
    def _():
        issue(t + 1, (t + 1) % 2)

    slot = t % 2
    _wait_slot(y_hbm, buf, slot, sem)
    route = route_ref[...]
    mix = route[:, 0:1] * buf[slot, 0:tm, :]
    for k in range(1, TOP_K):
        mix = mix + route[:, k:k + 1] * buf[slot, k * tm:(k + 1) * tm, :]
    result = h_ref[...] + _rms(mix, gpost_ref[...])
    if len(out_refs) == 1:
        out_refs[0][...] = result
    else:
        @pl.when(t < first_sample_tile)
        def _():
            out_refs[0][...] = result

        @pl.when(t >= first_sample_tile)
        def _():
            out_refs[1][...] = result


def _combine(y_sorted, token_slots, route, h, g_post, tm, split_rows=None):
    m, d = h.shape
    if split_rows is None:
        first_sample_tile = 0
        out_specs = pl.BlockSpec((tm, d), lambda t, sl: (t, 0))
        out_shape = jax.ShapeDtypeStruct((m, d), F32)
    else:
        assert split_rows % tm == 0 and 0 < split_rows < m
        first_sample_tile = split_rows // tm
        out_specs = [pl.BlockSpec((tm, d), lambda t, sl: (jnp.minimum(t, first_sample_tile - 1), 0)),
                     pl.BlockSpec((tm, d), lambda t, sl: (jnp.maximum(t - first_sample_tile, 0), 0))]
        out_shape = [jax.ShapeDtypeStruct((split_rows, d), F32), jax.ShapeDtypeStruct((m - split_rows, d), F32)]
    grid_spec = pltpu.PrefetchScalarGridSpec(
        num_scalar_prefetch=1,
        grid=(m // tm,),
        in_specs=[pl.BlockSpec(memory_space=pl.ANY),
                  pl.BlockSpec((tm, LANES), lambda t, sl: (t, 0)),
                  pl.BlockSpec((tm, d), lambda t, sl: (t, 0)),
                  pl.BlockSpec((1, d), lambda t, sl: (0, 0))],
        out_specs=out_specs,
        scratch_shapes=[pltpu.VMEM((2, TOP_K * tm, d), F32), pltpu.SemaphoreType.DMA((2,))],
    )
    return pl.pallas_call(
        functools.partial(_combine_kernel, tm=tm, first_sample_tile=first_sample_tile), grid_spec=grid_spec,
        out_shape=out_shape,
        compiler_params=_params("arbitrary"),
        name="moe_combine",
    )(token_slots, y_sorted, route, h, g_post.reshape(1, d))


def _routing_tables(route, n_experts, tm):
    m = route.shape[0]
    eid = jnp.concatenate([route[:, 2 + k].astype(I32) for k in range(TOP_K)])
    blk = LANES
    assert (TOP_K * m) % blk == 0 and TOP_K * m < 2 ** 24
    onehot = (eid[:, None] == jnp.arange(n_experts, dtype=I32)[None, :]).astype(F32).reshape(-1, blk, n_experts)
    before = jnp.tril(jnp.ones((blk, blk), F32), -1)
    within = jnp.einsum("ts,bse->bte", before, onehot)
    block_total = jnp.sum(onehot, axis=1)
    block_start = jnp.cumsum(block_total, axis=0) - block_total
    rank = jnp.sum((within + block_start[:, None, :]) * onehot, axis=-1).reshape(-1).astype(I32)
    counts = jnp.sum(block_total, axis=0).astype(I32)
    tiles_e = (counts + tm - 1) // tm
    tile_end = jnp.cumsum(tiles_e)
    start = (tile_end - tiles_e) * tm
    slot = start[eid] + rank
    n_tiles = (TOP_K * m + n_experts * (tm - 1)) // tm
    token = jnp.tile(jnp.arange(m, dtype=I32), TOP_K)
    slot_token = (jnp.arange(n_tiles * tm, dtype=I32) % m).at[slot].set(token)
    n_valid = tile_end[-1].astype(I32)
    tile_id = jnp.arange(n_tiles, dtype=I32)
    tile_src = jnp.minimum(tile_id, n_valid - 1)
    tile_expert = jnp.minimum(jnp.sum((tile_end[None, :] <= tile_src[:, None]).astype(I32), axis=1), n_experts - 1)
    following, cur = [], jnp.int32(-1)
    for e in reversed(range(n_experts)):
        following.append(cur)
        cur = jnp.where(tiles_e[e] > 0, jnp.int32(e), cur)
    following = jnp.stack(following[::-1])
    wraps = following < 0
    following = jnp.where(wraps, cur, following)
    rows_in_last = counts - (tiles_e - 1) * tm
    is_last = tile_src == tile_end[tile_expert] - 1
    part = tm // TILE_PARTS
    parts = jnp.where(is_last, (rows_in_last[tile_expert] + part - 1) // part, TILE_PARTS).astype(I32)
    tiles = (tile_expert, tile_src, n_valid.reshape(1), following[tile_expert], wraps[tile_expert].astype(I32), parts)
    return slot_token, slot.reshape(TOP_K, m).astype(I32), tiles


def _pool_prompt_kernel(halo_ref, p_ref, w_ref, s_ref, o_ref, buf_ref, *, tiles_per_seq):
    i = pl.program_id(0) % tiles_per_seq
    t = p_ref.shape[0]
    gw = w_ref.shape[1]
    buf_ref[0:POOL_HALO, :] = jnp.where(i == 0, 0.0, halo_ref[...])
    buf_ref[POOL_HALO:POOL_HALO + t, :] = p_ref[...]
    pos = i * t + lax.broadcasted_iota(I32, (t, 1), 0)
    for g, w in enumerate(POOL_WINDOWS):
        c0 = g * gw
        cur = buf_ref[POOL_HALO:POOL_HALO + t, c0:c0 + gw]
        acc = cur
        for j in range(1, w):
            acc = acc + buf_ref[POOL_HALO - j:POOL_HALO - j + t, c0:c0 + gw]
        cnt = jnp.minimum(pos + 1, w).astype(F32)
        d = acc / cnt - cur
        y = jnp.dot(d.astype(BF16), w_ref[g], preferred_element_type=F32)
        o_ref[:, c0:c0 + gw] = (y * s_ref[:, c0:c0 + gw]).astype(o_ref.dtype)


def _pool_prompt(proj, n_rows, seq, w_pool, scale, t):
    c = scale.shape[0]
    halo_per_tile = t // POOL_HALO
    return pl.pallas_call(
        functools.partial(_pool_prompt_kernel, tiles_per_seq=seq // t),
        grid=(n_rows // t,),
        in_specs=[pl.BlockSpec((POOL_HALO, c), lambda i: (jnp.maximum(i * halo_per_tile - 1, 0), 0)),
                  pl.BlockSpec((t, c), lambda i: (i, 0)),
                  pl.BlockSpec(w_pool.shape, lambda i: (0, 0, 0)),
                  pl.BlockSpec((1, c), lambda i: (0, 0))],
        out_specs=pl.BlockSpec((t, c), lambda i: (i, 0)),
        out_shape=jax.ShapeDtypeStruct((n_rows, c), BF16),
        scratch_shapes=[pltpu.VMEM((POOL_HALO + t, c), F32)],
        compiler_params=_params("parallel"),
        name="pool_prompt",
    )(proj, proj, w_pool, scale.reshape(1, c))


def _pool_sample_kernel(ctx_ref, w_ref, s_ref, o_ref, *, n_ctx, pos0):
    gw = w_ref.shape[1]
    for t in range(o_ref.shape[0]):
        hi = n_ctx + t + 1
        for g, w in enumerate(POOL_WINDOWS):
            c0 = g * gw
            lo = max(hi - w, 0)
            acc = ctx_ref[lo, :, c0:c0 + gw]
            for r in range(lo + 1, hi):
                acc = acc + ctx_ref[r, :, c0:c0 + gw]
            d = acc / float(min(pos0 + t + 1, w)) - ctx_ref[hi - 1, :, c0:c0 + gw]
            y = jnp.dot(d.astype(BF16), w_ref[g], preferred_element_type=F32)
            o_ref[t, :, c0:c0 + gw] = (y * s_ref[:, c0:c0 + gw]).astype(o_ref.dtype)


def _pool_sample(ctx_t, n_ctx, pos0, w_pool, scale):
    total, n, c = ctx_t.shape
    return pl.pallas_call(
        functools.partial(_pool_sample_kernel, n_ctx=n_ctx, pos0=pos0),
        out_shape=jax.ShapeDtypeStruct((total - n_ctx, n, c), BF16),
        compiler_params=pltpu.CompilerParams(vmem_limit_bytes=VMEM_LIMIT_BYTES),
        name="pool_sample",
    )(ctx_t, w_pool, scale.reshape(1, c))


def _gmlp_kernel(u_ref, v_ref, w_ref, b_ref, o_ref):
    t = u_ref.shape[0]
    n_heads = w_ref.shape[0]
    ck = w_ref.shape[1]
    row = lax.broadcasted_iota(I32, (ck, ck), 0)
    col = lax.broadcasted_iota(I32, (ck, ck), 1)
    for h in range(n_heads):
        c0 = h * HEAD_DIM
        wm = jnp.where(col <= row, w_ref[h], 0.0).astype(BF16)
        bias = b_ref[:, h:h + 1]
        for k in range(t // ck):
            r0 = k * ck
            v = v_ref[r0:r0 + ck, c0:c0 + HEAD_DIM].astype(BF16)
            mixed = jnp.dot(wm, v, preferred_element_type=F32) + bias
            o_ref[r0:r0 + ck, c0:c0 + HEAD_DIM] = (u_ref[r0:r0 + ck, c0:c0 + HEAD_DIM] * mixed).astype(o_ref.dtype)


def _gmlp(proj, n_prompt_rows, w_pair, b_pair, t):
    m = proj.shape[0]
    c = proj.shape[1] // 3
    n_heads = w_pair.shape[1]
    first_sample_tile = n_prompt_rows // t
    which = lambda i: jnp.where(i >= first_sample_tile, 1, 0)
    return pl.pallas_call(
        _gmlp_kernel,
        grid=(m // t,),
        in_specs=[pl.BlockSpec((t, c), lambda i: (i, 1)),
                  pl.BlockSpec((t, c), lambda i: (i, 2)),
                  pl.BlockSpec((None, n_heads, GMLP_CHUNK, GMLP_CHUNK), lambda i: (which(i), 0, 0, 0)),
                  pl.BlockSpec((None, GMLP_CHUNK, n_heads), lambda i: (which(i), 0, 0))],
        out_specs=pl.BlockSpec((t, c), lambda i: (i, 0)),
        out_shape=jax.ShapeDtypeStruct((m, c), BF16),
        compiler_params=_params("parallel"),
        name="gmlp",
    )(proj, proj, w_pair, b_pair)


def _conv_prompt_kernel(xc_ref, bg_ref, cg_ref, xch_ref, cgh_ref, w_ref, o_ref, tail_ref, buf_ref, *, tiles_per_seq):
    i = pl.program_id(0) % tiles_per_seq
    t = xc_ref.shape[0]
    z = cg_ref[...] * xc_ref[...]
    buf_ref[0:CONV_HALO, :] = jnp.where(i == 0, 0.0, cgh_ref[...] * xch_ref[...])
    buf_ref[CONV_HALO:CONV_HALO + t, :] = z
    y = w_ref[0:1, :] * buf_ref[CONV_HALO - 2:CONV_HALO - 2 + t, :]
    for j in range(1, CONV_WIDTH):
        y = y + w_ref[j:j + 1, :] * buf_ref[CONV_HALO - 2 + j:CONV_HALO - 2 + j + t, :]
    o_ref[...] = (bg_ref[...] * y).astype(o_ref.dtype)
    tail_ref[...] = z[t - CONV_HALO:, :]


def _conv_prompt(proj, n_rows, seq, conv_w, t):
    c = conv_w.shape[1]
    halo_per_tile = t // CONV_HALO
    halo_row = lambda i: jnp.maximum(i * halo_per_tile - 1, 0)
    return pl.pallas_call(
        functools.partial(_conv_prompt_kernel, tiles_per_seq=seq // t),
        grid=(n_rows // t,),
        in_specs=[pl.BlockSpec((t, c), lambda i: (i, 0)),
                  pl.BlockSpec((t, c), lambda i: (i, 1)),
                  pl.BlockSpec((t, c), lambda i: (i, 2)),
                  pl.BlockSpec((CONV_HALO, c), lambda i: (halo_row(i), 0)),
                  pl.BlockSpec((CONV_HALO, c), lambda i: (halo_row(i), 2)),
                  pl.BlockSpec((CONV_WIDTH, c), lambda i: (0, 0))],
        out_specs=[pl.BlockSpec((t, c), lambda i: (i, 0)),
                   pl.BlockSpec((CONV_HALO, c), lambda i: (i, 0))],
        out_shape=[jax.ShapeDtypeStruct((n_rows, c), BF16),
                   jax.ShapeDtypeStruct((n_rows // t * CONV_HALO, c), F32)],
        scratch_shapes=[pltpu.VMEM((CONV_HALO + t, c), F32)],
        compiler_params=_params("parallel"),
        name="conv_prompt",
    )(proj, proj, proj, proj, proj, conv_w)


def _conv_sample_kernel(ctx_ref, xc_ref, bg_ref, cg_ref, w_ref, o_ref, z_ref):
    n_ctx = ctx_ref.shape[0]
    length = xc_ref.shape[0]
    rows = [ctx_ref[r] for r in range(n_ctx)]
    for t in range(length):
        z = cg_ref[t] * xc_ref[t]
        z_ref[t] = z
        rows.append(z)
    for t in range(length):
        y = w_ref[0:1, :] * rows[t]
        for j in range(1, CONV_WIDTH):
            y = y + w_ref[j:j + 1, :] * rows[t + j]
        o_ref[t] = (bg_ref[t] * y).astype(o_ref.dtype)


def _conv_sample(ctx_t, xc_t, bg_t, cg_t, conv_w):
    length, n, c = xc_t.shape
    return pl.pallas_call(
        _conv_sample_kernel,
        out_shape=[jax.ShapeDtypeStruct((length, n, c), BF16), jax.ShapeDtypeStruct((length, n, c), F32)],
        compiler_params=pltpu.CompilerParams(vmem_limit_bytes=VMEM_LIMIT_BYTES),
        name="conv_sample",
    )(ctx_t, xc_t, bg_t, cg_t, conv_w)


def _log_sigmoid(x):
    return jnp.minimum(x, 0.0) - jnp.log1p(jnp.exp(-jnp.abs(x)))


def _logf_kernel(f_ref, b_ref, lf_ref, c_ref, carry_ref, *, blocks_per_seq):
    i = pl.program_id(0)
    blk = f_ref.shape[0]
    lf = _log_sigmoid(f_ref[...] + b_ref[...])
    lf_ref[...] = lf

    @pl.when(i % blocks_per_seq == 0)
    def _():
        carry_ref[...] = jnp.zeros_like(carry_ref)

    row = lax.broadcasted_iota(I32, (blk, blk), 0)
    col = lax.broadcasted_iota(I32, (blk, blk), 1)
    tri = jnp.where(col <= row, 1.0, 0.0).astype(F32)
    c = jnp.dot(tri, lf, precision=HIGHEST, preferred_element_type=F32) + carry_ref[0:1, :]
    c_ref[...] = c
    carry_ref[...] = jnp.broadcast_to(c[blk - 1:blk, :], carry_ref.shape)


def _logf_cumsum(f_logit, b_forget, seq, blk):
    m = f_logit.shape[0]
    return pl.pallas_call(
        functools.partial(_logf_kernel, blocks_per_seq=seq // blk),
        grid=(m // blk,),
        in_specs=[pl.BlockSpec((blk, LANES), lambda i: (i, 0)), pl.BlockSpec((1, LANES), lambda i: (0, 0))],
        out_specs=[pl.BlockSpec((blk, LANES), lambda i: (i, 0)), pl.BlockSpec((blk, LANES), lambda i: (i, 0))],
        out_shape=[jax.ShapeDtypeStruct((m, LANES), F32), jax.ShapeDtypeStruct((m, LANES), F32)],
        scratch_shapes=[pltpu.VMEM((8, LANES), F32)],
        compiler_params=_params("arbitrary"),
        name="logf_cumsum",
    )(f_logit, b_forget)


FLASH_HEADS = 2


def _flash_kernel(q_ref, k_ref, v_ref, cc_ref, cr_ref, o_ref, k_bf, v_bf, *, tq):
    hp = pl.program_id(1)
    qb = pl.program_id(2)

    @pl.when(qb == 0)
    def _():
        k_bf[...] = k_ref[...].astype(BF16)
        v_bf[...] = v_ref[...].astype(BF16)

    lane = lax.broadcasted_iota(I32, (tq, LANES), 1)
    heads = []
    for hh in range(FLASH_HEADS):
        cols = slice(hh * HEAD_DIM, (hh + 1) * HEAD_DIM)
        q = (q_ref[:, cols] * (1.0 / math.sqrt(HEAD_DIM))).astype(BF16)
        cq = jnp.sum(jnp.where(lane == hp * FLASH_HEADS + hh, cc_ref[...], 0.0), axis=-1, keepdims=True)
        heads.append((cols, q, cq))

    def scores(hh, j):
        cols, q, cq = heads[hh]
        start = pl.multiple_of(j * tq, tq)
        k = k_bf[pl.ds(start, tq), cols]
        v = v_bf[pl.ds(start, tq), cols]
        s = lax.dot_general(q, k, (((1,), (1,)), ((), ())), preferred_element_type=F32)
        ck = cr_ref[hh, :, pl.ds(start, tq)]
        return s + (cq - ck), v

    def update(carry, s, v):
        m, l, acc = carry
        m_new = jnp.maximum(m, jnp.max(s, axis=-1, keepdims=True))
        alpha = jnp.exp(m - m_new)
        p = jnp.exp(s - m_new)
        l = alpha * l + jnp.sum(p, axis=-1, keepdims=True)
        acc = alpha * acc + jnp.dot(p.astype(BF16), v, preferred_element_type=F32)
        return m_new, l, acc

    def body(j, carries):
        return tuple(update(carries[hh], *scores(hh, j)) for hh in range(FLASH_HEADS))

    init = (jnp.full((tq, 1), MASKED, F32), jnp.zeros((tq, 1), F32), jnp.zeros((tq, HEAD_DIM), F32))
    carries = lax.fori_loop(0, qb, body, (init,) * FLASH_HEADS)
    row = lax.broadcasted_iota(I32, (tq, tq), 0)
    col = lax.broadcasted_iota(I32, (tq, tq), 1)
    for hh in range(FLASH_HEADS):
        s, v = scores(hh, qb)
        _, l, acc = update(carries[hh], jnp.where(col <= row, s, MASKED), v)
        o_ref[:, heads[hh][0]] = (acc / l).astype(o_ref.dtype)


def _flash_prompt(proj, c_col, c_row, n_seq, seq, n_heads, tq):
    assert n_heads % FLASH_HEADS == 0
    pairs = n_heads // FLASH_HEADS
    width = FLASH_HEADS * HEAD_DIM
    q_blk, k_blk, v_blk = 3 * pairs, 4 * pairs, 5 * pairs
    nqb = seq // tq
    return pl.pallas_call(
        functools.partial(_flash_kernel, tq=tq),
        grid=(n_seq, pairs, nqb),
        in_specs=[pl.BlockSpec((tq, width), lambda n, h, i: (n * nqb + i, q_blk + h)),
                  pl.BlockSpec((seq, width), lambda n, h, i: (n, k_blk + h)),
                  pl.BlockSpec((seq, width), lambda n, h, i: (n, v_blk + h)),
                  pl.BlockSpec((tq, LANES), lambda n, h, i: (n * nqb + i, 0)),
                  pl.BlockSpec((FLASH_HEADS, 1, seq), lambda n, h, i: (h, 0, n))],
        out_specs=pl.BlockSpec((tq, width), lambda n, h, i: (n * nqb + i, h)),
        out_shape=jax.ShapeDtypeStruct((n_seq * seq, n_heads * HEAD_DIM), BF16),
        scratch_shapes=[pltpu.VMEM((seq, width), BF16), pltpu.VMEM((seq, width), BF16)],
        compiler_params=_params("arbitrary", "arbitrary", "arbitrary"),
        name="fox_prompt",
    )(proj, proj, proj, c_col, c_row)


def _decay_of_later_rows(lf, later_groups, n_heads):
    group, width = lf.shape
    lane = lax.broadcasted_iota(I32, lf.shape, 1)
    incl, total = lf, lf
    shift = n_heads
    while shift < width:
        incl = incl + jnp.where(lane + shift < width, pltpu.roll(incl, width - shift, 1), 0.0)
        total = total + pltpu.roll(total, shift, 1)
        shift *= 2
    later_pages = jnp.zeros_like(lf)
    if group > 1:
        prow = lax.broadcasted_iota(I32, (group, group), 0)
        pcol = lax.broadcasted_iota(I32, (group, group), 1)
        later = jnp.where(pcol > prow, 1.0, 0.0).astype(F32)
        later_pages = jnp.dot(later, total, precision=HIGHEST, preferred_element_type=F32)
    return (incl - lf) + later_pages + later_groups, jnp.sum(total, axis=0, keepdims=True)


def _decode_kernel(pt_ref, q_ref, kn_ref, vn_ref, lfn_ref, *refs, n_heads, n_q, group, n_par):
    pages = 3 * group * n_par
    o_ref, m_ref, l_ref, acc_ref, later_ref = refs[pages:]
    j = pl.program_id(1)
    rows = n_heads * n_q
    width = refs[0].shape[0] * n_heads

    @pl.when(j == 0)
    def _():
        m_ref[...] = jnp.full_like(m_ref, MASKED)
        l_ref[...] = jnp.zeros_like(l_ref)
        acc_ref[...] = jnp.zeros_like(acc_ref)
        later_ref[...] = jnp.zeros_like(later_ref)

    r_i = lax.broadcasted_iota(I32, (rows, LANES), 0)
    c_i = lax.broadcasted_iota(I32, (rows, LANES), 1)
    sel = (c_i % n_heads == r_i // n_q) & (c_i // n_heads <= r_i % n_q) & (c_i < n_q * n_heads)
    r_w = lax.broadcasted_iota(I32, (rows, group * width), 0)
    c_w = lax.broadcasted_iota(I32, (rows, group * width), 1)
    same_head = c_w % n_heads == r_w // n_q

    def update(state, s, v):
        m_old, l, acc = state
        m_new = jnp.maximum(m_old, jnp.max(s, axis=-1, keepdims=True))
        alpha = jnp.exp(m_old - m_new)
        p = jnp.exp(s - m_new)
        l = alpha * l + jnp.sum(p, axis=-1, keepdims=True)
        acc = alpha * acc + jnp.dot(p.astype(BF16), v, preferred_element_type=F32)
        return m_new, l, acc

    def load_state(b):
        return m_ref[b, :, 0:1], l_ref[b, :, 0:1], acc_ref[b]

    def store_state(b, state):
        m, l, acc = state
        m_ref[b] = jnp.broadcast_to(m, m_ref.shape[1:])
        l_ref[b] = jnp.broadcast_to(l, l_ref.shape[1:])
        acc_ref[b] = acc

    states = [load_state(b) for b in range(n_par)]
    later = [later_ref[b] for b in range(n_par)]
    queries = []
    for b in range(n_par):
        base = 3 * group * b
        k_refs, v_refs = refs[base:base + group], refs[base + group:base + 2 * group]
        lf_refs = refs[base + 2 * group:base + 3 * group]
        lf = jnp.concatenate([r[...] for r in lf_refs], axis=0)
        decay, group_total = _decay_of_later_rows(lf, later[b][0:1, :], n_heads)
        later[b] = later[b] + group_total

        q = (q_ref[b] * (1.0 / math.sqrt(HEAD_DIM))).astype(BF16)
        lfn = lfn_ref[b]
        cq = jnp.sum(jnp.where(sel, jnp.broadcast_to(lfn, (rows, LANES)), 0.0), axis=-1, keepdims=True)
        queries.append((q, lfn, cq))

        k = jnp.concatenate([r[...].reshape(width, HEAD_DIM).astype(BF16) for r in k_refs], axis=0)
        v = jnp.concatenate([r[...].reshape(width, HEAD_DIM).astype(BF16) for r in v_refs], axis=0)
        decay_row = jnp.concatenate([decay[p:p + 1, :] for p in range(group)], axis=1)
        s = lax.dot_general(q, k, (((1,), (1,)), ((), ())), preferred_element_type=F32)
        s = jnp.where(same_head, s + decay_row + cq, MASKED)
        states[b] = update(states[b], s, v)
    for b in range(n_par):
        store_state(b, states[b])
        later_ref[b] = later[b]

    @pl.when(j == pl.num_programs(1) - 1)
    def _():
        for b in range(n_par):
            q, lfn, cq = queries[b]
            kn = kn_ref[b].astype(BF16)
            vn = vn_ref[b].astype(BF16)
            sn = lax.dot_general(q, kn, (((1,), (1,)), ((), ())), preferred_element_type=F32)
            lf8 = jnp.broadcast_to(lfn, (8, LANES))
            lane8 = lax.broadcasted_iota(I32, (8, LANES), 1)
            csum = lf8
            shift = n_heads
            while shift < n_q * n_heads:
                csum = csum + jnp.where(lane8 >= shift, pltpu.roll(csum, shift, 1), 0.0)
                shift *= 2
            _, l, acc = update(states[b], jnp.where(sel, sn + (cq - csum[0:1, :]), MASKED), vn)
            o_ref[b] = acc / l


def _decode_attention(q_hq, cache_k, cache_v, logf_pages, layer, page_table, k_new, v_new, lf_new, n_heads, n_q):
    n, n_pages = page_table.shape
    assert n_pages >= 1
    page = cache_k.shape[2]
    rows = n_heads * n_q
    width = page * n_heads
    group = max(g for g in (1, 2, 4, 8) if n_pages % g == 0)
    n_groups = n_pages // group
    n_par = 2 if n % 2 == 0 else 1

    def physical_page(i, j, pt, b, p):
        return pt[(i * n_par + b) * n_pages + (n_groups - 1 - j) * group + p]

    def page_spec(b, p):
        return pl.BlockSpec((None, None, page, n_heads, HEAD_DIM),
                            lambda i, j, pt: (layer, physical_page(i, j, pt, b, p), 0, 0, 0))

    def logf_spec(b, p):
        return pl.BlockSpec((None, 1, width), lambda i, j, pt: (physical_page(i, j, pt, b, p), 0, 0))

    per_seq = lambda shape: pl.BlockSpec((n_par,) + shape, lambda i, j, pt: (i, 0, 0))
    page_specs, page_args = [], []
    for b in range(n_par):
        page_specs += [page_spec(b, p) for p in range(group)] * 2 + [logf_spec(b, p) for p in range(group)]
        page_args += [cache_k] * group + [cache_v] * group + [logf_pages] * group
    grid_spec = pltpu.PrefetchScalarGridSpec(
        num_scalar_prefetch=1,
        grid=(n // n_par, n_groups),
        in_specs=[per_seq((rows, HEAD_DIM)), per_seq((LANES, HEAD_DIM)), per_seq((LANES, HEAD_DIM)),
                  per_seq((1, LANES))] + page_specs,
        out_specs=per_seq((rows, HEAD_DIM)),
        scratch_shapes=[pltpu.VMEM((n_par, rows, LANES), F32), pltpu.VMEM((n_par, rows, LANES), F32),
                        pltpu.VMEM((n_par, rows, HEAD_DIM), F32), pltpu.VMEM((n_par, 8, width), F32)],
    )
    return pl.pallas_call(
        functools.partial(_decode_kernel, n_heads=n_heads, n_q=n_q, group=group, n_par=n_par),
        grid_spec=grid_spec,
        out_shape=jax.ShapeDtypeStruct((n, rows, HEAD_DIM), F32),
        compiler_params=_params("parallel", "arbitrary"),
        name="fox_decode",
    )(page_table.reshape(-1), q_hq, k_new, v_new, lf_new, *page_args)


def kernel(x_prompt, x_sample, state_pool, state_conv, cache_k, cache_v, cache_logf, page_table, e_norm_mix_pre, e_norm_mix_post, e_norm_ffn_pre, e_norm_ffn_post, e_w_in, e_w_pool, e_pool_scale, e_w_spatial, e_b_spatial, e_w_out, e_ffn_gate, e_ffn_up, e_ffn_down, o_norm_mix_pre, o_norm_mix_post, o_norm_ffn_pre, o_norm_ffn_post, o_w_in, o_conv_w, o_b_forget, o_w_out, o_w_router, o_exp_gate, o_exp_up, o_exp_down):
    n_p, seq, d = x_prompt.shape
    n_s, dec, _ = x_sample.shape
    mp, ms = n_p * seq, n_s * dec
    m = mp + ms
    c = d // 2
    n_heads = c // HEAD_DIM
    n_pages = page_table.shape[1]
    page = cache_k.shape[2]
    past_len = n_pages * page
    n_layers = e_w_in.shape[0] + o_w_in.shape[0]
    n_experts = o_w_router.shape[2]
    assert ms % GMLP_CHUNK == 0 and seq % GMLP_CHUNK == 0 and past_len % GMLP_CHUNK == 0
    assert GMLP_CHUNK % min(dec, GMLP_CHUNK) == 0 and dec % min(dec, GMLP_CHUNK) == 0
    assert dec * n_heads <= LANES and dec >= CONV_WIDTH - 1

    tm = _tile(math.gcd(mp, ms), 512, LANES)
    t_seq = _tile(math.gcd(seq, tm), 512, LANES)
    tq = _tile(seq, 512, LANES)
    tm_moe = 512
    dense_tiles = _dense_tiles(m // tm)

    h = (x_prompt.reshape(mp, d), x_sample.reshape(ms, d))
    xn = _norm_cast(h, e_norm_mix_pre[0], tm)
    out_rows = None

    pool_p, pool_s, gv_s, conv_p, conv_s = [], [], [], [], []
    kp_l, vp_l, lp_l, ks_l, vs_l, ls_l = [], [], [], [], [], []

    def time_major(x2d):
        return jnp.transpose(x2d.reshape(n_s, dec, -1), (1, 0, 2))

    def row_major(x3d):
        return jnp.transpose(x3d, (1, 0, 2)).reshape(ms, -1)

    for layer in range(n_layers):
        i = layer // 2
        if layer % 2 == 0:
            g_next = e_norm_ffn_pre[i]
            proj = _mm(xn, e_w_in, 3 * c, tm, _tile(3 * c, 1024, LANES), w_layer=i)
            p_s = proj[mp:, :c].reshape(n_s, dec, c)
            ctx = jnp.concatenate([state_pool[i].astype(F32), p_s], axis=1)
            n_ctx = state_pool.shape[2]
            w_pool = e_w_pool[i].astype(BF16)
            a_p = _pool_prompt(proj, mp, seq, w_pool, e_pool_scale[i], t_seq)
            a_s = row_major(_pool_sample(jnp.transpose(ctx, (1, 0, 2)), n_ctx, past_len, w_pool, e_pool_scale[i]))
            cl = min(dec, GMLP_CHUNK)
            reps = GMLP_CHUNK // cl
            w_samp = jnp.einsum("ab,hts->hatbs", jnp.eye(reps, dtype=F32),
                                e_w_spatial[i][:, :cl, :cl]).reshape(n_heads, GMLP_CHUNK, GMLP_CHUNK)
            b_samp = jnp.tile(e_b_spatial[i][:, :cl], (1, reps))
            w_pair = jnp.stack([e_w_spatial[i], w_samp])
            b_pair = jnp.stack([e_b_spatial[i].T, b_samp.T])
            b = _gmlp(proj, mp, w_pair, b_pair, tm)
            h, xn = _proj_out((a_p, a_s), (b,), e_w_out[i].astype(BF16), h, e_norm_mix_post[i], g_next, tm, mp)
            g_after = o_norm_mix_pre[i] if layer + 1 < n_layers else jnp.ones((d,), F32)
            mix = _swiglu_grouped(xn, e_ffn_gate[i][None], e_ffn_up[i][None], e_ffn_down[i][None], dense_tiles, tm)
            h, xn = _norm_residual(mix, h, e_norm_ffn_post[i], g_after, tm)
            h = (h,)
            pool_p.append(jnp.stack([proj[(s + 1) * seq - n_ctx:(s + 1) * seq, :c] for s in range(n_p)]))
            pool_s.append(ctx[:, ctx.shape[1] - n_ctx:])
            gv_s.append(proj[mp:, 2 * c:].reshape(n_s, dec, c))
        else:
            w_in = o_w_in[i]
            proj, k_p, v_p = _mm(xn, o_w_in, 6 * c, tm, c, copy_cols=(4, 5), copy_rows=mp, w_layer=i)
            w_f = jnp.zeros((d, LANES), F32).at[:, :n_heads].set(w_in[:, 6 * c:])
            f_logit = _mm(xn, w_f, LANES, tm, LANES)
            b_f = jnp.zeros((1, LANES), F32).at[0, :n_heads].set(o_b_forget[i])
            logf, csum = _logf_cumsum(f_logit, b_f, seq, _tile(math.gcd(seq, m), 512, LANES))
            c_p, z_tail = _conv_prompt(proj, mp, seq, o_conv_w[i], t_seq)
            z_tail = z_tail.reshape(n_p, seq // t_seq, CONV_HALO, c)[:, -1, CONV_HALO - (CONV_WIDTH - 1):]
            xs = proj[mp:]
            conv_ctx = jnp.transpose(state_conv[i].astype(F32), (1, 0, 2))
            c_s, z_s = _conv_sample(conv_ctx, time_major(xs[:, :c]), time_major(xs[:, c:2 * c]),
                                    time_major(xs[:, 2 * c:3 * c]), o_conv_w[i])
            z_all = jnp.concatenate([conv_ctx, z_s], axis=0)
            c_row = csum[:mp, :n_heads].T.reshape(n_heads, 1, mp)
            att_p = _flash_prompt(proj, csum, c_row, n_p, seq, n_heads, tq)
            k_s = xs[:, 4 * c:5 * c]
            v_s = xs[:, 5 * c:6 * c]
            q_hq = jnp.transpose(xs[:, 3 * c:4 * c].reshape(n_s, dec, n_heads, HEAD_DIM), (0, 2, 1, 3))
            q_hq = q_hq.reshape(n_s, n_heads * dec, HEAD_DIM)
            pad_rows = LANES - dec * n_heads
            k_new = jnp.pad(k_s.reshape(n_s, dec * n_heads, HEAD_DIM), ((0, 0), (0, pad_rows), (0, 0)))
            v_new = jnp.pad(v_s.reshape(n_s, dec * n_heads, HEAD_DIM), ((0, 0), (0, pad_rows), (0, 0)))
            lf_new = jnp.pad(logf[mp:, :n_heads].reshape(n_s, 1, dec * n_heads), ((0, 0), (0, 0), (0, pad_rows)))
            logf_pages = cache_logf[i].astype(F32).reshape(cache_logf.shape[1], 1, page * n_heads)
            att_s = _decode_attention(q_hq, cache_k, cache_v, logf_pages, i, page_table, k_new, v_new, lf_new,
                                      n_heads, dec)
            att_s = jnp.transpose(att_s.reshape(n_s, n_heads, dec, HEAD_DIM), (0, 2, 1, 3)).reshape(ms, c)
            h, _, route = _proj_out((c_p, row_major(c_s)), (att_p, att_s.astype(BF16)), o_w_out[i].astype(BF16), h,
                                    o_norm_mix_post[i], o_norm_ffn_pre[i], tm, mp, w_router=o_w_router[i])
            slot_token, token_slots, expert_tiles = _routing_tables(route, n_experts, tm_moe)
            x_sorted = _dispatch(h, o_norm_ffn_pre[i], slot_token, tm_moe)
            y_sorted = _swiglu_grouped(x_sorted, o_exp_gate[i], o_exp_up[i], o_exp_down[i], expert_tiles, tm_moe)
            t_comb = _tile(math.gcd(mp, ms), 256)
            if layer + 1 < n_layers:
                h = (_combine(y_sorted, token_slots, route, h, o_norm_ffn_post[i], t_comb),)
                xn = _norm_cast(h, e_norm_mix_pre[i + 1], tm)
            else:
                out_rows = _combine(y_sorted, token_slots, route, h, o_norm_ffn_post[i], t_comb, split_rows=mp)
            conv_p.append(z_tail)
            conv_s.append(jnp.transpose(z_all[z_all.shape[0] - (CONV_WIDTH - 1):], (1, 0, 2)))
            kp_l.append(k_p.reshape(n_p, seq, n_heads, HEAD_DIM))
            vp_l.append(v_p.reshape(n_p, seq, n_heads, HEAD_DIM))
            lp_l.append(logf[:mp, :n_heads].reshape(n_p, seq, n_heads))
            ks_l.append(k_s.reshape(n_s, dec, n_heads, HEAD_DIM))
            vs_l.append(v_s.reshape(n_s, dec, n_heads, HEAD_DIM))
            ls_l.append(logf[mp:, :n_heads].reshape(n_s, dec, n_heads))

    if out_rows is None:
        out_rows = (h[0][:mp], h[0][mp:])
    return (out_rows[0].reshape(n_p, seq, d), out_rows[1].reshape(n_s, dec, d),
            jnp.stack(pool_p), jnp.stack(pool_s), jnp.stack(gv_s), jnp.stack(conv_p), jnp.stack(conv_s),
            jnp.stack(kp_l), jnp.stack(vp_l), jnp.stack(lp_l), jnp.stack(ks_l), jnp.stack(vs_l), jnp.stack(ls_l))
```
